```python
import math
import jax, jax.numpy as jnp
from jax import lax
import numpy as np

D_MODEL = 1024
BATCH = 8
SEQ = 4096
DEPTH = 4

SSM_WIDTH = D_MODEL // 2
SSM_GROUP = 16
SSM_GROUPS = SSM_WIDTH // SSM_GROUP
SSM_STATE = 64
ATTN_WIDTH = D_MODEL - SSM_WIDTH
ATTN_HEAD_DIM = 64
ATTN_HEADS = ATTN_WIDTH // (2 * ATTN_HEAD_DIM)
IN_COLS = SSM_WIDTH + 3 * ATTN_WIDTH
D_FF = 2752
N_EXPERTS = 8
TOP_K = 2
D_FF_EXPERT = 3584
N_DENSE = (DEPTH + 1) // 2
N_MOE = DEPTH // 2
Q_BLOCK = 128
NORM_EPS = 1e-6
STEP_MIN = 1e-3
STEP_MAX = 1e-1

kernel_name = "hymba_style_s5_diffattn_moe_encoder"


def rmsnorm(x, g):
    x32 = x.astype(jnp.float32)
    y = x32 * lax.rsqrt(jnp.mean(x32 * x32, axis=-1, keepdims=True) + NORM_EPS)
    return (y * g.astype(jnp.float32)).astype(x.dtype)


def swiglu(h, w_gate, w_up, w_down):
    return (jax.nn.silu(h @ w_gate) * (h @ w_up)) @ w_down


def alibi_slopes(n_heads):
    return 2.0 ** (-8.0 * jnp.arange(1, n_heads + 1, dtype=jnp.float32) / n_heads)


def _scan_combine(e1, e2):
    a1, b1 = e1
    a2, b2 = e2
    return a1 * a2, a2 * b1 + b2


def ssm_direction(u_c, lam_re, lam_im, log_step, b_re, b_im, c_re, c_im, reverse):
    f32 = jnp.float32
    lam = lax.complex(lam_re.astype(f32), lam_im.astype(f32))
    step = jnp.exp(log_step.astype(f32))[:, None]
    lam_bar = jnp.exp(lam * step)
    b = lax.complex(b_re.astype(f32), b_im.astype(f32))
    b_bar = ((lam_bar - 1.0) / lam)[:, :, None] * b
    c = lax.complex(c_re.astype(f32), c_im.astype(f32))
    bu = jnp.einsum('blgh,gph->blgp', u_c, b_bar)
    a = jnp.broadcast_to(lam_bar, (1, bu.shape[1]) + lam_bar.shape)
    _, states = lax.associative_scan(_scan_combine, (a, bu), reverse=reverse, axis=1)
    return jnp.real(jnp.einsum('blgp,ghp->blgh', states, c))


def ssm_mixer(u, lam_re, lam_im, log_step, b_re, b_im, c_re, c_im, d_skip, w_glu, b_glu, g_out):
    bsz, seq, _ = u.shape
    u32 = u.astype(jnp.float32)
    u_c = u32.reshape(bsz, seq, SSM_GROUPS, SSM_GROUP).astype(jnp.complex64)
    y = u32 * d_skip.astype(jnp.float32)
    for direction, reverse in ((0, False), (1, True)):
        y = y + ssm_direction(u_c, lam_re[direction], lam_im[direction], log_step[direction],
                              b_re[direction], b_im[direction], c_re[direction], c_im[direction],
                              reverse).reshape(bsz, seq, SSM_WIDTH)
    y = jax.nn.gelu(y).astype(u.dtype)
    y = y * jax.nn.sigmoid(y @ w_glu + b_glu)
    return rmsnorm(y, g_out)


def diff_attention(q, k, v, lq1, lk1, lq2, lk2, g_subln, lambda_init):
    f32 = jnp.float32
    bsz, seq, _ = q.shape
    scale = ATTN_HEAD_DIM ** -0.5
    qh = q.astype(f32).reshape(bsz, seq, ATTN_HEADS, 2, ATTN_HEAD_DIM) * scale
    kh = k.astype(f32).reshape(bsz, seq, ATTN_HEADS, 2, ATTN_HEAD_DIM)
    vh = v.astype(f32).reshape(bsz, seq, ATTN_HEADS, 2 * ATTN_HEAD_DIM)
    k1, k2 = kh[..., 0, :], kh[..., 1, :]
    lam = (jnp.exp(jnp.sum(lq1.astype(f32) * lk1.astype(f32)))
           - jnp.exp(jnp.sum(lq2.astype(f32) * lk2.astype(f32))) + lambda_init)
    slopes = alibi_slopes(ATTN_HEADS)
    pos = jnp.arange(seq, dtype=f32)
    n_blocks = seq // Q_BLOCK
    q_blocks = qh.reshape(bsz, n_blocks, Q_BLOCK, ATTN_HEADS, 2, ATTN_HEAD_DIM).transpose(1, 0, 2, 3, 4, 5)
    pos_blocks = pos.reshape(n_blocks, Q_BLOCK)

    def one_block(xs):
        q_blk, p_blk = xs
        bias = -slopes[:, None, None] * jnp.abs(p_blk[:, None] - pos[None, :])
        s1 = jnp.einsum('bqhd,bkhd->bhqk', q_blk[..., 0, :], k1) + bias
        s2 = jnp.einsum('bqhd,bkhd->bhqk', q_blk[..., 1, :], k2) + bias
        attn = jax.nn.softmax(s1, axis=-1) - lam * jax.nn.softmax(s2, axis=-1)
        return jnp.einsum('bhqk,bkhe->bqhe', attn, vh)

    o = lax.map(one_block, (q_blocks, pos_blocks))
    o = o.transpose(1, 0, 2, 3, 4).reshape(bsz, seq, ATTN_HEADS, 2 * ATTN_HEAD_DIM)
    o = rmsnorm(o, g_subln) * (1.0 - lambda_init)
    return o.reshape(bsz, seq, ATTN_WIDTH).astype(q.dtype)


def moe_swiglu(h, w_router, b_router, w_gate, w_up, w_down):
    bsz, seq, dm = h.shape
    t = h.reshape(-1, dm)
    logits = (t @ w_router).astype(jnp.float32) + b_router.astype(jnp.float32)
    top_vals, top_idx = lax.top_k(logits, TOP_K)
    top_w = jax.nn.softmax(top_vals, axis=-1)
    gates = jnp.einsum('nk,nke->ne', top_w,
                       jax.nn.one_hot(top_idx, N_EXPERTS, dtype=jnp.float32)).astype(h.dtype)
    out = jnp.zeros_like(t)
    for e in range(N_EXPERTS):
        out = out + gates[:, e:e + 1] * swiglu(t, w_gate[e], w_up[e], w_down[e])
    return out.reshape(bsz, seq, dm)


def setup_inputs(seed: int = 0) -> dict:
    key = jax.random.key(seed)
    ks = jax.random.split(key, 40)
    f32 = jnp.float32
    nrm = lambda k, shape, s: jax.random.normal(k, shape, f32) * s
    gain = lambda k, shape: 1.0 + 0.02 * jax.random.normal(k, shape, f32)
    n_idx = jnp.arange(SSM_STATE, dtype=f32)
    lam_shape = (DEPTH, 2, SSM_GROUPS, SSM_STATE)
    return {
        "x": jax.random.normal(ks[0], (BATCH, SEQ, D_MODEL), f32),
        "g_mix": gain(ks[1], (DEPTH, D_MODEL)),
        "w_in": nrm(ks[2], (DEPTH, D_MODEL, IN_COLS), D_MODEL ** -0.5),
        "ssm_lambda_re": -0.5 + 0.01 * jax.random.normal(ks[3], lam_shape, f32),
        "ssm_lambda_im": math.pi * n_idx + 0.01 * jax.random.normal(ks[4], lam_shape, f32),
        "ssm_log_step": jax.random.uniform(ks[5], (DEPTH, 2, SSM_GROUPS), f32,
                                           math.log(STEP_MIN), math.log(STEP_MAX)),
        "ssm_b_re": nrm(ks[6], (DEPTH, 2, SSM_GROUPS, SSM_STATE, SSM_GROUP), (2.0 * SSM_GROUP) ** -0.5),
        "ssm_b_im": nrm(ks[7], (DEPTH, 2, SSM_GROUPS, SSM_STATE, SSM_GROUP), (2.0 * SSM_GROUP) ** -0.5),
        "ssm_c_re": nrm(ks[8], (DEPTH, 2, SSM_GROUPS, SSM_GROUP, SSM_STATE), (2.0 * SSM_STATE) ** -0.5),
        "ssm_c_im": nrm(ks[9], (DEPTH, 2, SSM_GROUPS, SSM_GROUP, SSM_STATE), (2.0 * SSM_STATE) ** -0.5),
        "ssm_d": nrm(ks[10], (DEPTH, SSM_WIDTH), 1.0),
        "w_glu": nrm(ks[11], (DEPTH, SSM_WIDTH, SSM_WIDTH), SSM_WIDTH ** -0.5),
        "b_glu": nrm(ks[12], (DEPTH, SSM_WIDTH), 0.02),
        "g_ssm_out": gain(ks[13], (DEPTH, SSM_WIDTH)),
        "lambda_q1": nrm(ks[14], (DEPTH, ATTN_HEAD_DIM), 0.1),
        "lambda_k1": nrm(ks[15], (DEPTH, ATTN_HEAD_DIM), 0.1),
        "lambda_q2": nrm(ks[16], (DEPTH, ATTN_HEAD_DIM), 0.1),
        "lambda_k2": nrm(ks[17], (DEPTH, ATTN_HEAD_DIM), 0.1),
        "g_subln": gain(ks[18], (DEPTH, 2 * ATTN_HEAD_DIM)),
        "w_out": nrm(ks[19], (DEPTH, D_MODEL, D_MODEL), D_MODEL ** -0.5),
        "g_ffn": gain(ks[20], (DEPTH, D_MODEL)),
        "dense_w_gate": nrm(ks[21], (N_DENSE, D_MODEL, D_FF), D_MODEL ** -0.5),
        "dense_w_up": nrm(ks[22], (N_DENSE, D_MODEL, D_FF), D_MODEL ** -0.5),
        "dense_w_down": nrm(ks[23], (N_DENSE, D_FF, D_MODEL), D_FF ** -0.5),
        "w_router": nrm(ks[24], (N_MOE, D_MODEL, N_EXPERTS), D_MODEL ** -0.5),
        "b_router": nrm(ks[25], (N_MOE, N_EXPERTS), 0.01),
        "moe_w_gate": nrm(ks[26], (N_MOE, N_EXPERTS, D_MODEL, D_FF_EXPERT), D_MODEL ** -0.5),
        "moe_w_up": nrm(ks[27], (N_MOE, N_EXPERTS, D_MODEL, D_FF_EXPERT), D_MODEL ** -0.5),
        "moe_w_down": nrm(ks[28], (N_MOE, N_EXPERTS, D_FF_EXPERT, D_MODEL), D_FF_EXPERT ** -0.5),
        "g_final": gain(ks[29], (D_MODEL,)),
    }


def reference(x, g_mix, w_in, ssm_lambda_re, ssm_lambda_im, ssm_log_step, ssm_b_re, ssm_b_im,
              ssm_c_re, ssm_c_im, ssm_d, w_glu, b_glu, g_ssm_out, lambda_q1, lambda_k1,
              lambda_q2, lambda_k2, g_subln, w_out, g_ffn, dense_w_gate, dense_w_up,
              dense_w_down, w_router, b_router, moe_w_gate, moe_w_up, moe_w_down, g_final):
    splits = [SSM_WIDTH, SSM_WIDTH + ATTN_WIDTH, SSM_WIDTH + 2 * ATTN_WIDTH]
    for i in range(DEPTH):
        lambda_init = 0.8 - 0.6 * math.exp(-0.3 * i)
        h = rmsnorm(x, g_mix[i])
        z = h @ w_in[i]
        u, q, k, v = jnp.split(z, splits, axis=-1)
        y_ssm = ssm_mixer(u, ssm_lambda_re[i], ssm_lambda_im[i], ssm_log_step[i],
                          ssm_b_re[i], ssm_b_im[i], ssm_c_re[i], ssm_c_im[i],
                          ssm_d[i], w_glu[i], b_glu[i], g_ssm_out[i])
        y_att = diff_attention(q, k, v, lambda_q1[i], lambda_k1[i], lambda_q2[i], lambda_k2[i],
                               g_subln[i], lambda_init)
        x = x + jnp.concatenate([y_ssm, y_att], axis=-1) @ w_out[i]
        h = rmsnorm(x, g_ffn[i])
        if i % 2 == 0:
            j = i // 2
            x = x + swiglu(h, dense_w_gate[j], dense_w_up[j], dense_w_down[j])
        else:
            j = i // 2
            x = x + moe_swiglu(h, w_router[j], b_router[j], moe_w_gate[j], moe_w_up[j], moe_w_down[j])
    return rmsnorm(x, g_final)
```

```python
import functools
import math

import jax
import jax.numpy as jnp
from jax import lax
from jax.experimental import pallas as pl
from jax.experimental.pallas import tpu as pltpu

F32 = jnp.float32
BF16 = jnp.bfloat16

D_MODEL = 1024
SSM_WIDTH = 512
SSM_GROUP = 16
SSM_GROUPS = 32
SSM_STATE = 64
ATTN_WIDTH = 512
HEAD_DIM = 64
HEADS = 4
HEAD_V = 2 * HEAD_DIM
N_EXPERTS = 8
NORM_EPS = 1e-6

LANES = 128
SUBLANES = 8
SSM_CHUNK = 16
CHUNK_COLS = SSM_CHUNK * SSM_GROUP
VMEM_LIMIT = 56 * 1024 * 1024


def _cparams(sem):
    return pltpu.CompilerParams(dimension_semantics=sem, vmem_limit_bytes=VMEM_LIMIT)


def _rms(x, g):
    return x * lax.rsqrt(jnp.mean(x * x, axis=-1, keepdims=True) + NORM_EPS) * g


def _inproj_kernel(x_ref, g_ref, w_ref, u_ref, q_ref, k_ref, v_ref):
    h = _rms(x_ref[...], g_ref[...]).astype(BF16)
    outs = (u_ref, q_ref, k_ref, v_ref)
    for n, o_ref in enumerate(outs):
        z = jnp.dot(h, w_ref[:, n * 512:(n + 1) * 512], preferred_element_type=F32)
        if n == 1:
            z = z * (HEAD_DIM ** -0.5)
        o_ref[...] = z.astype(BF16)


def _inproj(x2, g, w_bf, tm):
    n = x2.shape[0]
    out = jax.ShapeDtypeStruct((n, 512), BF16)
    return pl.pallas_call(
        _inproj_kernel,
        out_shape=(out, out, out, out),
        grid=(n // tm,),
        in_specs=[
            pl.BlockSpec((tm, D_MODEL), lambda i: (i, 0)),
            pl.BlockSpec((1, D_MODEL), lambda i: (0, 0)),
            pl.BlockSpec((D_MODEL, 2048), lambda i: (0, 0)),
        ],
        out_specs=tuple(pl.BlockSpec((tm, 512), lambda i: (i, 0)) for _ in range(4)),
        compiler_params=_cparams(("parallel",)),
        name="inproj",
    )(x2, g, w_bf)


def _ssm_matrices(lam_re, lam_im, log_step, b_re, b_im, c_re, c_im, d_skip):
    t = SSM_CHUNK
    tau = jnp.arange(t + 1, dtype=F32)
    ks, ws, es, decs = [], [], [], []
    for d in range(2):
        lam = lax.complex(lam_re[d].astype(F32), lam_im[d].astype(F32))
        step = jnp.exp(log_step[d].astype(F32))[:, None]
        ls = lam * step
        lam_bar = jnp.exp(ls)
        pw = jnp.exp(ls[:, None, :] * tau[None, :, None])
        b_bar = ((lam_bar - 1.0) / lam)[:, :, None] * lax.complex(b_re[d].astype(F32), b_im[d].astype(F32))
        c = lax.complex(c_re[d].astype(F32), c_im[d].astype(F32))
        ks.append(jnp.real(jnp.einsum('gop,gtp,gpi->gtoi', c, pw[:, :t], b_bar)))
        if d == 0:
            wpow = pw[:, :t][:, ::-1]
            epow = pw[:, 1:t + 1]
        else:
            wpow = pw[:, :t]
            epow = pw[:, 1:t + 1][:, ::-1]
        ws.append(jnp.einsum('gsp,gpi->gsip', wpow, b_bar).reshape(SSM_GROUPS, CHUNK_COLS, SSM_STATE))
        es.append(jnp.einsum('gop,gtp->gpto', c, epow).reshape(SSM_GROUPS, SSM_STATE, CHUNK_COLS))
        decs.append(pw[:, t])
    kf, kb = ks
    s_idx = jnp.arange(t)[:, None]
    t_idx = jnp.arange(t)[None, :]
    lag_f = t_idx - s_idx
    lag_b = s_idx - t_idx
    mf = jnp.where((lag_f >= 0)[None, :, :, None, None], kf[:, jnp.clip(lag_f, 0, t - 1)], 0.0)
    mb = jnp.where((lag_b >= 0)[None, :, :, None, None], kb[:, jnp.clip(lag_b, 0, t - 1)], 0.0)
    dmat = (jnp.eye(t, dtype=F32)[None, :, :, None, None]
            * (jnp.eye(SSM_GROUP, dtype=F32)[None, None, None] * d_skip.astype(F32).reshape(SSM_GROUPS, 1, 1, 1, SSM_GROUP)))
    m = (mf + mb + dmat).transpose(0, 1, 4, 2, 3).reshape(SSM_GROUPS, CHUNK_COLS, CHUNK_COLS)
    w = jnp.concatenate([jnp.real(ws[0]), jnp.real(ws[1]), jnp.imag(ws[0]), jnp.imag(ws[1])], axis=-1)
    v = jnp.concatenate([jnp.real(es[0]), jnp.real(es[1]), -jnp.imag(es[0]), -jnp.imag(es[1])], axis=1)
    dec_re = jnp.concatenate([jnp.real(decs[0]), jnp.real(decs[1])], axis=-1)[:, None, :]
    dec_im = jnp.concatenate([jnp.imag(decs[0]), jnp.imag(decs[1])], axis=-1)[:, None, :]
    return m.astype(BF16), w.astype(BF16), v.astype(BF16), dec_re, dec_im


def _ssm_kernel(u_ref, m_ref, w_ref, v_ref, ar_ref, ai_ref, y_ref, s_ref, hf_ref, hb_ref, *, n_chunks, rows):
    u = u_ref[0]
    s_ref[...] = jnp.dot(u, w_ref[0], preferred_element_type=F32)
    ar = jnp.broadcast_to(ar_ref[0], (rows, LANES))
    ai = jnp.broadcast_to(ai_ref[0], (rows, LANES))

    def body(c, carry):
        fr, fi, br, bi = carry
        rf = pl.multiple_of(c * rows, rows)
        rb = pl.multiple_of((n_chunks - 1 - c) * rows, rows)
        hf_ref[pl.ds(rf, rows), :LANES] = fr
        hf_ref[pl.ds(rf, rows), LANES:] = fi
        hb_ref[pl.ds(rb, rows), :LANES] = br
        hb_ref[pl.ds(rb, rows), LANES:] = bi
        sf = s_ref[pl.ds(rf, rows), :]
        sb = s_ref[pl.ds(rb, rows), :]
        nfr = ar * fr - ai * fi + sf[:, :LANES]
        nfi = ai * fr + ar * fi + sf[:, LANES:]
        nbr = ar * br - ai * bi + sb[:, :LANES]
        nbi = ai * br + ar * bi + sb[:, LANES:]
        return nfr, nfi, nbr, nbi

    z = jnp.zeros((rows, LANES), F32)
    lax.fori_loop(0, n_chunks, body, (z, z, z, z), unroll=4)
    lane = lax.broadcasted_iota(jnp.int32, (1, 2 * LANES), 1)
    is_fwd = (lane % LANES) < SSM_STATE
    hcat = jnp.where(is_fwd, hf_ref[...], hb_ref[...]).astype(BF16)
    y = jnp.dot(u, m_ref[0], preferred_element_type=F32)
    y = y + jnp.dot(hcat, v_ref[0], preferred_element_type=F32)
    y_ref[0] = jax.nn.gelu(y).astype(BF16)


def _ssm(uc, m, w, v, dec_re, dec_im, n_chunks, rows):
    g, r, _ = uc.shape
    mat = pl.BlockSpec((1, CHUNK_COLS, CHUNK_COLS), lambda i: (i, 0, 0))
    dec = pl.BlockSpec((1, 1, LANES), lambda i: (i, 0, 0))
    return pl.pallas_call(
        functools.partial(_ssm_kernel, n_chunks=n_chunks, rows=rows),
        out_shape=jax.ShapeDtypeStruct((g, r, CHUNK_COLS), BF16),
        grid=(g,),
        in_specs=[pl.BlockSpec((1, r, CHUNK_COLS), lambda i: (i, 0, 0)), mat, mat, mat, dec, dec],
        out_specs=pl.BlockSpec((1, r, CHUNK_COLS), lambda i: (i, 0, 0)),
        scratch_shapes=[pltpu.VMEM((r, CHUNK_COLS), F32)] * 3,
        compiler_params=_cparams(("parallel",)),
        name="ssm",
    )(uc, m, w, v, dec_re, dec_im)


def _attn_kernel(par_ref, q_ref, k_ref, vt_ref, g_ref, o_ref, *, tq, tk, n_kv):
    head = pl.program_id(1)
    qi = pl.program_id(2)
    lam = par_ref[0]
    out_scale = par_ref[1]
    slope = par_ref[2 + head]

    q = q_ref[0]
    lane = lax.broadcasted_iota(jnp.int32, (1, LANES), 1)
    zero = jnp.zeros_like(q)
    q1 = jnp.where(lane < HEAD_DIM, q, zero)
    q2 = jnp.where(lane >= HEAD_DIM, q, zero)
    qpos = (qi * tq + lax.broadcasted_iota(jnp.int32, (1, tq), 1)).astype(F32)
    krel = lax.broadcasted_iota(jnp.int32, (tk, 1), 0).astype(F32)
    nt = (((1,), (1,)), ((), ()))

    def one_map(s, m, l, acc, vt):
        m_new = jnp.maximum(m, jnp.max(s, axis=0, keepdims=True))
        p = jnp.exp(s - m_new)
        alpha = jnp.exp(m - m_new)
        l_new = alpha * l + jnp.sum(p, axis=0, keepdims=True)
        acc_new = alpha * acc + jnp.dot(vt, p.astype(BF16), preferred_element_type=F32)
        return m_new, l_new, acc_new

    def body(j, carry):
        m1, l1, a1, m2, l2, a2 = carry
        k = k_ref[0, pl.ds(pl.multiple_of(j * tk, tk), tk), :]
        vt = vt_ref[0, 0, j]
        kpos = krel + (j * tk).astype(F32)
        bias = -slope * jnp.abs(kpos - qpos)
        s1 = lax.dot_general(k, q1, nt, preferred_element_type=F32) + bias
        s2 = lax.dot_general(k, q2, nt, preferred_element_type=F32) + bias
        m1, l1, a1 = one_map(s1, m1, l1, a1, vt)
        m2, l2, a2 = one_map(s2, m2, l2, a2, vt)
        return m1, l1, a1, m2, l2, a2

    neg = jnp.full((1, tq), -1e30, F32)
    zl = jnp.zeros((1, tq), F32)
    za = jnp.zeros((HEAD_V, tq), F32)
    m1, l1, a1, m2, l2, a2 = lax.fori_loop(0, n_kv, body, (neg, zl, za, neg, zl, za))
    o = a1 / l1 - lam * (a2 / l2)
    o = o * lax.rsqrt(jnp.mean(o * o, axis=0, keepdims=True) + NORM_EPS)
    o = o * (g_ref[...] * out_scale)
    o_ref[0] = o.T.astype(BF16)


def _attention(par, q, k, vt, g_col, tq, tk):
    b, l, _ = q.shape
    n_kv = l // tk
    return pl.pallas_call(
        functools.partial(_attn_kernel, tq=tq, tk=tk, n_kv=n_kv),
        out_shape=jax.ShapeDtypeStruct((b, l, ATTN_WIDTH), BF16),
        grid=(b, HEADS, l // tq),
        in_specs=[
            pl.BlockSpec(memory_space=pltpu.SMEM),
            pl.BlockSpec((1, tq, HEAD_V), lambda bi, h, i: (bi, i, h)),
            pl.BlockSpec((1, l, HEAD_V), lambda bi, h, i: (bi, 0, h)),
            pl.BlockSpec((1, 1, n_kv, HEAD_V, tk), lambda bi, h, i: (bi, h, 0, 0, 0)),
            pl.BlockSpec((HEAD_V, 1), lambda bi, h, i: (0, 0)),
        ],
        out_specs=pl.BlockSpec((1, tq, HEAD_V), lambda bi, h, i: (bi, i, h)),
        compiler_params=_cparams(("parallel", "parallel", "parallel")),
        name="diff_attn",
    )(par, q, k, vt, g_col)


def _postmix_kernel(x_ref, ys_ref, ya_ref, wglu_ref, bglu_ref, gs_ref, wout_ref, gf_ref, *rest, router):
    if router:
        wr_ref, br_ref, xo_ref, h_ref, gate_ref = rest
    else:
        xo_ref, h_ref = rest
    y = ys_ref[...]
    t = jnp.dot(y, wglu_ref[...], preferred_element_type=F32) + bglu_ref[...]
    y = y.astype(F32) * jax.nn.sigmoid(t)
    y = _rms(y, gs_ref[...]).astype(BF16)
    mix = jnp.dot(y, wout_ref[:SSM_WIDTH, :], preferred_element_type=F32)
    mix = mix + jnp.dot(ya_ref[...], wout_ref[SSM_WIDTH:, :], preferred_element_type=F32)
    x = x_ref[...] + mix
    xo_ref[...] = x
    h = _rms(x, gf_ref[...])
    h_ref[...] = h.astype(BF16)
    if router:
        logits = jnp.dot(h, wr_ref[...], precision=lax.Precision.HIGHEST, preferred_element_type=F32) + br_ref[...]
        lane = lax.broadcasted_iota(jnp.int32, logits.shape, 1)
        big = jnp.int32(LANES)
        m1 = jnp.max(logits, axis=-1, keepdims=True)
        i1 = jnp.min(jnp.where(logits == m1, lane, big), axis=-1, keepdims=True)
        rest_l = jnp.where(lane == i1, -jnp.inf, logits)
        m2 = jnp.max(rest_l, axis=-1, keepdims=True)
        i2 = jnp.min(jnp.where(rest_l == m2, lane, big), axis=-1, keepdims=True)
        e2 = jnp.exp(m2 - m1)
        w1 = 1.0 / (1.0 + e2)
        w2 = e2 * w1
        gates = jnp.where(lane == i1, w1, 0.0) + jnp.where(lane == i2, w2, 0.0)
        gate_ref[...] = gates[:, :N_EXPERTS]


def _postmix(x2, ys, ya, wglu, bglu, gs, wout, gf, router_w, tm):
    n = x2.shape[0]
    router = router_w is not None
    row = lambda c: pl.BlockSpec((tm, c), lambda i: (i, 0))
    full = lambda r, c: pl.BlockSpec((r, c), lambda i: (0, 0))
    in_specs = [row(D_MODEL), row(512), row(512), full(512, 512), full(1, 512), full(1, 512),
                full(D_MODEL, D_MODEL), full(1, D_MODEL)]
    args = [x2, ys, ya, wglu, bglu, gs, wout, gf]
    out_shape = [jax.ShapeDtypeStruct((n, D_MODEL), F32), jax.ShapeDtypeStruct((n, D_MODEL), BF16)]
    out_specs = [row(D_MODEL), row(D_MODEL)]
    if router:
        in_specs += [full(D_MODEL, LANES), full(1, LANES)]
        args += list(router_w)
        out_shape.append(jax.ShapeDtypeStruct((n, N_EXPERTS), F32))
        out_specs.append(row(N_EXPERTS))
    return pl.pallas_call(
        functools.partial(_postmix_kernel, router=router),
        out_shape=tuple(out_shape),
        grid=(n // tm,),
        in_specs=in_specs,
        out_specs=tuple(out_specs),
        compiler_params=_cparams(("parallel",)),
        name="postmix_router" if router else "postmix",
    )(*args)


def _ffn_kernel(*refs, gated, final):
    if gated:
        h_ref, x_ref, gate_ref, wg_ref, wu_ref, wd_ref, gfin_ref, o_ref = refs
    else:
        h_ref, x_ref, wg_ref, wu_ref, wd_ref, gfin_ref, o_ref = refs
    e = pl.program_id(1)
    f = pl.program_id(2)
    first = jnp.logical_and(e == 0, f == 0)
    last = jnp.logical_and(e == pl.num_programs(1) - 1, f == pl.num_programs(2) - 1)

    @pl.when(first)
    def _():
        o_ref[...] = x_ref[...]

    h = h_ref[...]
    g = jnp.dot(h, wg_ref[0], preferred_element_type=F32)
    u = jnp.dot(h, wu_ref[0], preferred_element_type=F32)
    a = jax.nn.silu(g) * u
    if gated:
        gates = gate_ref[...]
        lane = lax.broadcasted_iota(jnp.int32, gates.shape, 1)
        a = a * jnp.sum(jnp.where(lane == e, gates, 0.0), axis=-1, keepdims=True)
    o_ref[...] += jnp.dot(a.astype(BF16), wd_ref[0], preferred_element_type=F32)

    if final:
        @pl.when(last)
        def _():
            o_ref[...] = _rms(o_ref[...], gfin_ref[...])


def _ffn(h, x2, gates, wg, wu, wd, gfin, tm, tf, final):
    n = x2.shape[0]
    n_e, _, ff = wg.shape
    gated = gates is not None
    row = lambda c: pl.BlockSpec((tm, c), lambda i, e, f: (i, 0))
    in_specs = [row(D_MODEL), row(D_MODEL)]
    args = [h, x2]
    if gated:
        in_specs.append(row(N_EXPERTS))
        args.append(gates)
    in_specs += [
        pl.BlockSpec((1, D_MODEL, tf), lambda i, e, f: (e, 0, f)),
        pl.BlockSpec((1, D_MODEL, tf), lambda i, e, f: (e, 0, f)),
        pl.BlockSpec((1, tf, D_MODEL), lambda i, e, f: (e, f, 0)),
        pl.BlockSpec((1, D_MODEL), lambda i, e, f: (0, 0)),
    ]
    args += [wg, wu, wd, gfin]
    return pl.pallas_call(
        functools.partial(_ffn_kernel, gated=gated, final=final),
        out_shape=jax.ShapeDtypeStruct((n, D_MODEL), F32),
        grid=(n // tm, n_e, ff // tf),
        in_specs=in_specs,
        out_specs=row(D_MODEL),
        compiler_params=_cparams(("parallel", "arbitrary", "arbitrary")),
        name="ffn_moe" if gated else "ffn_dense",
    )(*args)


def _pad_axis(a, axis, size):
    pad = [(0, 0)] * a.ndim
    pad[axis] = (0, size - a.shape[axis])
    return jnp.pad(a, pad)


def kernel(x, g_mix, w_in, ssm_lambda_re, ssm_lambda_im, ssm_log_step, ssm_b_re, ssm_b_im, ssm_c_re, ssm_c_im, ssm_d, w_glu, b_glu, g_ssm_out, lambda_q1, lambda_k1, lambda_q2, lambda_k2, g_subln, w_out, g_ffn, dense_w_gate, dense_w_up, dense_w_down, w_router, b_router, moe_w_gate, moe_w_up, moe_w_down, g_final):
    bsz, seq, _ = x.shape
    n = bsz * seq
    depth = w_in.shape[0]
    n_chunks = seq // SSM_CHUNK
    tm = min(512, n)
    tm_ffn = min(1024, n)
    tq = min(256, seq)
    tk = min(256, seq)
    d_ff = dense_w_gate.shape[-1]
    d_ff_pad = -(-d_ff // 256) * 256
    slopes = [2.0 ** (-8.0 * (h + 1) / HEADS) for h in range(HEADS)]

    x2 = x.reshape(n, D_MODEL)
    for i in range(depth):
        lambda_init = 0.8 - 0.6 * math.exp(-0.3 * i)
        u, q, k, v = _inproj(x2, g_mix[i][None, :], w_in[i].astype(BF16), tm)

        m, w, vv, dec_re, dec_im = _ssm_matrices(
            ssm_lambda_re[i], ssm_lambda_im[i], ssm_log_step[i], ssm_b_re[i], ssm_b_im[i],
            ssm_c_re[i], ssm_c_im[i], ssm_d[i])
        uc = u.reshape(bsz, n_chunks, SSM_CHUNK, SSM_GROUPS, SSM_GROUP).transpose(3, 1, 0, 2, 4)
        uc = uc.reshape(SSM_GROUPS, n_chunks * bsz, CHUNK_COLS)
        yc = _ssm(uc, m, w, vv, dec_re, dec_im, n_chunks, bsz)
        ys = yc.reshape(SSM_GROUPS, n_chunks, bsz, SSM_CHUNK, SSM_GROUP).transpose(2, 1, 3, 0, 4)
        ys = ys.reshape(n, SSM_WIDTH)

        lam = (jnp.exp(jnp.sum(lambda_q1[i].astype(F32) * lambda_k1[i].astype(F32)))
               - jnp.exp(jnp.sum(lambda_q2[i].astype(F32) * lambda_k2[i].astype(F32))) + lambda_init)
        par = jnp.concatenate([jnp.stack([lam, jnp.asarray(1.0 - lambda_init, F32)]),
                               jnp.asarray(slopes, F32), jnp.zeros((2,), F32)])
        vt = v.reshape(bsz, seq // tk, tk, HEADS, HEAD_V).transpose(0, 3, 1, 4, 2)
        ya = _attention(par, q.reshape(bsz, seq, ATTN_WIDTH), k.reshape(bsz, seq, ATTN_WIDTH), vt,
                        g_subln[i].astype(F32)[:, None], tq, tk)
        ya = ya.reshape(n, ATTN_WIDTH)

        j = i // 2
        if i % 2 == 0:
            router_w = None
        else:
            router_w = (_pad_axis(w_router[j].astype(F32), 1, LANES),
                        jnp.concatenate([b_router[j].astype(F32),
                                         jnp.full((LANES - N_EXPERTS,), -jnp.inf, F32)])[None, :])
        outs = _postmix(x2, ys, ya, w_glu[i].astype(BF16), b_glu[i][None, :], g_ssm_out[i][None, :],
                        w_out[i].astype(BF16), g_ffn[i][None, :], router_w, tm)
        final = i == depth - 1
        gfin = g_final[None, :]
        if i % 2 == 0:
            x2, h = outs
            wg = _pad_axis(dense_w_gate[j].astype(BF16), 1, d_ff_pad)[None]
            wu = _pad_axis(dense_w_up[j].astype(BF16), 1, d_ff_pad)[None]
            wd = _pad_axis(dense_w_down[j].astype(BF16), 0, d_ff_pad)[None]
            x2 = _ffn(h, x2, None, wg, wu, wd, gfin, tm_ffn, 256, final)
        else:
            x2, h, gates = outs
            x2 = _ffn(h, x2, gates, moe_w_gate[j].astype(BF16), moe_w_up[j].astype(BF16),
                      moe_w_down[j].astype(BF16), gfin, tm_ffn, 512, final)
    return x2.reshape(bsz, seq, D_MODEL)
```

```python
import functools
import math

import numpy as np
import jax
import jax.numpy as jnp
from jax import lax
from jax.experimental import pallas as pl
from jax.experimental.pallas import tpu as pltpu

F32 = jnp.float32
BF16 = jnp.bfloat16

D_MODEL = 1024
SSM_WIDTH = 512
SSM_GROUP = 16
SSM_GROUPS = 32
SSM_STATE = 64
ATTN_WIDTH = 512
HEAD_DIM = 64
HEADS = 4
HEAD_V = 2 * HEAD_DIM
N_EXPERTS = 8
NORM_EPS = 1e-6

LANES = 128
SUBLANES = 8
KV_CHUNK = 512
TQ = 256
LOG2E = math.log2(math.e)
SSM_CHUNK = 16
CHUNK_COLS = SSM_CHUNK * SSM_GROUP
VMEM_LIMIT = 56 * 1024 * 1024


def _cparams(sem):
    return pltpu.CompilerParams(dimension_semantics=sem, vmem_limit_bytes=VMEM_LIMIT)


def _rms(x, g):
    return x * lax.rsqrt(jnp.mean(x * x, axis=-1, keepdims=True) + NORM_EPS) * g


def _inproj_kernel(x_ref, g_ref, w_ref, u_ref, q_ref, k_ref, v_ref):
    h = _rms(x_ref[...], g_ref[...]).astype(BF16)
    proj = lambda n: jnp.dot(h, w_ref[:, n * 512:(n + 1) * 512], preferred_element_type=F32)
    u_ref[...] = proj(0).astype(BF16)
    q_ref[...] = (proj(1) * (HEAD_DIM ** -0.5 * LOG2E)).astype(BF16)
    k_ref[...] = proj(2).astype(BF16)
    v_ref[...] = proj(3).astype(BF16)


def _inproj(x2, g, w_bf, tm):
    n = x2.shape[0]
    out = jax.ShapeDtypeStruct((n, 512), BF16)
    row = lambda c: pl.BlockSpec((tm, c), lambda i: (i, 0))
    return pl.pallas_call(
        _inproj_kernel,
        out_shape=(out, out, out, out),
        grid=(n // tm,),
        in_specs=[
            row(D_MODEL),
            pl.BlockSpec((1, D_MODEL), lambda i: (0, 0)),
            pl.BlockSpec((D_MODEL, 2048), lambda i: (0, 0)),
        ],
        out_specs=(row(512), row(512), row(512), row(512)),
        compiler_params=_cparams(("parallel",)),
        name="inproj",
    )(x2, g, w_bf)


def _ssm_matrices(lam_re, lam_im, log_step, b_re, b_im, c_re, c_im, d_skip):
    t = SSM_CHUNK
    tau = jnp.arange(t + 1, dtype=F32)
    ks, ws, es, decs = [], [], [], []
    for d in range(2):
        lam = lax.complex(lam_re[d].astype(F32), lam_im[d].astype(F32))
        step = jnp.exp(log_step[d].astype(F32))[:, None]
        ls = lam * step
        lam_bar = jnp.exp(ls)
        pw = jnp.exp(ls[:, None, :] * tau[None, :, None])
        b_bar = ((lam_bar - 1.0) / lam)[:, :, None] * lax.complex(b_re[d].astype(F32), b_im[d].astype(F32))
        c = lax.complex(c_re[d].astype(F32), c_im[d].astype(F32))
        ks.append(jnp.real(jnp.einsum('gop,gtp,gpi->gtoi', c, pw[:, :t], b_bar)))
        if d == 0:
            wpow = pw[:, :t][:, ::-1]
            epow = pw[:, 1:t + 1]
        else:
            wpow = pw[:, :t]
            epow = pw[:, 1:t + 1][:, ::-1]
        ws.append(jnp.einsum('gsp,gpi->gsip', wpow, b_bar).reshape(SSM_GROUPS, CHUNK_COLS, SSM_STATE))
        es.append(jnp.einsum('gop,gtp->gpto', c, epow).reshape(SSM_GROUPS, SSM_STATE, CHUNK_COLS))
        decs.append(pw[:, t])
    kf, kb = ks
    s_idx = jnp.arange(t)[:, None]
    t_idx = jnp.arange(t)[None, :]
    lag_f = t_idx - s_idx
    lag_b = s_idx - t_idx
    mf = jnp.where((lag_f >= 0)[None, :, :, None, None], kf[:, jnp.clip(lag_f, 0, t - 1)], 0.0)
    mb = jnp.where((lag_b >= 0)[None, :, :, None, None], kb[:, jnp.clip(lag_b, 0, t - 1)], 0.0)
    dmat = (jnp.eye(t, dtype=F32)[None, :, :, None, None]
            * (jnp.eye(SSM_GROUP, dtype=F32)[None, None, None] * d_skip.astype(F32).reshape(SSM_GROUPS, 1, 1, 1, SSM_GROUP)))
    m = (mf + mb + dmat).transpose(0, 1, 4, 2, 3).reshape(SSM_GROUPS, CHUNK_COLS, CHUNK_COLS)
    w = jnp.concatenate([jnp.real(ws[0]), jnp.real(ws[1]), jnp.imag(ws[0]), jnp.imag(ws[1])], axis=-1)
    v = jnp.concatenate([jnp.real(es[0]), jnp.real(es[1]), -jnp.imag(es[0]), -jnp.imag(es[1])], axis=1)
    dec_re = jnp.concatenate([jnp.real(decs[0]), jnp.real(decs[1])], axis=-1)[:, None, :]
    dec_im = jnp.concatenate([jnp.imag(decs[0]), jnp.imag(decs[1])], axis=-1)[:, None, :]
    return m.astype(BF16), w.astype(BF16), v.astype(BF16), dec_re, dec_im


def _ssm_kernel(u_ref, m_ref, w_ref, v_ref, ar_ref, ai_ref, y_ref, s_ref, hf_ref, hb_ref, *, n_chunks, rows):
    u = u_ref[0]
    s_ref[...] = jnp.dot(u, w_ref[0], preferred_element_type=F32)
    ar = jnp.broadcast_to(ar_ref[0], (rows, LANES))
    ai = jnp.broadcast_to(ai_ref[0], (rows, LANES))

    def body(c, carry):
        fr, fi, br, bi = carry
        rf = pl.multiple_of(c * rows, rows)
        rb = pl.multiple_of((n_chunks - 1 - c) * rows, rows)
        hf_ref[pl.ds(rf, rows), :LANES] = fr
        hf_ref[pl.ds(rf, rows), LANES:] = fi
        hb_ref[pl.ds(rb, rows), :LANES] = br
        hb_ref[pl.ds(rb, rows), LANES:] = bi
        sf = s_ref[pl.ds(rf, rows), :]
        sb = s_ref[pl.ds(rb, rows), :]
        nfr = ar * fr - ai * fi + sf[:, :LANES]
        nfi = ai * fr + ar * fi + sf[:, LANES:]
        nbr = ar * br - ai * bi + sb[:, :LANES]
        nbi = ai * br + ar * bi + sb[:, LANES:]
        return nfr, nfi, nbr, nbi

    z = jnp.zeros((rows, LANES), F32)
    lax.fori_loop(0, n_chunks, body, (z, z, z, z), unroll=4)
    lane = lax.broadcasted_iota(jnp.int32, (1, 2 * LANES), 1)
    is_fwd = (lane % LANES) < SSM_STATE
    hcat = jnp.where(is_fwd, hf_ref[...], hb_ref[...]).astype(BF16)
    y = jnp.dot(u, m_ref[0], preferred_element_type=F32)
    y = y + jnp.dot(hcat, v_ref[0], preferred_element_type=F32)
    y_ref[0] = jax.nn.gelu(y).astype(BF16)


def _ssm(uc, m, w, v, dec_re, dec_im, n_chunks, rows):
    g, r, _ = uc.shape
    mat = pl.BlockSpec((1, CHUNK_COLS, CHUNK_COLS), lambda i: (i, 0, 0))
    dec = pl.BlockSpec((1, 1, LANES), lambda i: (i, 0, 0))
    return pl.pallas_call(
        functools.partial(_ssm_kernel, n_chunks=n_chunks, rows=rows),
        out_shape=jax.ShapeDtypeStruct((g, r, CHUNK_COLS), BF16),
        grid=(g,),
        in_specs=[pl.BlockSpec((1, r, CHUNK_COLS), lambda i: (i, 0, 0)), mat, mat, mat, dec, dec],
        out_specs=pl.BlockSpec((1, r, CHUNK_COLS), lambda i: (i, 0, 0)),
        scratch_shapes=[pltpu.VMEM((r, CHUNK_COLS), F32)] * 3,
        compiler_params=_cparams(("parallel",)),
        name="ssm",
    )(uc, m, w, v, dec_re, dec_im)


def _alibi_tiles(slopes2):
    rel = np.arange(TQ, dtype=np.float32)[None, :] - np.arange(KV_CHUNK, dtype=np.float32)[:, None]
    tiles = [np.stack([-(np.float32(s) * rel), np.float32(s) * rel]) for s in slopes2]
    return jnp.asarray(np.stack(tiles), F32)


def _attn_kernel(par_ref, q_ref, k_ref, v_ref, bt_ref, g_ref, o_ref, vt_ref, qv_ref, acc_ref, sa_ref, sb_ref,
                 *, n_chunks):
    head = pl.program_id(1)
    qi = pl.program_id(2)
    lam = par_ref[0]
    out_scale = par_ref[1]
    slope = par_ref[2 + head]

    @pl.when(qi == 0)
    def _():
        for c in range(n_chunks):
            vt_ref[c] = v_ref[0, c * KV_CHUNK:(c + 1) * KV_CHUNK, :].astype(F32).T.astype(BF16)

    qt = q_ref[0].astype(F32).T.astype(BF16)
    zero = jnp.zeros((HEAD_DIM, TQ), BF16)
    qv_ref[0, :HEAD_DIM] = qt[:HEAD_DIM]
    qv_ref[0, HEAD_DIM:] = zero
    qv_ref[1, :HEAD_DIM] = zero
    qv_ref[1, HEAD_DIM:] = qt[HEAD_DIM:]

    qd = qi // (KV_CHUNK // TQ)
    i0 = qi * TQ

    s_bufs = (sa_ref, sb_ref)

    def scores(t):
        buf = s_bufs[t % 2]
        if t == 0:
            c = qd
            c_off = jnp.float32(0.0)
            bias = -jnp.abs(bt_ref[0, 1] + slope * (i0 - qd * KV_CHUNK).astype(F32))
        else:
            c = (t - 1) + ((t - 1) >= qd).astype(jnp.int32)
            c_off = -slope * jnp.abs(i0 - c * KV_CHUNK).astype(F32)
            bias = bt_ref[0, (c > qd).astype(jnp.int32)]
        k = k_ref[0, pl.ds(pl.multiple_of(c * KV_CHUNK, KV_CHUNK), KV_CHUNK), :]
        smax = []
        for mi in range(2):
            s = jnp.dot(k, qv_ref[mi], preferred_element_type=F32) + bias
            buf[mi] = s
            smax.append(jnp.max(s, axis=0, keepdims=True) + c_off)
        return c, c_off, smax

    def accumulate(t, c, c_off, smax, m, l):
        buf = s_bufs[t % 2]
        vt = vt_ref[c]
        for mi in range(2):
            m_new = jnp.maximum(m[mi], smax[mi])
            p = jnp.exp2(buf[mi] - (m_new - c_off))
            alpha = jnp.exp2(m[mi] - m_new)
            l[mi] = alpha * l[mi] + jnp.sum(p, axis=0, keepdims=True)
            acc_ref[mi] = alpha * acc_ref[mi] + jnp.dot(vt, p.astype(BF16), preferred_element_type=F32)
            m[mi] = m_new

    acc_ref[...] = jnp.zeros_like(acc_ref)
    m = [jnp.full((1, TQ), -1e30, F32)] * 2
    l = [jnp.zeros((1, TQ), F32)] * 2
    pending = scores(0)
    for t in range(n_chunks):
        nxt = scores(t + 1) if t + 1 < n_chunks else None
        accumulate(t, *pending, m, l)
        pending = nxt
    l1, l2 = l
    o = acc_ref[0] / l1 - lam * (acc_ref[1] / l2)
    o = o * lax.rsqrt(jnp.mean(o * o, axis=0, keepdims=True) + NORM_EPS)
    o = o * (g_ref[...] * out_scale)
    o_ref[0] = o.T.astype(BF16)


def _attention(par, q, k, v, bias_tiles, g_col):
    b, l, _ = q.shape
    n_chunks = l // KV_CHUNK
    return pl.pallas_call(
        functools.partial(_attn_kernel, n_chunks=n_chunks),
        out_shape=jax.ShapeDtypeStruct((b, l, ATTN_WIDTH), BF16),
        grid=(b, HEADS, l // TQ),
        in_specs=[
            pl.BlockSpec(memory_space=pltpu.SMEM),
            pl.BlockSpec((1, TQ, HEAD_V), lambda bi, h, i: (bi, i, h)),
            pl.BlockSpec((1, l, HEAD_V), lambda bi, h, i: (bi, 0, h)),
            pl.BlockSpec((1, l, HEAD_V), lambda bi, h, i: (bi, 0, h)),
            pl.BlockSpec((1, 2, KV_CHUNK, TQ), lambda bi, h, i: (h, 0, 0, 0)),
            pl.BlockSpec((HEAD_V, 1), lambda bi, h, i: (0, 0)),
        ],
        out_specs=pl.BlockSpec((1, TQ, HEAD_V), lambda bi, h, i: (bi, i, h)),
        scratch_shapes=[
            pltpu.VMEM((n_chunks, HEAD_V, KV_CHUNK), BF16),
            pltpu.VMEM((2, HEAD_V, TQ), BF16),
            pltpu.VMEM((2, HEAD_V, TQ), F32),
            pltpu.VMEM((2, KV_CHUNK, TQ), F32),
            pltpu.VMEM((2, KV_CHUNK, TQ), F32),
        ],
        compiler_params=_cparams(("parallel", "parallel", "arbitrary")),
        name="diff_attn",
    )(par, q, k, v, bias_tiles, g_col)


def _postmix_kernel(x_ref, ys_ref, ya_ref, wglu_ref, bglu_ref, gs_ref, wout_ref, gf_ref, *rest, router):
    if router:
        wr_ref, br_ref, xo_ref, h_ref, gate_ref = rest
    else:
        xo_ref, h_ref = rest
    y = ys_ref[...]
    t = jnp.dot(y, wglu_ref[...], preferred_element_type=F32) + bglu_ref[...]
    y = y.astype(F32) * jax.nn.sigmoid(t)
    y = _rms(y, gs_ref[...]).astype(BF16)
    mix = jnp.dot(y, wout_ref[:SSM_WIDTH, :], preferred_element_type=F32)
    mix = mix + jnp.dot(ya_ref[...], wout_ref[SSM_WIDTH:, :], preferred_element_type=F32)
    x = x_ref[...] + mix
    xo_ref[...] = x
    h = _rms(x, gf_ref[...])
    h_ref[...] = h.astype(BF16)
    if router:
        logits = jnp.dot(h, wr_ref[...], precision=lax.Precision.HIGHEST, preferred_element_type=F32) + br_ref[...]
        lane = lax.broadcasted_iota(jnp.int32, logits.shape, 1)
        big = jnp.int32(LANES)
        m1 = jnp.max(logits, axis=-1, keepdims=True)
        i1 = jnp.min(jnp.where(logits == m1, lane, big), axis=-1, keepdims=True)
        rest_l = jnp.where(lane == i1, -jnp.inf, logits)
        m2 = jnp.max(rest_l, axis=-1, keepdims=True)
        i2 = jnp.min(jnp.where(rest_l == m2, lane, big), axis=-1, keepdims=True)
        e2 = jnp.exp(m2 - m1)
        w1 = 1.0 / (1.0 + e2)
        w2 = e2 * w1
        gates = jnp.where(lane == i1, w1, 0.0) + jnp.where(lane == i2, w2, 0.0)
        gate_ref[...] = gates[:, :N_EXPERTS]


def _postmix(x2, ys, ya, wglu, bglu, gs, wout, gf, router_w, tm):
    n = x2.shape[0]
    router = router_w is not None
    row = lambda c: pl.BlockSpec((tm, c), lambda i: (i, 0))
    full = lambda r, c: pl.BlockSpec((r, c), lambda i: (0, 0))
    in_specs = [row(D_MODEL), row(512), row(512), full(512, 512), full(1, 512), full(1, 512),
                full(D_MODEL, D_MODEL), full(1, D_MODEL)]
    args = [x2, ys, ya, wglu, bglu, gs, wout, gf]
    out_shape = [jax.ShapeDtypeStruct((n, D_MODEL), F32), jax.ShapeDtypeStruct((n, D_MODEL), BF16)]
    out_specs = [row(D_MODEL), row(D_MODEL)]
    if router:
        in_specs += [full(D_MODEL, LANES), full(1, LANES)]
        args += list(router_w)
        out_shape.append(jax.ShapeDtypeStruct((n, N_EXPERTS), F32))
        out_specs.append(row(N_EXPERTS))
    return pl.pallas_call(
        functools.partial(_postmix_kernel, router=router),
        out_shape=tuple(out_shape),
        grid=(n // tm,),
        in_specs=in_specs,
        out_specs=tuple(out_specs),
        compiler_params=_cparams(("parallel",)),
        name="postmix_router" if router else "postmix",
    )(*args)


def _ffn_kernel(*refs, gated, final):
    if gated:
        h_ref, x_ref, gate_ref, wg_ref, wu_ref, wd_ref, gfin_ref, o_ref = refs
    else:
        h_ref, x_ref, wg_ref, wu_ref, wd_ref, gfin_ref, o_ref = refs
    e = pl.program_id(1)
    f = pl.program_id(2)
    first = jnp.logical_and(e == 0, f == 0)
    last = jnp.logical_and(e == pl.num_programs(1) - 1, f == pl.num_programs(2) - 1)

    @pl.when(first)
    def _():
        o_ref[...] = x_ref[...]

    h = h_ref[...]
    g = jnp.dot(h, wg_ref[0], preferred_element_type=F32)
    u = jnp.dot(h, wu_ref[0], preferred_element_type=F32)
    a = jax.nn.silu(g) * u
    if gated:
        gates = gate_ref[...]
        lane = lax.broadcasted_iota(jnp.int32, gates.shape, 1)
        a = a * jnp.sum(jnp.where(lane == e, gates, 0.0), axis=-1, keepdims=True)
    o_ref[...] += jnp.dot(a.astype(BF16), wd_ref[0], preferred_element_type=F32)

    if final:
        @pl.when(last)
        def _():
            o_ref[...] = _rms(o_ref[...], gfin_ref[...])


def _ffn(h, x2, gates, wg, wu, wd, gfin, tm, tf, final):
    n = x2.shape[0]
    n_e, _, ff = wg.shape
    gated = gates is not None
    row = lambda c: pl.BlockSpec((tm, c), lambda i, e, f: (i, 0))
    in_specs = [row(D_MODEL), row(D_MODEL)]
    args = [h, x2]
    if gated:
        in_specs.append(row(N_EXPERTS))
        args.append(gates)
    in_specs += [
        pl.BlockSpec((1, D_MODEL, tf), lambda i, e, f: (e, 0, f)),
        pl.BlockSpec((1, D_MODEL, tf), lambda i, e, f: (e, 0, f)),
        pl.BlockSpec((1, tf, D_MODEL), lambda i, e, f: (e, f, 0)),
        pl.BlockSpec((1, D_MODEL), lambda i, e, f: (0, 0)),
    ]
    args += [wg, wu, wd, gfin]
    return pl.pallas_call(
        functools.partial(_ffn_kernel, gated=gated, final=final),
        out_shape=jax.ShapeDtypeStruct((n, D_MODEL), F32),
        grid=(n // tm, n_e, ff // tf),
        in_specs=in_specs,
        out_specs=row(D_MODEL),
        compiler_params=_cparams(("parallel", "arbitrary", "arbitrary")),
        name="ffn_moe" if gated else "ffn_dense",
    )(*args)


def _pad_axis(a, axis, size):
    pad = [(0, 0)] * a.ndim
    pad[axis] = (0, size - a.shape[axis])
    return jnp.pad(a, pad)


def kernel(x, g_mix, w_in, ssm_lambda_re, ssm_lambda_im, ssm_log_step, ssm_b_re, ssm_b_im, ssm_c_re, ssm_c_im, ssm_d, w_glu, b_glu, g_ssm_out, lambda_q1, lambda_k1, lambda_q2, lambda_k2, g_subln, w_out, g_ffn, dense_w_gate, dense_w_up, dense_w_down, w_router, b_router, moe_w_gate, moe_w_up, moe_w_down, g_final):
    bsz, seq, _ = x.shape
    n = bsz * seq
    depth = w_in.shape[0]
    n_chunks = seq // SSM_CHUNK
    assert seq % KV_CHUNK == 0 and bsz == SUBLANES
    tm = min(512, n)
    tm_ffn = min(1024, n)
    d_ff = dense_w_gate.shape[-1]
    d_ff_pad = -(-d_ff // 256) * 256
    slopes2 = [2.0 ** (-8.0 * (h + 1) / HEADS) * LOG2E for h in range(HEADS)]
    bias_tiles = _alibi_tiles(slopes2)

    x2 = x.reshape(n, D_MODEL)
    for i in range(depth):
        lambda_init = 0.8 - 0.6 * math.exp(-0.3 * i)
        u, q, k, v = _inproj(x2, g_mix[i][None, :], w_in[i].astype(BF16), tm)

        m, w, vv, dec_re, dec_im = _ssm_matrices(
            ssm_lambda_re[i], ssm_lambda_im[i], ssm_log_step[i], ssm_b_re[i], ssm_b_im[i],
            ssm_c_re[i], ssm_c_im[i], ssm_d[i])
        uc = u.reshape(bsz, n_chunks, SSM_CHUNK, SSM_GROUPS, SSM_GROUP).transpose(3, 1, 0, 2, 4)
        uc = uc.reshape(SSM_GROUPS, n_chunks * bsz, CHUNK_COLS)
        yc = _ssm(uc, m, w, vv, dec_re, dec_im, n_chunks, bsz)
        ys = yc.reshape(SSM_GROUPS, n_chunks, bsz, SSM_CHUNK, SSM_GROUP).transpose(2, 1, 3, 0, 4)
        ys = ys.reshape(n, SSM_WIDTH)

        lam = (jnp.exp(jnp.sum(lambda_q1[i].astype(F32) * lambda_k1[i].astype(F32)))
               - jnp.exp(jnp.sum(lambda_q2[i].astype(F32) * lambda_k2[i].astype(F32))) + lambda_init)
        par = jnp.concatenate([jnp.stack([lam, jnp.asarray(1.0 - lambda_init, F32)]),
                               jnp.asarray(slopes2, F32), jnp.zeros((2,), F32)])
        ya = _attention(par, q.reshape(bsz, seq, ATTN_WIDTH), k.reshape(bsz, seq, ATTN_WIDTH),
                        v.reshape(bsz, seq, ATTN_WIDTH), bias_tiles, g_subln[i].astype(F32)[:, None])
        ya = ya.reshape(n, ATTN_WIDTH)

        j = i // 2
        if i % 2 == 0:
            router_w = None
        else:
            router_w = (_pad_axis(w_router[j].astype(F32), 1, LANES),
                        jnp.concatenate([b_router[j].astype(F32),
                                         jnp.full((LANES - N_EXPERTS,), -jnp.inf, F32)])[None, :])
        outs = _postmix(x2, ys, ya, w_glu[i].astype(BF16), b_glu[i][None, :], g_ssm_out[i][None, :],
                        w_out[i].astype(BF16), g_ffn[i][None, :], router_w, tm)
        final = i == depth - 1
        gfin = g_final[None, :]
        if i % 2 == 0:
            x2, h = outs
            wg = _pad_axis(dense_w_gate[j].astype(BF16), 1, d_ff_pad)[None]
            wu = _pad_axis(dense_w_up[j].astype(BF16), 1, d_ff_pad)[None]
            wd = _pad_axis(dense_w_down[j].astype(BF16), 0, d_ff_pad)[None]
            x2 = _ffn(h, x2, None, wg, wu, wd, gfin, tm_ffn, 256, final)
        else:
            x2, h, gates = outs
            x2 = _ffn(h, x2, gates, moe_w_gate[j].astype(BF16), moe_w_up[j].astype(BF16),
                      moe_w_down[j].astype(BF16), gfin, tm_ffn, 512, final)
    return x2.reshape(bsz, seq, D_MODEL)
```

```python
import functools
import math

import numpy as np
import jax
import jax.numpy as jnp
from jax import lax
from jax.experimental import pallas as pl
from jax.experimental.pallas import tpu as pltpu

F32 = jnp.float32
BF16 = jnp.bfloat16

D_MODEL = 1024
SSM_WIDTH = 512
SSM_GROUP = 16
SSM_GROUPS = 32
SSM_STATE = 64
ATTN_WIDTH = 512
HEAD_DIM = 64
HEADS = 4
HEAD_V = 2 * HEAD_DIM
N_EXPERTS = 8
NORM_EPS = 1e-6

LANES = 128
SUBLANES = 8
KV_CHUNK = 512
TQ = 256
LOG2E = math.log2(math.e)
MOE_TILE = 1024
MOE_COMBINE_TILE = 512
SSM_CHUNK = 16
CHUNK_COLS = SSM_CHUNK * SSM_GROUP
VMEM_LIMIT = 56 * 1024 * 1024


def _cparams(sem):
    return pltpu.CompilerParams(dimension_semantics=sem, vmem_limit_bytes=VMEM_LIMIT)


def _rms(x, g):
    return x * lax.rsqrt(jnp.mean(x * x, axis=-1, keepdims=True) + NORM_EPS) * g


def _inproj_kernel(x_ref, g_ref, w_ref, u_ref, q_ref, k_ref, v_ref):
    h = _rms(x_ref[...], g_ref[...]).astype(BF16)
    proj = lambda n: jnp.dot(h, w_ref[:, n * 512:(n + 1) * 512], preferred_element_type=F32)
    u_ref[...] = proj(0).astype(BF16)
    q_ref[...] = (proj(1) * (HEAD_DIM ** -0.5 * LOG2E)).astype(BF16)
    k_ref[...] = proj(2).astype(BF16)
    v_ref[...] = proj(3).astype(BF16)


def _inproj(x2, g, w_bf, tm):
    n = x2.shape[0]
    out = jax.ShapeDtypeStruct((n, 512), BF16)
    row = lambda c: pl.BlockSpec((tm, c), lambda i: (i, 0))
    return pl.pallas_call(
        _inproj_kernel,
        out_shape=(out, out, out, out),
        grid=(n // tm,),
        in_specs=[
            row(D_MODEL),
            pl.BlockSpec((1, D_MODEL), lambda i: (0, 0)),
            pl.BlockSpec((D_MODEL, 2048), lambda i: (0, 0)),
        ],
        out_specs=(row(512), row(512), row(512), row(512)),
        compiler_params=_cparams(("parallel",)),
        name="inproj",
    )(x2, g, w_bf)


def _ssm_matrices(lam_re, lam_im, log_step, b_re, b_im, c_re, c_im, d_skip):
    t = SSM_CHUNK
    tau = jnp.arange(t + 1, dtype=F32)
    ks, ws, es, decs = [], [], [], []
    for d in range(2):
        lam = lax.complex(lam_re[d].astype(F32), lam_im[d].astype(F32))
        step = jnp.exp(log_step[d].astype(F32))[:, None]
        ls = lam * step
        lam_bar = jnp.exp(ls)
        pw = jnp.exp(ls[:, None, :] * tau[None, :, None])
        b_bar = ((lam_bar - 1.0) / lam)[:, :, None] * lax.complex(b_re[d].astype(F32), b_im[d].astype(F32))
        c = lax.complex(c_re[d].astype(F32), c_im[d].astype(F32))
        ks.append(jnp.real(jnp.einsum('gop,gtp,gpi->gtoi', c, pw[:, :t], b_bar)))
        if d == 0:
            wpow = pw[:, :t][:, ::-1]
            epow = pw[:, 1:t + 1]
        else:
            wpow = pw[:, :t]
            epow = pw[:, 1:t + 1][:, ::-1]
        ws.append(jnp.einsum('gsp,gpi->gsip', wpow, b_bar).reshape(SSM_GROUPS, CHUNK_COLS, SSM_STATE))
        es.append(jnp.einsum('gop,gtp->gpto', c, epow).reshape(SSM_GROUPS, SSM_STATE, CHUNK_COLS))
        decs.append(pw[:, t])
    kf, kb = ks
    s_idx = jnp.arange(t)[:, None]
    t_idx = jnp.arange(t)[None, :]
    lag_f = t_idx - s_idx
    lag_b = s_idx - t_idx
    mf = jnp.where((lag_f >= 0)[None, :, :, None, None], kf[:, jnp.clip(lag_f, 0, t - 1)], 0.0)
    mb = jnp.where((lag_b >= 0)[None, :, :, None, None], kb[:, jnp.clip(lag_b, 0, t - 1)], 0.0)
    dmat = (jnp.eye(t, dtype=F32)[None, :, :, None, None]
            * (jnp.eye(SSM_GROUP, dtype=F32)[None, None, None] * d_skip.astype(F32).reshape(SSM_GROUPS, 1, 1, 1, SSM_GROUP)))
    m = (mf + mb + dmat).transpose(0, 1, 4, 2, 3).reshape(SSM_GROUPS, CHUNK_COLS, CHUNK_COLS)
    w = jnp.concatenate([jnp.real(ws[0]), jnp.real(ws[1]), jnp.imag(ws[0]), jnp.imag(ws[1])], axis=-1)
    v = jnp.concatenate([jnp.real(es[0]), jnp.real(es[1]), -jnp.imag(es[0]), -jnp.imag(es[1])], axis=1)
    dec_re = jnp.concatenate([jnp.real(decs[0]), jnp.real(decs[1])], axis=-1)[:, None, :]
    dec_im = jnp.concatenate([jnp.imag(decs[0]), jnp.imag(decs[1])], axis=-1)[:, None, :]
    return m.astype(BF16), w.astype(BF16), v.astype(BF16), dec_re, dec_im


def _ssm_kernel(u_ref, m_ref, w_ref, v_ref, ar_ref, ai_ref, y_ref, s_ref, hf_ref, hb_ref, *, n_chunks, rows):
    u = u_ref[0]
    s_ref[...] = jnp.dot(u, w_ref[0], preferred_element_type=F32)
    ar = jnp.broadcast_to(ar_ref[0], (rows, LANES))
    ai = jnp.broadcast_to(ai_ref[0], (rows, LANES))

    def body(c, carry):
        fr, fi, br, bi = carry
        rf = pl.multiple_of(c * rows, rows)
        rb = pl.multiple_of((n_chunks - 1 - c) * rows, rows)
        hf_ref[pl.ds(rf, rows), :LANES] = fr
        hf_ref[pl.ds(rf, rows), LANES:] = fi
        hb_ref[pl.ds(rb, rows), :LANES] = br
        hb_ref[pl.ds(rb, rows), LANES:] = bi
        sf = s_ref[pl.ds(rf, rows), :]
        sb = s_ref[pl.ds(rb, rows), :]
        nfr = ar * fr - ai * fi + sf[:, :LANES]
        nfi = ai * fr + ar * fi + sf[:, LANES:]
        nbr = ar * br - ai * bi + sb[:, :LANES]
        nbi = ai * br + ar * bi + sb[:, LANES:]
        return nfr, nfi, nbr, nbi

    z = jnp.zeros((rows, LANES), F32)
    lax.fori_loop(0, n_chunks, body, (z, z, z, z), unroll=4)
    lane = lax.broadcasted_iota(jnp.int32, (1, 2 * LANES), 1)
    is_fwd = (lane % LANES) < SSM_STATE
    hcat = jnp.where(is_fwd, hf_ref[...], hb_ref[...]).astype(BF16)
    y = jnp.dot(u, m_ref[0], preferred_element_type=F32)
    y = y + jnp.dot(hcat, v_ref[0], preferred_element_type=F32)
    y_ref[0] = jax.nn.gelu(y).astype(BF16)


def _ssm(uc, m, w, v, dec_re, dec_im, n_chunks, rows):
    g, r, _ = uc.shape
    mat = pl.BlockSpec((1, CHUNK_COLS, CHUNK_COLS), lambda i: (i, 0, 0))
    dec = pl.BlockSpec((1, 1, LANES), lambda i: (i, 0, 0))
    return pl.pallas_call(
        functools.partial(_ssm_kernel, n_chunks=n_chunks, rows=rows),
        out_shape=jax.ShapeDtypeStruct((g, r, CHUNK_COLS), BF16),
        grid=(g,),
        in_specs=[pl.BlockSpec((1, r, CHUNK_COLS), lambda i: (i, 0, 0)), mat, mat, mat, dec, dec],
        out_specs=pl.BlockSpec((1, r, CHUNK_COLS), lambda i: (i, 0, 0)),
        scratch_shapes=[pltpu.VMEM((r, CHUNK_COLS), F32)] * 3,
        compiler_params=_cparams(("parallel",)),
        name="ssm",
    )(uc, m, w, v, dec_re, dec_im)


def _alibi_tiles(slopes2):
    rel = np.arange(TQ, dtype=np.float32)[None, :] - np.arange(KV_CHUNK, dtype=np.float32)[:, None]
    tiles = [np.stack([-(np.float32(s) * rel), np.float32(s) * rel]) for s in slopes2]
    return jnp.asarray(np.stack(tiles), F32)


def _attn_kernel(par_ref, q_ref, k_ref, v_ref, bt_ref, g_ref, o_ref, vt_ref, qv_ref, acc_ref, sa_ref, sb_ref,
                 *, n_chunks):
    head = pl.program_id(1)
    qi = pl.program_id(2)
    lam = par_ref[0]
    out_scale = par_ref[1]
    slope = par_ref[2 + head]

    @pl.when(qi == 0)
    def _():
        for c in range(n_chunks):
            vt_ref[c] = v_ref[0, c * KV_CHUNK:(c + 1) * KV_CHUNK, :].astype(F32).T.astype(BF16)

    qt = q_ref[0].astype(F32).T.astype(BF16)
    zero = jnp.zeros((HEAD_DIM, TQ), BF16)
    qv_ref[0, :HEAD_DIM] = qt[:HEAD_DIM]
    qv_ref[0, HEAD_DIM:] = zero
    qv_ref[1, :HEAD_DIM] = zero
    qv_ref[1, HEAD_DIM:] = qt[HEAD_DIM:]

    qd = qi // (KV_CHUNK // TQ)
    i0 = qi * TQ

    s_bufs = (sa_ref, sb_ref)

    def scores(t):
        buf = s_bufs[t % 2]
        if t == 0:
            c = qd
            c_off = jnp.float32(0.0)
            bias = -jnp.abs(bt_ref[0, 1] + slope * (i0 - qd * KV_CHUNK).astype(F32))
        else:
            c = (t - 1) + ((t - 1) >= qd).astype(jnp.int32)
            c_off = -slope * jnp.abs(i0 - c * KV_CHUNK).astype(F32)
            bias = bt_ref[0, (c > qd).astype(jnp.int32)]
        k = k_ref[0, pl.ds(pl.multiple_of(c * KV_CHUNK, KV_CHUNK), KV_CHUNK), :]
        smax = []
        for mi in range(2):
            s = jnp.dot(k, qv_ref[mi], preferred_element_type=F32) + bias
            buf[mi] = s
            smax.append(jnp.max(s, axis=0, keepdims=True) + c_off)
        return c, c_off, smax

    def accumulate(t, c, c_off, smax, m, l):
        buf = s_bufs[t % 2]
        vt = vt_ref[c]
        for mi in range(2):
            m_new = jnp.maximum(m[mi], smax[mi])
            p = jnp.exp2(buf[mi] - (m_new - c_off))
            alpha = jnp.exp2(m[mi] - m_new)
            l[mi] = alpha * l[mi] + jnp.sum(p, axis=0, keepdims=True)
            acc_ref[mi] = alpha * acc_ref[mi] + jnp.dot(vt, p.astype(BF16), preferred_element_type=F32)
            m[mi] = m_new

    acc_ref[...] = jnp.zeros_like(acc_ref)
    m = [jnp.full((1, TQ), -1e30, F32)] * 2
    l = [jnp.zeros((1, TQ), F32)] * 2
    pending = scores(0)
    for t in range(n_chunks):
        nxt = scores(t + 1) if t + 1 < n_chunks else None
        accumulate(t, *pending, m, l)
        pending = nxt
    l1, l2 = l
    o = acc_ref[0] / l1 - lam * (acc_ref[1] / l2)
    o = o * lax.rsqrt(jnp.mean(o * o, axis=0, keepdims=True) + NORM_EPS)
    o = o * (g_ref[...] * out_scale)
    o_ref[0] = o.T.astype(BF16)


def _attention(par, q, k, v, bias_tiles, g_col):
    b, l, _ = q.shape
    n_chunks = l // KV_CHUNK
    return pl.pallas_call(
        functools.partial(_attn_kernel, n_chunks=n_chunks),
        out_shape=jax.ShapeDtypeStruct((b, l, ATTN_WIDTH), BF16),
        grid=(b, HEADS, l // TQ),
        in_specs=[
            pl.BlockSpec(memory_space=pltpu.SMEM),
            pl.BlockSpec((1, TQ, HEAD_V), lambda bi, h, i: (bi, i, h)),
            pl.BlockSpec((1, l, HEAD_V), lambda bi, h, i: (bi, 0, h)),
            pl.BlockSpec((1, l, HEAD_V), lambda bi, h, i: (bi, 0, h)),
            pl.BlockSpec((1, 2, KV_CHUNK, TQ), lambda bi, h, i: (h, 0, 0, 0)),
            pl.BlockSpec((HEAD_V, 1), lambda bi, h, i: (0, 0)),
        ],
        out_specs=pl.BlockSpec((1, TQ, HEAD_V), lambda bi, h, i: (bi, i, h)),
        scratch_shapes=[
            pltpu.VMEM((n_chunks, HEAD_V, KV_CHUNK), BF16),
            pltpu.VMEM((2, HEAD_V, TQ), BF16),
            pltpu.VMEM((2, HEAD_V, TQ), F32),
            pltpu.VMEM((2, KV_CHUNK, TQ), F32),
            pltpu.VMEM((2, KV_CHUNK, TQ), F32),
        ],
        compiler_params=_cparams(("parallel", "parallel", "arbitrary")),
        name="diff_attn",
    )(par, q, k, v, bias_tiles, g_col)


def _postmix_kernel(x_ref, ys_ref, ya_ref, wglu_ref, bglu_ref, gs_ref, wout_ref, gf_ref, *rest, router):
    if router:
        wr_ref, br_ref, xo_ref, h_ref, gate_ref = rest
    else:
        xo_ref, h_ref = rest
    y = ys_ref[...]
    t = jnp.dot(y, wglu_ref[...], preferred_element_type=F32) + bglu_ref[...]
    y = y.astype(F32) * jax.nn.sigmoid(t)
    y = _rms(y, gs_ref[...]).astype(BF16)
    mix = jnp.dot(y, wout_ref[:SSM_WIDTH, :], preferred_element_type=F32)
    mix = mix + jnp.dot(ya_ref[...], wout_ref[SSM_WIDTH:, :], preferred_element_type=F32)
    x = x_ref[...] + mix
    xo_ref[...] = x
    h = _rms(x, gf_ref[...])
    h_ref[...] = h.astype(h_ref.dtype)
    if router:
        logits = jnp.dot(h, wr_ref[...], precision=lax.Precision.HIGHEST, preferred_element_type=F32) + br_ref[...]
        lane = lax.broadcasted_iota(jnp.int32, logits.shape, 1)
        big = jnp.int32(LANES)
        m1 = jnp.max(logits, axis=-1, keepdims=True)
        i1 = jnp.min(jnp.where(logits == m1, lane, big), axis=-1, keepdims=True)
        rest_l = jnp.where(lane == i1, -jnp.inf, logits)
        m2 = jnp.max(rest_l, axis=-1, keepdims=True)
        i2 = jnp.min(jnp.where(rest_l == m2, lane, big), axis=-1, keepdims=True)
        e2 = jnp.exp(m2 - m1)
        w1 = 1.0 / (1.0 + e2)
        w2 = e2 * w1
        rec = jnp.where(lane == 0, w1, jnp.where(lane == 1, w2, jnp.where(
            lane == 2, i1.astype(F32), jnp.where(lane == 3, i2.astype(F32), 0.0))))
        gate_ref[...] = rec[:, :N_EXPERTS]


def _postmix(x2, ys, ya, wglu, bglu, gs, wout, gf, router_w, tm):
    n = x2.shape[0]
    router = router_w is not None
    row = lambda c: pl.BlockSpec((tm, c), lambda i: (i, 0))
    full = lambda r, c: pl.BlockSpec((r, c), lambda i: (0, 0))
    in_specs = [row(D_MODEL), row(512), row(512), full(512, 512), full(1, 512), full(1, 512),
                full(D_MODEL, D_MODEL), full(1, D_MODEL)]
    args = [x2, ys, ya, wglu, bglu, gs, wout, gf]
    out_shape = [jax.ShapeDtypeStruct((n, D_MODEL), F32), jax.ShapeDtypeStruct((n, D_MODEL), F32 if router else BF16)]
    out_specs = [row(D_MODEL), row(D_MODEL)]
    if router:
        in_specs += [full(D_MODEL, LANES), full(1, LANES)]
        args += list(router_w)
        out_shape.append(jax.ShapeDtypeStruct((n, N_EXPERTS), F32))
        out_specs.append(row(N_EXPERTS))
    return pl.pallas_call(
        functools.partial(_postmix_kernel, router=router),
        out_shape=tuple(out_shape),
        grid=(n // tm,),
        in_specs=in_specs,
        out_specs=tuple(out_specs),
        compiler_params=_cparams(("parallel",)),
        name="postmix_router" if router else "postmix",
    )(*args)


def _swiglu_partial(h, wg, wu, wd):
    g = jnp.dot(h, wg, preferred_element_type=F32)
    u = jnp.dot(h, wu, preferred_element_type=F32)
    return jnp.dot((jax.nn.silu(g) * u).astype(BF16), wd, preferred_element_type=F32)


def _ffn_kernel(h_ref, x_ref, wg_ref, wu_ref, wd_ref, o_ref):
    @pl.when(pl.program_id(1) == 0)
    def _():
        o_ref[...] = x_ref[...]

    o_ref[...] += _swiglu_partial(h_ref[...], wg_ref[...], wu_ref[...], wd_ref[...])


def _ffn(h, x2, wg, wu, wd, tm, tf):
    n = x2.shape[0]
    ff = wg.shape[1]
    row = lambda c: pl.BlockSpec((tm, c), lambda i, f: (i, 0))
    return pl.pallas_call(
        _ffn_kernel,
        out_shape=jax.ShapeDtypeStruct((n, D_MODEL), F32),
        grid=(n // tm, ff // tf),
        in_specs=[row(D_MODEL), row(D_MODEL),
                  pl.BlockSpec((D_MODEL, tf), lambda i, f: (0, f)),
                  pl.BlockSpec((D_MODEL, tf), lambda i, f: (0, f)),
                  pl.BlockSpec((tf, D_MODEL), lambda i, f: (f, 0))],
        out_specs=row(D_MODEL),
        compiler_params=_cparams(("parallel", "arbitrary")),
        name="ffn_dense",
    )(h, x2, wg, wu, wd)


def _route(rec, tm):
    n = rec.shape[0]
    a = 2 * n
    n_tiles = a // tm + N_EXPERTS
    w = rec[:, :2].reshape(a)
    e = rec[:, 2:4].astype(jnp.int32).reshape(a)
    onehot = (e[:, None] == jnp.arange(N_EXPERTS, dtype=jnp.int32)[None, :]).astype(jnp.int32)
    csum = jnp.cumsum(onehot, axis=0)
    rank = jnp.sum(csum * onehot, axis=1) - 1
    padded = (csum[-1] + tm - 1) // tm * tm
    ends = jnp.cumsum(padded)
    pos = jnp.sum(onehot * (ends - padded)[None, :], axis=1) + rank
    rows = n_tiles * tm
    tok_sorted = jnp.zeros((rows,), jnp.int32).at[pos].set(jnp.arange(a, dtype=jnp.int32) // 2, unique_indices=True)
    w_sorted = jnp.zeros((rows,), F32).at[pos].set(w, unique_indices=True)
    n_used = (ends[-1] // tm).astype(jnp.int32)
    tile = jnp.arange(n_tiles, dtype=jnp.int32)
    texp = jnp.sum((tile[:, None] >= (ends // tm)[None, :]).astype(jnp.int32), axis=1)
    texp = jnp.minimum(texp, N_EXPERTS - 1)
    texp = jnp.where(tile < n_used, texp, texp[jnp.maximum(n_used - 1, 0)])
    return tok_sorted.reshape(n_tiles, tm), w_sorted[:, None], pos.reshape(n, 2), texp, n_used.reshape(1)


def _row_gather(idx_smem, src_hbm, dst, sem, count):
    def issue(r, carry):
        pltpu.make_async_copy(src_hbm.at[pl.ds(idx_smem[r], 1)], dst.at[pl.ds(r, 1)], sem).start()
        return carry

    lax.fori_loop(0, count, issue, 0, unroll=8)
    pltpu.make_async_copy(src_hbm.at[pl.ds(0, count)], dst, sem).wait()


def _moe_gather_kernel(nused_ref, tok_hbm, h_hbm, o_ref, idx_smem, buf, sem_idx, sem):
    t = pl.program_id(0)

    @pl.when(t < nused_ref[0])
    def _():
        cp = pltpu.make_async_copy(tok_hbm.at[t], idx_smem, sem_idx)
        cp.start()
        cp.wait()
        _row_gather(idx_smem, h_hbm, buf, sem, buf.shape[0])
        o_ref[...] = buf[...].astype(BF16)

    @pl.when(t >= nused_ref[0])
    def _():
        o_ref[...] = jnp.zeros_like(o_ref)


def _moe_gather(n_used, tok2d, h):
    n_tiles, tm = tok2d.shape
    return pl.pallas_call(
        _moe_gather_kernel,
        out_shape=jax.ShapeDtypeStruct((n_tiles * tm, D_MODEL), BF16),
        grid_spec=pltpu.PrefetchScalarGridSpec(
            num_scalar_prefetch=1,
            grid=(n_tiles,),
            in_specs=[pl.BlockSpec(memory_space=pl.ANY), pl.BlockSpec(memory_space=pl.ANY)],
            out_specs=pl.BlockSpec((tm, D_MODEL), lambda t, nu: (t, 0)),
            scratch_shapes=[pltpu.SMEM((tm,), jnp.int32), pltpu.VMEM((tm, D_MODEL), F32),
                            pltpu.SemaphoreType.DMA, pltpu.SemaphoreType.DMA],
        ),
        compiler_params=_cparams(("arbitrary",)),
        name="moe_gather",
    )(n_used, tok2d, h)


def _moe_ffn_kernel(texp_ref, nused_ref, xs_ref, ws_ref, wg_ref, wu_ref, wd_ref, o_ref):
    t = pl.program_id(0)
    f = pl.program_id(1)
    last = f == pl.num_programs(1) - 1
    valid = t < nused_ref[0]

    @pl.when(valid)
    def _():
        d = _swiglu_partial(xs_ref[...], wg_ref[0], wu_ref[0], wd_ref[0])

        @pl.when(f == 0)
        def _():
            o_ref[...] = d

        @pl.when(f > 0)
        def _():
            o_ref[...] += d

        @pl.when(last)
        def _():
            o_ref[...] = o_ref[...] * ws_ref[...]

    @pl.when(jnp.logical_and(jnp.logical_not(valid), last))
    def _():
        o_ref[...] = jnp.zeros_like(o_ref)


def _moe_ffn(texp, n_used, xs, ws, wg, wu, wd, tm, tf):
    rows = xs.shape[0]
    ff = wg.shape[-1]
    nf = ff // tf
    fidx = lambda t, f, nu: jnp.where(t < nu[0], f, nf - 1)
    row = lambda c: pl.BlockSpec((tm, c), lambda t, f, te, nu: (t, 0))
    return pl.pallas_call(
        _moe_ffn_kernel,
        out_shape=jax.ShapeDtypeStruct((rows, D_MODEL), F32),
        grid_spec=pltpu.PrefetchScalarGridSpec(
            num_scalar_prefetch=2,
            grid=(rows // tm, nf),
            in_specs=[row(D_MODEL), row(1),
                      pl.BlockSpec((1, D_MODEL, tf), lambda t, f, te, nu: (te[t], 0, fidx(t, f, nu))),
                      pl.BlockSpec((1, D_MODEL, tf), lambda t, f, te, nu: (te[t], 0, fidx(t, f, nu))),
                      pl.BlockSpec((1, tf, D_MODEL), lambda t, f, te, nu: (te[t], fidx(t, f, nu), 0))],
            out_specs=row(D_MODEL),
        ),
        compiler_params=_cparams(("arbitrary", "arbitrary")),
        name="moe_ffn",
    )(texp, n_used, xs, ws, wg, wu, wd)


def _moe_combine_kernel(pos_hbm, ys_hbm, x_ref, gfin_ref, o_ref, idx_smem, buf, sem_idx, sem, *, final):
    cp = pltpu.make_async_copy(pos_hbm.at[pl.program_id(0)], idx_smem, sem_idx)
    cp.start()
    cp.wait()
    _row_gather(idx_smem, ys_hbm, buf, sem, buf.shape[0])
    tc = x_ref.shape[0]
    out = x_ref[...] + buf[:tc] + buf[tc:]
    if final:
        out = _rms(out, gfin_ref[...])
    o_ref[...] = out


def _moe_combine(pos, ys, x2, gfin, tc, final):
    n = x2.shape[0]
    pos2d = pos.reshape(n // tc, tc, 2).transpose(0, 2, 1).reshape(n // tc, 2 * tc)
    row = pl.BlockSpec((tc, D_MODEL), lambda i: (i, 0))
    return pl.pallas_call(
        functools.partial(_moe_combine_kernel, final=final),
        out_shape=jax.ShapeDtypeStruct((n, D_MODEL), F32),
        grid=(n // tc,),
        in_specs=[pl.BlockSpec(memory_space=pl.ANY), pl.BlockSpec(memory_space=pl.ANY), row,
                  pl.BlockSpec((1, D_MODEL), lambda i: (0, 0))],
        out_specs=row,
        scratch_shapes=[pltpu.SMEM((2 * tc,), jnp.int32), pltpu.VMEM((2 * tc, D_MODEL), F32),
                        pltpu.SemaphoreType.DMA, pltpu.SemaphoreType.DMA],
        compiler_params=_cparams(("arbitrary",)),
        name="moe_combine",
    )(pos2d, ys, x2, gfin)


def _pad_axis(a, axis, size):
    pad = [(0, 0)] * a.ndim
    pad[axis] = (0, size - a.shape[axis])
    return jnp.pad(a, pad)


def kernel(x, g_mix, w_in, ssm_lambda_re, ssm_lambda_im, ssm_log_step, ssm_b_re, ssm_b_im, ssm_c_re, ssm_c_im, ssm_d, w_glu, b_glu, g_ssm_out, lambda_q1, lambda_k1, lambda_q2, lambda_k2, g_subln, w_out, g_ffn, dense_w_gate, dense_w_up, dense_w_down, w_router, b_router, moe_w_gate, moe_w_up, moe_w_down, g_final):
    bsz, seq, _ = x.shape
    n = bsz * seq
    depth = w_in.shape[0]
    n_chunks = seq // SSM_CHUNK
    assert seq % KV_CHUNK == 0 and bsz == SUBLANES and depth % 2 == 0 and (2 * n) % MOE_TILE == 0
    tm = min(512, n)
    tm_ffn = min(1024, n)
    d_ff = dense_w_gate.shape[-1]
    d_ff_pad = -(-d_ff // 256) * 256
    slopes2 = [2.0 ** (-8.0 * (h + 1) / HEADS) * LOG2E for h in range(HEADS)]
    bias_tiles = _alibi_tiles(slopes2)

    x2 = x.reshape(n, D_MODEL)
    for i in range(depth):
        lambda_init = 0.8 - 0.6 * math.exp(-0.3 * i)
        u, q, k, v = _inproj(x2, g_mix[i][None, :], w_in[i].astype(BF16), tm)

        m, w, vv, dec_re, dec_im = _ssm_matrices(
            ssm_lambda_re[i], ssm_lambda_im[i], ssm_log_step[i], ssm_b_re[i], ssm_b_im[i],
            ssm_c_re[i], ssm_c_im[i], ssm_d[i])
        uc = u.reshape(bsz, n_chunks, SSM_CHUNK, SSM_GROUPS, SSM_GROUP).transpose(3, 1, 0, 2, 4)
        uc = uc.reshape(SSM_GROUPS, n_chunks * bsz, CHUNK_COLS)
        yc = _ssm(uc, m, w, vv, dec_re, dec_im, n_chunks, bsz)
        ys = yc.reshape(SSM_GROUPS, n_chunks, bsz, SSM_CHUNK, SSM_GROUP).transpose(2, 1, 3, 0, 4)
        ys = ys.reshape(n, SSM_WIDTH)

        lam = (jnp.exp(jnp.sum(lambda_q1[i].astype(F32) * lambda_k1[i].astype(F32)))
               - jnp.exp(jnp.sum(lambda_q2[i].astype(F32) * lambda_k2[i].astype(F32))) + lambda_init)
        par = jnp.concatenate([jnp.stack([lam, jnp.asarray(1.0 - lambda_init, F32)]),
                               jnp.asarray(slopes2, F32), jnp.zeros((2,), F32)])
        ya = _attention(par, q.reshape(bsz, seq, ATTN_WIDTH), k.reshape(bsz, seq, ATTN_WIDTH),
                        v.reshape(bsz, seq, ATTN_WIDTH), bias_tiles, g_subln[i].astype(F32)[:, None])
        ya = ya.reshape(n, ATTN_WIDTH)

        j = i // 2
        if i % 2 == 0:
            router_w = None
        else:
            router_w = (_pad_axis(w_router[j].astype(F32), 1, LANES),
                        jnp.concatenate([b_router[j].astype(F32),
                                         jnp.full((LANES - N_EXPERTS,), -jnp.inf, F32)])[None, :])
        outs = _postmix(x2, ys, ya, w_glu[i].astype(BF16), b_glu[i][None, :], g_ssm_out[i][None, :],
                        w_out[i].astype(BF16), g_ffn[i][None, :], router_w, tm)
        if i % 2 == 0:
            x2, h = outs
            wg = _pad_axis(dense_w_gate[j].astype(BF16), 1, d_ff_pad)
            wu = _pad_axis(dense_w_up[j].astype(BF16), 1, d_ff_pad)
            wd = _pad_axis(dense_w_down[j].astype(BF16), 0, d_ff_pad)
            x2 = _ffn(h, x2, wg, wu, wd, tm_ffn, 256)
        else:
            x2, h, rec = outs
            tok2d, w_sorted, pos, texp, n_used = _route(rec, MOE_TILE)
            xs = _moe_gather(n_used, tok2d, h)
            ys = _moe_ffn(texp, n_used, xs, w_sorted, moe_w_gate[j].astype(BF16), moe_w_up[j].astype(BF16),
                          moe_w_down[j].astype(BF16), MOE_TILE, 512)
            x2 = _moe_combine(pos, ys, x2, g_final[None, :], min(MOE_COMBINE_TILE, n), i == depth - 1)
    return x2.reshape(bsz, seq, D_MODEL)
```

```python
import functools
import math

import numpy as np
import jax
import jax.numpy as jnp
from jax import lax
from jax.experimental import pallas as pl
from jax.experimental.pallas import tpu as pltpu

F32 = jnp.float32
BF16 = jnp.bfloat16

D_MODEL = 1024
SSM_WIDTH = 512
SSM_GROUP = 16
SSM_GROUPS = 32
SSM_STATE = 64
ATTN_WIDTH = 512
HEAD_DIM = 64
HEADS = 4
HEAD_V = 2 * HEAD_DIM
N_EXPERTS = 8
NORM_EPS = 1e-6

LANES = 128
SUBLANES = 8
KV_CHUNK = 512
TQ = 256
LOG2E = math.log2(math.e)
MOE_TILE = 1024
MOE_COMBINE_TILE = 512
SSM_CHUNK = 16
CHUNK_COLS = SSM_CHUNK * SSM_GROUP
GROUP_BLOCK = LANES // SSM_GROUP
STEP_PAIRS = SSM_CHUNK // 2
VMEM_LIMIT = 56 * 1024 * 1024


def _cparams(sem):
    return pltpu.CompilerParams(dimension_semantics=sem, vmem_limit_bytes=VMEM_LIMIT)


def _rms(x, g):
    return x * lax.rsqrt(jnp.mean(x * x, axis=-1, keepdims=True) + NORM_EPS) * g


def _inproj_kernel(x_ref, g_ref, w_ref, u_ref, q_ref, k_ref, v_ref, us_ref):
    h = _rms(x_ref[...], g_ref[...]).astype(BF16)
    proj = lambda n: jnp.dot(h, w_ref[:, n * 512:(n + 1) * 512], preferred_element_type=F32)
    z = proj(0)
    n_chunks = us_ref.shape[1] // SSM_CHUNK
    for j in range(SSM_WIDTH // LANES):
        us_ref[j] = z[:, j * LANES:(j + 1) * LANES]
        for s in range(SSM_CHUNK):
            u_ref[s, :, j * LANES:(j + 1) * LANES] = us_ref[j, pl.ds(s, n_chunks, stride=SSM_CHUNK), :].astype(BF16)
    q_ref[...] = (proj(1) * (HEAD_DIM ** -0.5 * LOG2E)).astype(BF16)
    k_ref[...] = proj(2).astype(BF16)
    v_ref[...] = proj(3).astype(BF16)


def _inproj(x2, g, w_bf, tm):
    n = x2.shape[0]
    out = jax.ShapeDtypeStruct((n, 512), BF16)
    row = lambda c: pl.BlockSpec((tm, c), lambda i: (i, 0))
    return pl.pallas_call(
        _inproj_kernel,
        out_shape=(jax.ShapeDtypeStruct((SSM_CHUNK, n // SSM_CHUNK, SSM_WIDTH), BF16), out, out, out),
        grid=(n // tm,),
        in_specs=[
            row(D_MODEL),
            pl.BlockSpec((1, D_MODEL), lambda i: (0, 0)),
            pl.BlockSpec((D_MODEL, 2048), lambda i: (0, 0)),
        ],
        out_specs=(pl.BlockSpec((SSM_CHUNK, tm // SSM_CHUNK, SSM_WIDTH), lambda i: (0, i, 0)),
                   row(512), row(512), row(512)),
        scratch_shapes=[pltpu.VMEM((SSM_WIDTH // LANES, tm, LANES), F32)],
        compiler_params=_cparams(("parallel",)),
        name="inproj",
    )(x2, g, w_bf)


def _ssm_matrices(lam_re, lam_im, log_step, b_re, b_im, c_re, c_im, d_skip):
    t = SSM_CHUNK
    tau = jnp.arange(t + 1, dtype=F32)
    ks, ws, es, decs = [], [], [], []
    for d in range(2):
        lam = lax.complex(lam_re[d].astype(F32), lam_im[d].astype(F32))
        step = jnp.exp(log_step[d].astype(F32))[:, None]
        ls = lam * step
        lam_bar = jnp.exp(ls)
        pw = jnp.exp(ls[:, None, :] * tau[None, :, None])
        b_bar = ((lam_bar - 1.0) / lam)[:, :, None] * lax.complex(b_re[d].astype(F32), b_im[d].astype(F32))
        c = lax.complex(c_re[d].astype(F32), c_im[d].astype(F32))
        ks.append(jnp.real(jnp.einsum('gop,gtp,gpi->gtoi', c, pw[:, :t], b_bar)))
        if d == 0:
            wpow = pw[:, :t][:, ::-1]
            epow = pw[:, 1:t + 1]
        else:
            wpow = pw[:, :t]
            epow = pw[:, 1:t + 1][:, ::-1]
        ws.append(jnp.einsum('gsp,gpi->gsip', wpow, b_bar).reshape(SSM_GROUPS, CHUNK_COLS, SSM_STATE))
        es.append(jnp.einsum('gop,gtp->gpto', c, epow).reshape(SSM_GROUPS, SSM_STATE, CHUNK_COLS))
        decs.append(pw[:, t])
    kf, kb = ks
    s_idx = jnp.arange(t)[:, None]
    t_idx = jnp.arange(t)[None, :]
    lag_f = t_idx - s_idx
    lag_b = s_idx - t_idx
    mf = jnp.where((lag_f >= 0)[None, :, :, None, None], kf[:, jnp.clip(lag_f, 0, t - 1)], 0.0)
    mb = jnp.where((lag_b >= 0)[None, :, :, None, None], kb[:, jnp.clip(lag_b, 0, t - 1)], 0.0)
    dmat = (jnp.eye(t, dtype=F32)[None, :, :, None, None]
            * (jnp.eye(SSM_GROUP, dtype=F32)[None, None, None] * d_skip.astype(F32).reshape(SSM_GROUPS, 1, 1, 1, SSM_GROUP)))
    m = (mf + mb + dmat).transpose(0, 1, 4, 2, 3).reshape(SSM_GROUPS, CHUNK_COLS, CHUNK_COLS)
    w = jnp.concatenate([jnp.real(ws[0]), jnp.real(ws[1]), jnp.imag(ws[0]), jnp.imag(ws[1])], axis=-1)
    v = jnp.concatenate([jnp.real(es[0]), jnp.real(es[1]), -jnp.imag(es[0]), -jnp.imag(es[1])], axis=1)
    dec_re = jnp.concatenate([jnp.real(decs[0]), jnp.real(decs[1])], axis=-1)[:, None, :]
    dec_im = jnp.concatenate([jnp.imag(decs[0]), jnp.imag(decs[1])], axis=-1)[:, None, :]
    return m.astype(BF16), w.astype(BF16), v.astype(BF16), dec_re, dec_im


def _ssm_kernel(u_ref, m_ref, w_ref, v_ref, ar_ref, ai_ref, y_ref, s_ref, hf_ref, hb_ref, *, n_chunks, rows):
    u = u_ref[0]
    s_ref[...] = jnp.dot(u, w_ref[0], preferred_element_type=F32)
    ar = jnp.broadcast_to(ar_ref[0], (rows, LANES))
    ai = jnp.broadcast_to(ai_ref[0], (rows, LANES))

    def body(c, carry):
        fr, fi, br, bi = carry
        rf = pl.multiple_of(c * rows, rows)
        rb = pl.multiple_of((n_chunks - 1 - c) * rows, rows)
        hf_ref[pl.ds(rf, rows), :LANES] = fr
        hf_ref[pl.ds(rf, rows), LANES:] = fi
        hb_ref[pl.ds(rb, rows), :LANES] = br
        hb_ref[pl.ds(rb, rows), LANES:] = bi
        sf = s_ref[pl.ds(rf, rows), :]
        sb = s_ref[pl.ds(rb, rows), :]
        nfr = ar * fr - ai * fi + sf[:, :LANES]
        nfi = ai * fr + ar * fi + sf[:, LANES:]
        nbr = ar * br - ai * bi + sb[:, :LANES]
        nbi = ai * br + ar * bi + sb[:, LANES:]
        return nfr, nfi, nbr, nbi

    z = jnp.zeros((rows, LANES), F32)
    lax.fori_loop(0, n_chunks, body, (z, z, z, z), unroll=4)
    lane = lax.broadcasted_iota(jnp.int32, (1, 2 * LANES), 1)
    is_fwd = (lane % LANES) < SSM_STATE
    hcat = jnp.where(is_fwd, hf_ref[...], hb_ref[...]).astype(BF16)
    y = jnp.dot(u, m_ref[0], preferred_element_type=F32)
    y = y + jnp.dot(hcat, v_ref[0], preferred_element_type=F32)
    y_ref[0] = jax.nn.gelu(y).astype(BF16)


def _ssm(uc, m, w, v, dec_re, dec_im, n_chunks, rows):
    g, r, _ = uc.shape
    mat = pl.BlockSpec((1, CHUNK_COLS, CHUNK_COLS), lambda i: (i, 0, 0))
    dec = pl.BlockSpec((1, 1, LANES), lambda i: (i, 0, 0))
    return pl.pallas_call(
        functools.partial(_ssm_kernel, n_chunks=n_chunks, rows=rows),
        out_shape=jax.ShapeDtypeStruct((g, r, CHUNK_COLS), BF16),
        grid=(g,),
        in_specs=[pl.BlockSpec((1, r, CHUNK_COLS), lambda i: (i, 0, 0)), mat, mat, mat, dec, dec],
        out_specs=pl.BlockSpec((1, r, CHUNK_COLS), lambda i: (i, 0, 0)),
        scratch_shapes=[pltpu.VMEM((r, CHUNK_COLS), F32)] * 3,
        compiler_params=_cparams(("parallel",)),
        name="ssm",
    )(uc, m, w, v, dec_re, dec_im)


def _regroup_matrices():
    p = np.zeros((STEP_PAIRS, 2, GROUP_BLOCK, SSM_GROUP, GROUP_BLOCK, SSM_CHUNK, SSM_GROUP), np.float32)
    gl = np.arange(GROUP_BLOCK)[:, None]
    h = np.arange(SSM_GROUP)[None, :]
    for sp in range(STEP_PAIRS):
        for half in range(2):
            p[sp, half, gl, h, gl, sp + half * STEP_PAIRS, h] = 1.0
    p = p.reshape(STEP_PAIRS, 2 * LANES, GROUP_BLOCK, CHUNK_COLS)
    q = p.transpose(0, 2, 3, 1)
    return (jnp.asarray(p.reshape(STEP_PAIRS, 2 * LANES, GROUP_BLOCK * CHUNK_COLS), BF16), jnp.asarray(q, BF16))


def _regroup_in_kernel(us_ref, p_ref, o_ref):
    pairs = [jnp.concatenate([us_ref[sp], us_ref[sp + STEP_PAIRS]], axis=1) for sp in range(STEP_PAIRS)]
    for gl in range(GROUP_BLOCK):
        acc = None
        for sp in range(STEP_PAIRS):
            d = jnp.dot(pairs[sp], p_ref[sp, :, gl * CHUNK_COLS:(gl + 1) * CHUNK_COLS], preferred_element_type=F32)
            acc = d if acc is None else acc + d
        o_ref[gl] = acc.astype(BF16)


def _regroup_out_kernel(yc_ref, q_ref, o_ref):
    for tp in range(STEP_PAIRS):
        acc = None
        for gl in range(GROUP_BLOCK):
            d = jnp.dot(yc_ref[gl], q_ref[tp, gl], preferred_element_type=F32)
            acc = d if acc is None else acc + d
        o_ref[tp] = acc[:, :LANES].astype(BF16)
        o_ref[tp + STEP_PAIRS] = acc[:, LANES:].astype(BF16)


def _regroup_specs(c):
    steps = lambda f: pl.BlockSpec((SSM_CHUNK, c, LANES), f)
    groups = lambda f: pl.BlockSpec((GROUP_BLOCK, c, CHUNK_COLS), f)
    return steps(lambda gb, b: (0, b, gb)), groups(lambda gb, b: (gb, 0, b))


def _regroup_in(us, pmat, bsz):
    c = us.shape[1] // bsz
    steps, groups = _regroup_specs(c)
    return pl.pallas_call(
        _regroup_in_kernel,
        out_shape=jax.ShapeDtypeStruct((SSM_GROUPS, c, bsz * CHUNK_COLS), BF16),
        grid=(SSM_GROUPS // GROUP_BLOCK, bsz),
        in_specs=[steps, pl.BlockSpec(pmat.shape, lambda gb, b: (0, 0, 0))],
        out_specs=groups,
        compiler_params=_cparams(("parallel", "parallel")),
        name="regroup_in",
    )(us, pmat)


def _regroup_out(yc, qmat, bsz):
    c = yc.shape[1]
    steps, groups = _regroup_specs(c)
    return pl.pallas_call(
        _regroup_out_kernel,
        out_shape=jax.ShapeDtypeStruct((SSM_CHUNK, c * bsz, SSM_WIDTH), BF16),
        grid=(SSM_GROUPS // GROUP_BLOCK, bsz),
        in_specs=[groups, pl.BlockSpec(qmat.shape, lambda gb, b: (0, 0, 0, 0))],
        out_specs=steps,
        compiler_params=_cparams(("parallel", "parallel")),
        name="regroup_out",
    )(yc, qmat)


def _alibi_tiles(slopes2):
    rel = np.arange(TQ, dtype=np.float32)[None, :] - np.arange(KV_CHUNK, dtype=np.float32)[:, None]
    tiles = [np.stack([-(np.float32(s) * rel), np.float32(s) * rel]) for s in slopes2]
    return jnp.asarray(np.stack(tiles), F32)


def _attn_kernel(par_ref, q_ref, k_ref, v_ref, bt_ref, g_ref, o_ref, vt_ref, qv_ref, acc_ref, sa_ref, sb_ref,
                 *, n_chunks):
    head = pl.program_id(1)
    qi = pl.program_id(2)
    lam = par_ref[0]
    out_scale = par_ref[1]
    slope = par_ref[2 + head]

    @pl.when(qi == 0)
    def _():
        for c in range(n_chunks):
            vt_ref[c] = v_ref[0, c * KV_CHUNK:(c + 1) * KV_CHUNK, :].astype(F32).T.astype(BF16)

    qt = q_ref[0].astype(F32).T.astype(BF16)
    zero = jnp.zeros((HEAD_DIM, TQ), BF16)
    qv_ref[0, :HEAD_DIM] = qt[:HEAD_DIM]
    qv_ref[0, HEAD_DIM:] = zero
    qv_ref[1, :HEAD_DIM] = zero
    qv_ref[1, HEAD_DIM:] = qt[HEAD_DIM:]

    qd = qi // (KV_CHUNK // TQ)
    i0 = qi * TQ

    s_bufs = (sa_ref, sb_ref)

    def scores(t):
        buf = s_bufs[t % 2]
        if t == 0:
            c = qd
            c_off = jnp.float32(0.0)
            bias = -jnp.abs(bt_ref[0, 1] + slope * (i0 - qd * KV_CHUNK).astype(F32))
        else:
            c = (t - 1) + ((t - 1) >= qd).astype(jnp.int32)
            c_off = -slope * jnp.abs(i0 - c * KV_CHUNK).astype(F32)
            bias = bt_ref[0, (c > qd).astype(jnp.int32)]
        k = k_ref[0, pl.ds(pl.multiple_of(c * KV_CHUNK, KV_CHUNK), KV_CHUNK), :]
        smax = []
        for mi in range(2):
            s = jnp.dot(k, qv_ref[mi], preferred_element_type=F32) + bias
            buf[mi] = s
            smax.append(jnp.max(s, axis=0, keepdims=True) + c_off)
        return c, c_off, smax

    def accumulate(t, c, c_off, smax, m, l):
        buf = s_bufs[t % 2]
        vt = vt_ref[c]
        for mi in range(2):
            m_new = jnp.maximum(m[mi], smax[mi])
            p = jnp.exp2(buf[mi] - (m_new - c_off))
            alpha = jnp.exp2(m[mi] - m_new)
            l[mi] = alpha * l[mi] + jnp.sum(p, axis=0, keepdims=True)
            acc_ref[mi] = alpha * acc_ref[mi] + jnp.dot(vt, p.astype(BF16), preferred_element_type=F32)
            m[mi] = m_new

    acc_ref[...] = jnp.zeros_like(acc_ref)
    m = [jnp.full((1, TQ), -1e30, F32)] * 2
    l = [jnp.zeros((1, TQ), F32)] * 2
    pending = scores(0)
    for t in range(n_chunks):
        nxt = scores(t + 1) if t + 1 < n_chunks else None
        accumulate(t, *pending, m, l)
        pending = nxt
    l1, l2 = l
    o = acc_ref[0] / l1 - lam * (acc_ref[1] / l2)
    o = o * lax.rsqrt(jnp.mean(o * o, axis=0, keepdims=True) + NORM_EPS)
    o = o * (g_ref[...] * out_scale)
    o_ref[0] = o.T.astype(BF16)


def _attention(par, q, k, v, bias_tiles, g_col):
    b, l, _ = q.shape
    n_chunks = l // KV_CHUNK
    return pl.pallas_call(
        functools.partial(_attn_kernel, n_chunks=n_chunks),
        out_shape=jax.ShapeDtypeStruct((b, l, ATTN_WIDTH), BF16),
        grid=(b, HEADS, l // TQ),
        in_specs=[
            pl.BlockSpec(memory_space=pltpu.SMEM),
            pl.BlockSpec((1, TQ, HEAD_V), lambda bi, h, i: (bi, i, h)),
            pl.BlockSpec((1, l, HEAD_V), lambda bi, h, i: (bi, 0, h)),
            pl.BlockSpec((1, l, HEAD_V), lambda bi, h, i: (bi, 0, h)),
            pl.BlockSpec((1, 2, KV_CHUNK, TQ), lambda bi, h, i: (h, 0, 0, 0)),
            pl.BlockSpec((HEAD_V, 1), lambda bi, h, i: (0, 0)),
        ],
        out_specs=pl.BlockSpec((1, TQ, HEAD_V), lambda bi, h, i: (bi, i, h)),
        scratch_shapes=[
            pltpu.VMEM((n_chunks, HEAD_V, KV_CHUNK), BF16),
            pltpu.VMEM((2, HEAD_V, TQ), BF16),
            pltpu.VMEM((2, HEAD_V, TQ), F32),
            pltpu.VMEM((2, KV_CHUNK, TQ), F32),
            pltpu.VMEM((2, KV_CHUNK, TQ), F32),
        ],
        compiler_params=_cparams(("parallel", "parallel", "arbitrary")),
        name="diff_attn",
    )(par, q, k, v, bias_tiles, g_col)


def _postmix_kernel(x_ref, ys_ref, ya_ref, wglu_ref, bglu_ref, gs_ref, wout_ref, gf_ref, *rest, router):
    if router:
        wr_ref, br_ref, xo_ref, h_ref, gate_ref, ysc_ref = rest
    else:
        xo_ref, h_ref, ysc_ref = rest
    n_chunks = ys_ref.shape[1]
    for j in range(SSM_WIDTH // LANES):
        for s in range(SSM_CHUNK):
            ysc_ref[j, pl.ds(s, n_chunks, stride=SSM_CHUNK), :] = ys_ref[s, :, j * LANES:(j + 1) * LANES].astype(F32)
    y = jnp.concatenate([ysc_ref[j] for j in range(SSM_WIDTH // LANES)], axis=1)
    t = jnp.dot(y.astype(BF16), wglu_ref[...], preferred_element_type=F32) + bglu_ref[...]
    y = y * jax.nn.sigmoid(t)
    y = _rms(y, gs_ref[...]).astype(BF16)
    mix = jnp.dot(y, wout_ref[:SSM_WIDTH, :], preferred_element_type=F32)
    mix = mix + jnp.dot(ya_ref[...], wout_ref[SSM_WIDTH:, :], preferred_element_type=F32)
    x = x_ref[...] + mix
    xo_ref[...] = x
    h = _rms(x, gf_ref[...])
    h_ref[...] = h.astype(h_ref.dtype)
    if router:
        logits = jnp.dot(h, wr_ref[...], precision=lax.Precision.HIGHEST, preferred_element_type=F32) + br_ref[...]
        lane = lax.broadcasted_iota(jnp.int32, logits.shape, 1)
        big = jnp.int32(LANES)
        m1 = jnp.max(logits, axis=-1, keepdims=True)
        i1 = jnp.min(jnp.where(logits == m1, lane, big), axis=-1, keepdims=True)
        rest_l = jnp.where(lane == i1, -jnp.inf, logits)
        m2 = jnp.max(rest_l, axis=-1, keepdims=True)
        i2 = jnp.min(jnp.where(rest_l == m2, lane, big), axis=-1, keepdims=True)
        e2 = jnp.exp(m2 - m1)
        w1 = 1.0 / (1.0 + e2)
        w2 = e2 * w1
        rec = jnp.where(lane == 0, w1, jnp.where(lane == 1, w2, jnp.where(
            lane == 2, i1.astype(F32), jnp.where(lane == 3, i2.astype(F32), 0.0))))
        gate_ref[...] = rec[:, :N_EXPERTS]


def _postmix(x2, ys, ya, wglu, bglu, gs, wout, gf, router_w, tm):
    n = x2.shape[0]
    router = router_w is not None
    row = lambda c: pl.BlockSpec((tm, c), lambda i: (i, 0))
    full = lambda r, c: pl.BlockSpec((r, c), lambda i: (0, 0))
    steps = pl.BlockSpec((SSM_CHUNK, tm // SSM_CHUNK, SSM_WIDTH), lambda i: (0, i, 0))
    in_specs = [row(D_MODEL), steps, row(512), full(512, 512), full(1, 512), full(1, 512),
                full(D_MODEL, D_MODEL), full(1, D_MODEL)]
    args = [x2, ys, ya, wglu, bglu, gs, wout, gf]
    out_shape = [jax.ShapeDtypeStruct((n, D_MODEL), F32), jax.ShapeDtypeStruct((n, D_MODEL), F32 if router else BF16)]
    out_specs = [row(D_MODEL), row(D_MODEL)]
    if router:
        in_specs += [full(D_MODEL, LANES), full(1, LANES)]
        args += list(router_w)
        out_shape.append(jax.ShapeDtypeStruct((n, N_EXPERTS), F32))
        out_specs.append(row(N_EXPERTS))
    return pl.pallas_call(
        functools.partial(_postmix_kernel, router=router),
        out_shape=tuple(out_shape),
        grid=(n // tm,),
        in_specs=in_specs,
        out_specs=tuple(out_specs),
        scratch_shapes=[pltpu.VMEM((SSM_WIDTH // LANES, tm, LANES), F32)],
        compiler_params=_cparams(("parallel",)),
        name="postmix_router" if router else "postmix",
    )(*args)


def _swiglu_partial(h, wg, wu, wd):
    g = jnp.dot(h, wg, preferred_element_type=F32)
    u = jnp.dot(h, wu, preferred_element_type=F32)
    return jnp.dot((jax.nn.silu(g) * u).astype(BF16), wd, preferred_element_type=F32)


def _ffn_kernel(h_ref, x_ref, wg_ref, wu_ref, wd_ref, o_ref):
    @pl.when(pl.program_id(1) == 0)
    def _():
        o_ref[...] = x_ref[...]

    o_ref[...] += _swiglu_partial(h_ref[...], wg_ref[...], wu_ref[...], wd_ref[...])


def _ffn(h, x2, wg, wu, wd, tm, tf):
    n = x2.shape[0]
    ff = wg.shape[1]
    row = lambda c: pl.BlockSpec((tm, c), lambda i, f: (i, 0))
    return pl.pallas_call(
        _ffn_kernel,
        out_shape=jax.ShapeDtypeStruct((n, D_MODEL), F32),
        grid=(n // tm, ff // tf),
        in_specs=[row(D_MODEL), row(D_MODEL),
                  pl.BlockSpec((D_MODEL, tf), lambda i, f: (0, f)),
                  pl.BlockSpec((D_MODEL, tf), lambda i, f: (0, f)),
                  pl.BlockSpec((tf, D_MODEL), lambda i, f: (f, 0))],
        out_specs=row(D_MODEL),
        compiler_params=_cparams(("parallel", "arbitrary")),
        name="ffn_dense",
    )(h, x2, wg, wu, wd)


def _route(rec, tm):
    n = rec.shape[0]
    a = 2 * n
    n_tiles = a // tm + N_EXPERTS
    w = rec[:, :2].reshape(a)
    e = rec[:, 2:4].astype(jnp.int32).reshape(a)
    onehot = (e[:, None] == jnp.arange(N_EXPERTS, dtype=jnp.int32)[None, :]).astype(jnp.int32)
    csum = jnp.cumsum(onehot, axis=0)
    rank = jnp.sum(csum * onehot, axis=1) - 1
    padded = (csum[-1] + tm - 1) // tm * tm
    ends = jnp.cumsum(padded)
    pos = jnp.sum(onehot * (ends - padded)[None, :], axis=1) + rank
    rows = n_tiles * tm
    tok_sorted = jnp.zeros((rows,), jnp.int32).at[pos].set(jnp.arange(a, dtype=jnp.int32) // 2, unique_indices=True)
    w_sorted = jnp.zeros((rows,), F32).at[pos].set(w, unique_indices=True)
    n_used = (ends[-1] // tm).astype(jnp.int32)
    tile = jnp.arange(n_tiles, dtype=jnp.int32)
    texp = jnp.sum((tile[:, None] >= (ends // tm)[None, :]).astype(jnp.int32), axis=1)
    texp = jnp.minimum(texp, N_EXPERTS - 1)
    texp = jnp.where(tile < n_used, texp, texp[jnp.maximum(n_used - 1, 0)])
    return tok_sorted.reshape(n_tiles, tm), w_sorted[:, None], pos.reshape(n, 2), texp, n_used.reshape(1)


def _row_gather(idx_smem, src_hbm, dst, sem, count):
    def issue(r, carry):
        pltpu.make_async_copy(src_hbm.at[pl.ds(idx_smem[r], 1)], dst.at[pl.ds(r, 1)], sem).start()
        return carry

    lax.fori_loop(0, count, issue, 0, unroll=8)
    pltpu.make_async_copy(src_hbm.at[pl.ds(0, count)], dst, sem).wait()


def _moe_gather_kernel(nused_ref, tok_hbm, h_hbm, o_ref, idx_smem, buf, sem_idx, sem):
    t = pl.program_id(0)

    @pl.when(t < nused_ref[0])
    def _():
        cp = pltpu.make_async_copy(tok_hbm.at[t], idx_smem, sem_idx)
        cp.start()
        cp.wait()
        _row_gather(idx_smem, h_hbm, buf, sem, buf.shape[0])
        o_ref[...] = buf[...].astype(BF16)

    @pl.when(t >= nused_ref[0])
    def _():
        o_ref[...] = jnp.zeros_like(o_ref)


def _moe_gather(n_used, tok2d, h):
    n_tiles, tm = tok2d.shape
    return pl.pallas_call(
        _moe_gather_kernel,
        out_shape=jax.ShapeDtypeStruct((n_tiles * tm, D_MODEL), BF16),
        grid_spec=pltpu.PrefetchScalarGridSpec(
            num_scalar_prefetch=1,
            grid=(n_tiles,),
            in_specs=[pl.BlockSpec(memory_space=pl.ANY), pl.BlockSpec(memory_space=pl.ANY)],
            out_specs=pl.BlockSpec((tm, D_MODEL), lambda t, nu: (t, 0)),
            scratch_shapes=[pltpu.SMEM((tm,), jnp.int32), pltpu.VMEM((tm, D_MODEL), F32),
                            pltpu.SemaphoreType.DMA, pltpu.SemaphoreType.DMA],
        ),
        compiler_params=_cparams(("arbitrary",)),
        name="moe_gather",
    )(n_used, tok2d, h)


def _moe_ffn_kernel(texp_ref, nused_ref, xs_ref, ws_ref, wg_ref, wu_ref, wd_ref, o_ref):
    t = pl.program_id(0)
    f = pl.program_id(1)
    last = f == pl.num_programs(1) - 1
    valid = t < nused_ref[0]

    @pl.when(valid)
    def _():
        d = _swiglu_partial(xs_ref[...], wg_ref[0], wu_ref[0], wd_ref[0])

        @pl.when(f == 0)
        def _():
            o_ref[...] = d

        @pl.when(f > 0)
        def _():
            o_ref[...] += d

        @pl.when(last)
        def _():
            o_ref[...] = o_ref[...] * ws_ref[...]

    @pl.when(jnp.logical_and(jnp.logical_not(valid), last))
    def _():
        o_ref[...] = jnp.zeros_like(o_ref)


def _moe_ffn(texp, n_used, xs, ws, wg, wu, wd, tm, tf):
    rows = xs.shape[0]
    ff = wg.shape[-1]
    nf = ff // tf
    fidx = lambda t, f, nu: jnp.where(t < nu[0], f, nf - 1)
    row = lambda c: pl.BlockSpec((tm, c), lambda t, f, te, nu: (t, 0))
    return pl.pallas_call(
        _moe_ffn_kernel,
        out_shape=jax.ShapeDtypeStruct((rows, D_MODEL), F32),
        grid_spec=pltpu.PrefetchScalarGridSpec(
            num_scalar_prefetch=2,
            grid=(rows // tm, nf),
            in_specs=[row(D_MODEL), row(1),
                      pl.BlockSpec((1, D_MODEL, tf), lambda t, f, te, nu: (te[t], 0, fidx(t, f, nu))),
                      pl.BlockSpec((1, D_MODEL, tf), lambda t, f, te, nu: (te[t], 0, fidx(t, f, nu))),
                      pl.BlockSpec((1, tf, D_MODEL), lambda t, f, te, nu: (te[t], fidx(t, f, nu), 0))],
            out_specs=row(D_MODEL),
        ),
        compiler_params=_cparams(("arbitrary", "arbitrary")),
        name="moe_ffn",
    )(texp, n_used, xs, ws, wg, wu, wd)


def _moe_combine_kernel(pos_hbm, ys_hbm, x_ref, gfin_ref, o_ref, idx_smem, buf, sem_idx, sem, *, final):
    cp = pltpu.make_async_copy(pos_hbm.at[pl.program_id(0)], idx_smem, sem_idx)
    cp.start()
    cp.wait()
    _row_gather(idx_smem, ys_hbm, buf, sem, buf.shape[0])
    tc = x_ref.shape[0]
    out = x_ref[...] + buf[:tc] + buf[tc:]
    if final:
        out = _rms(out, gfin_ref[...])
    o_ref[...] = out


def _moe_combine(pos, ys, x2, gfin, tc, final):
    n = x2.shape[0]
    pos2d = pos.reshape(n // tc, tc, 2).transpose(0, 2, 1).reshape(n // tc, 2 * tc)
    row = pl.BlockSpec((tc, D_MODEL), lambda i: (i, 0))
    return pl.pallas_call(
        functools.partial(_moe_combine_kernel, final=final),
        out_shape=jax.ShapeDtypeStruct((n, D_MODEL), F32),
        grid=(n // tc,),
        in_specs=[pl.BlockSpec(memory_space=pl.ANY), pl.BlockSpec(memory_space=pl.ANY), row,
                  pl.BlockSpec((1, D_MODEL), lambda i: (0, 0))],
        out_specs=row,
        scratch_shapes=[pltpu.SMEM((2 * tc,), jnp.int32), pltpu.VMEM((2 * tc, D_MODEL), F32),
                        pltpu.SemaphoreType.DMA, pltpu.SemaphoreType.DMA],
        compiler_params=_cparams(("arbitrary",)),
        name="moe_combine",
    )(pos2d, ys, x2, gfin)


def _pad_axis(a, axis, size):
    pad = [(0, 0)] * a.ndim
    pad[axis] = (0, size - a.shape[axis])
    return jnp.pad(a, pad)


def kernel(x, g_mix, w_in, ssm_lambda_re, ssm_lambda_im, ssm_log_step, ssm_b_re, ssm_b_im, ssm_c_re, ssm_c_im, ssm_d, w_glu, b_glu, g_ssm_out, lambda_q1, lambda_k1, lambda_q2, lambda_k2, g_subln, w_out, g_ffn, dense_w_gate, dense_w_up, dense_w_down, w_router, b_router, moe_w_gate, moe_w_up, moe_w_down, g_final):
    bsz, seq, _ = x.shape
    n = bsz * seq
    depth = w_in.shape[0]
    n_chunks = seq // SSM_CHUNK
    assert seq % KV_CHUNK == 0 and bsz == SUBLANES and depth % 2 == 0 and (2 * n) % MOE_TILE == 0
    tm = min(512, n)
    tm_ffn = min(1024, n)
    d_ff = dense_w_gate.shape[-1]
    d_ff_pad = -(-d_ff // 256) * 256
    slopes2 = [2.0 ** (-8.0 * (h + 1) / HEADS) * LOG2E for h in range(HEADS)]
    bias_tiles = _alibi_tiles(slopes2)
    pmat, qmat = _regroup_matrices()

    x2 = x.reshape(n, D_MODEL)
    for i in range(depth):
        lambda_init = 0.8 - 0.6 * math.exp(-0.3 * i)
        u, q, k, v = _inproj(x2, g_mix[i][None, :], w_in[i].astype(BF16), tm)

        m, w, vv, dec_re, dec_im = _ssm_matrices(
            ssm_lambda_re[i], ssm_lambda_im[i], ssm_log_step[i], ssm_b_re[i], ssm_b_im[i],
            ssm_c_re[i], ssm_c_im[i], ssm_d[i])
        uc = _regroup_in(u, pmat, bsz).reshape(SSM_GROUPS, n_chunks * bsz, CHUNK_COLS)
        yc = _ssm(uc, m, w, vv, dec_re, dec_im, n_chunks, bsz)
        ys = _regroup_out(yc.reshape(SSM_GROUPS, n_chunks, bsz * CHUNK_COLS), qmat, bsz)

        lam = (jnp.exp(jnp.sum(lambda_q1[i].astype(F32) * lambda_k1[i].astype(F32)))
               - jnp.exp(jnp.sum(lambda_q2[i].astype(F32) * lambda_k2[i].astype(F32))) + lambda_init)
        par = jnp.concatenate([jnp.stack([lam, jnp.asarray(1.0 - lambda_init, F32)]),
                               jnp.asarray(slopes2, F32), jnp.zeros((2,), F32)])
        ya = _attention(par, q.reshape(bsz, seq, ATTN_WIDTH), k.reshape(bsz, seq, ATTN_WIDTH),
                        v.reshape(bsz, seq, ATTN_WIDTH), bias_tiles, g_subln[i].astype(F32)[:, None])
        ya = ya.reshape(n, ATTN_WIDTH)

        j = i // 2
        if i % 2 == 0:
            router_w = None
        else:
            router_w = (_pad_axis(w_router[j].astype(F32), 1, LANES),
                        jnp.concatenate([b_router[j].astype(F32),
                                         jnp.full((LANES - N_EXPERTS,), -jnp.inf, F32)])[None, :])
        outs = _postmix(x2, ys, ya, w_glu[i].astype(BF16), b_glu[i][None, :], g_ssm_out[i][None, :],
                        w_out[i].astype(BF16), g_ffn[i][None, :], router_w, tm)
        if i % 2 == 0:
            x2, h = outs
            wg = _pad_axis(dense_w_gate[j].astype(BF16), 1, d_ff_pad)
            wu = _pad_axis(dense_w_up[j].astype(BF16), 1, d_ff_pad)
            wd = _pad_axis(dense_w_down[j].astype(BF16), 0, d_ff_pad)
            x2 = _ffn(h, x2, wg, wu, wd, tm, d_ff_pad // 2)
        else:
            x2, h, rec = outs
            tok2d, w_sorted, pos, texp, n_used = _route(rec, MOE_TILE)
            xs = _moe_gather(n_used, tok2d, h)
            ys = _moe_ffn(texp, n_used, xs, w_sorted, moe_w_gate[j].astype(BF16), moe_w_up[j].astype(BF16),
                          moe_w_down[j].astype(BF16), MOE_TILE, 512)
            x2 = _moe_combine(pos, ys, x2, g_final[None, :], min(MOE_COMBINE_TILE, n), i == depth - 1)
    return x2.reshape(bsz, seq, D_MODEL)
```

```python
import functools
import math

import numpy as np
import jax
import jax.numpy as jnp
from jax import lax
from jax.experimental import pallas as pl
from jax.experimental.pallas import tpu as pltpu

F32 = jnp.float32
BF16 = jnp.bfloat16

D_MODEL = 1024
SSM_WIDTH = 512
SSM_GROUP = 16
SSM_GROUPS = 32
SSM_STATE = 64
ATTN_WIDTH = 512
HEAD_DIM = 64
HEADS = 4
HEAD_V = 2 * HEAD_DIM
N_EXPERTS = 8
NORM_EPS = 1e-6

LANES = 128
SUBLANES = 8
KV_CHUNK = 512
TQ = 256
VT_ROWS = HEAD_V + 16
LOG2E = math.log2(math.e)
MOE_TILE = 1024
MOE_TOKEN_TILE = 512
SSM_CHUNK = 16
CHUNK_COLS = SSM_CHUNK * SSM_GROUP
GROUP_BLOCK = LANES // SSM_GROUP
STEP_PAIRS = SSM_CHUNK // 2
VMEM_LIMIT = 56 * 1024 * 1024


def _cparams(sem):
    return pltpu.CompilerParams(dimension_semantics=sem, vmem_limit_bytes=VMEM_LIMIT)


def _rms(x, g):
    return x * lax.rsqrt(jnp.mean(x * x, axis=-1, keepdims=True) + NORM_EPS) * g


def _inproj_kernel(x_ref, g_ref, w_ref, u_ref, q_ref, k_ref, v_ref, us_ref):
    h = _rms(x_ref[...], g_ref[...]).astype(BF16)
    proj = lambda n: jnp.dot(h, w_ref[:, n * 512:(n + 1) * 512], preferred_element_type=F32)
    z = proj(0)
    n_chunks = us_ref.shape[1] // SSM_CHUNK
    for j in range(SSM_WIDTH // LANES):
        us_ref[j] = z[:, j * LANES:(j + 1) * LANES]
        for s in range(SSM_CHUNK):
            u_ref[s, :, j * LANES:(j + 1) * LANES] = us_ref[j, pl.ds(s, n_chunks, stride=SSM_CHUNK), :].astype(BF16)
    q_ref[...] = (proj(1) * (HEAD_DIM ** -0.5 * LOG2E)).astype(BF16)
    k_ref[...] = proj(2).astype(BF16)
    v_ref[...] = proj(3).astype(BF16)


def _inproj(x2, g, w_bf, tm):
    n = x2.shape[0]
    out = jax.ShapeDtypeStruct((n, 512), BF16)
    row = lambda c: pl.BlockSpec((tm, c), lambda i: (i, 0))
    return pl.pallas_call(
        _inproj_kernel,
        out_shape=(jax.ShapeDtypeStruct((SSM_CHUNK, n // SSM_CHUNK, SSM_WIDTH), BF16), out, out, out),
        grid=(n // tm,),
        in_specs=[
            row(D_MODEL),
            pl.BlockSpec((1, D_MODEL), lambda i: (0, 0)),
            pl.BlockSpec((D_MODEL, 2048), lambda i: (0, 0)),
        ],
        out_specs=(pl.BlockSpec((SSM_CHUNK, tm // SSM_CHUNK, SSM_WIDTH), lambda i: (0, i, 0)),
                   row(512), row(512), row(512)),
        scratch_shapes=[pltpu.VMEM((SSM_WIDTH // LANES, tm, LANES), F32)],
        compiler_params=_cparams(("parallel",)),
        name="inproj",
    )(x2, g, w_bf)


def _ssm_matrices(lam_re, lam_im, log_step, b_re, b_im, c_re, c_im, d_skip):
    t = SSM_CHUNK
    tau = jnp.arange(t + 1, dtype=F32)
    ks, ws, es, decs = [], [], [], []
    for d in range(2):
        lam = lax.complex(lam_re[d].astype(F32), lam_im[d].astype(F32))
        step = jnp.exp(log_step[d].astype(F32))[:, None]
        ls = lam * step
        lam_bar = jnp.exp(ls)
        pw = jnp.exp(ls[:, None, :] * tau[None, :, None])
        b_bar = ((lam_bar - 1.0) / lam)[:, :, None] * lax.complex(b_re[d].astype(F32), b_im[d].astype(F32))
        c = lax.complex(c_re[d].astype(F32), c_im[d].astype(F32))
        ks.append(jnp.real(jnp.einsum('gop,gtp,gpi->gtoi', c, pw[:, :t], b_bar)))
        if d == 0:
            wpow = pw[:, :t][:, ::-1]
            epow = pw[:, 1:t + 1]
        else:
            wpow = pw[:, :t]
            epow = pw[:, 1:t + 1][:, ::-1]
        ws.append(jnp.einsum('gsp,gpi->gsip', wpow, b_bar).reshape(SSM_GROUPS, CHUNK_COLS, SSM_STATE))
        es.append(jnp.einsum('gop,gtp->gpto', c, epow).reshape(SSM_GROUPS, SSM_STATE, CHUNK_COLS))
        decs.append(pw[:, t])
    kf, kb = ks
    s_idx = jnp.arange(t)[:, None]
    t_idx = jnp.arange(t)[None, :]
    lag_f = t_idx - s_idx
    lag_b = s_idx - t_idx
    mf = jnp.where((lag_f >= 0)[None, :, :, None, None], kf[:, jnp.clip(lag_f, 0, t - 1)], 0.0)
    mb = jnp.where((lag_b >= 0)[None, :, :, None, None], kb[:, jnp.clip(lag_b, 0, t - 1)], 0.0)
    dmat = (jnp.eye(t, dtype=F32)[None, :, :, None, None]
            * (jnp.eye(SSM_GROUP, dtype=F32)[None, None, None] * d_skip.astype(F32).reshape(SSM_GROUPS, 1, 1, 1, SSM_GROUP)))
    m = (mf + mb + dmat).transpose(0, 1, 4, 2, 3).reshape(SSM_GROUPS, CHUNK_COLS, CHUNK_COLS)
    w = jnp.concatenate([jnp.real(ws[0]), jnp.real(ws[1]), jnp.imag(ws[0]), jnp.imag(ws[1])], axis=-1)
    v = jnp.concatenate([jnp.real(es[0]), jnp.real(es[1]), -jnp.imag(es[0]), -jnp.imag(es[1])], axis=1)
    dec_re = jnp.concatenate([jnp.real(decs[0]), jnp.real(decs[1])], axis=-1)[:, None, :]
    dec_im = jnp.concatenate([jnp.imag(decs[0]), jnp.imag(decs[1])], axis=-1)[:, None, :]
    return m.astype(BF16), w.astype(BF16), v.astype(BF16), dec_re, dec_im


def _ssm_kernel(u_ref, m_ref, w_ref, v_ref, ar_ref, ai_ref, y_ref, s_ref, hf_ref, hb_ref, *, n_chunks, rows):
    u = u_ref[0]
    s_ref[...] = jnp.dot(u, w_ref[0], preferred_element_type=F32)
    ar = jnp.broadcast_to(ar_ref[0], (rows, LANES))
    ai = jnp.broadcast_to(ai_ref[0], (rows, LANES))

    def body(c, carry):
        fr, fi, br, bi = carry
        rf = pl.multiple_of(c * rows, rows)
        rb = pl.multiple_of((n_chunks - 1 - c) * rows, rows)
        hf_ref[pl.ds(rf, rows), :LANES] = fr
        hf_ref[pl.ds(rf, rows), LANES:] = fi
        hb_ref[pl.ds(rb, rows), :LANES] = br
        hb_ref[pl.ds(rb, rows), LANES:] = bi
        sf = s_ref[pl.ds(rf, rows), :]
        sb = s_ref[pl.ds(rb, rows), :]
        nfr = ar * fr - ai * fi + sf[:, :LANES]
        nfi = ai * fr + ar * fi + sf[:, LANES:]
        nbr = ar * br - ai * bi + sb[:, :LANES]
        nbi = ai * br + ar * bi + sb[:, LANES:]
        return nfr, nfi, nbr, nbi

    z = jnp.zeros((rows, LANES), F32)
    lax.fori_loop(0, n_chunks, body, (z, z, z, z), unroll=4)
    lane = lax.broadcasted_iota(jnp.int32, (1, 2 * LANES), 1)
    is_fwd = (lane % LANES) < SSM_STATE
    hcat = jnp.where(is_fwd, hf_ref[...], hb_ref[...]).astype(BF16)
    y = jnp.dot(u, m_ref[0], preferred_element_type=F32)
    y = y + jnp.dot(hcat, v_ref[0], preferred_element_type=F32)
    y_ref[0] = jax.nn.gelu(y).astype(BF16)


def _ssm(uc, m, w, v, dec_re, dec_im, n_chunks, rows):
    g, r, _ = uc.shape
    mat = pl.BlockSpec((1, CHUNK_COLS, CHUNK_COLS), lambda i: (i, 0, 0))
    dec = pl.BlockSpec((1, 1, LANES), lambda i: (i, 0, 0))
    return pl.pallas_call(
        functools.partial(_ssm_kernel, n_chunks=n_chunks, rows=rows),
        out_shape=jax.ShapeDtypeStruct((g, r, CHUNK_COLS), BF16),
        grid=(g,),
        in_specs=[pl.BlockSpec((1, r, CHUNK_COLS), lambda i: (i, 0, 0)), mat, mat, mat, dec, dec],
        out_specs=pl.BlockSpec((1, r, CHUNK_COLS), lambda i: (i, 0, 0)),
        scratch_shapes=[pltpu.VMEM((r, CHUNK_COLS), F32)] * 3,
        compiler_params=_cparams(("parallel",)),
        name="ssm",
    )(uc, m, w, v, dec_re, dec_im)


def _regroup_matrices():
    p = np.zeros((STEP_PAIRS, 2, GROUP_BLOCK, SSM_GROUP, GROUP_BLOCK, SSM_CHUNK, SSM_GROUP), np.float32)
    gl = np.arange(GROUP_BLOCK)[:, None]
    h = np.arange(SSM_GROUP)[None, :]
    for sp in range(STEP_PAIRS):
        for half in range(2):
            p[sp, half, gl, h, gl, sp + half * STEP_PAIRS, h] = 1.0
    p = p.reshape(STEP_PAIRS, 2 * LANES, GROUP_BLOCK, CHUNK_COLS)
    q = p.transpose(0, 2, 3, 1)
    return (jnp.asarray(p.reshape(STEP_PAIRS, 2 * LANES, GROUP_BLOCK * CHUNK_COLS), BF16), jnp.asarray(q, BF16))


def _regroup_in_kernel(us_ref, p_ref, o_ref):
    pairs = [jnp.concatenate([us_ref[sp], us_ref[sp + STEP_PAIRS]], axis=1) for sp in range(STEP_PAIRS)]
    for gl in range(GROUP_BLOCK):
        acc = None
        for sp in range(STEP_PAIRS):
            d = jnp.dot(pairs[sp], p_ref[sp, :, gl * CHUNK_COLS:(gl + 1) * CHUNK_COLS], preferred_element_type=F32)
            acc = d if acc is None else acc + d
        o_ref[gl] = acc.astype(BF16)


def _regroup_out_kernel(yc_ref, q_ref, o_ref):
    for tp in range(STEP_PAIRS):
        acc = None
        for gl in range(GROUP_BLOCK):
            d = jnp.dot(yc_ref[gl], q_ref[tp, gl], preferred_element_type=F32)
            acc = d if acc is None else acc + d
        o_ref[tp] = acc[:, :LANES].astype(BF16)
        o_ref[tp + STEP_PAIRS] = acc[:, LANES:].astype(BF16)


def _regroup_specs(c):
    steps = lambda f: pl.BlockSpec((SSM_CHUNK, c, LANES), f)
    groups = lambda f: pl.BlockSpec((GROUP_BLOCK, c, CHUNK_COLS), f)
    return steps(lambda gb, b: (0, b, gb)), groups(lambda gb, b: (gb, 0, b))


def _regroup_in(us, pmat, bsz):
    c = us.shape[1] // bsz
    steps, groups = _regroup_specs(c)
    return pl.pallas_call(
        _regroup_in_kernel,
        out_shape=jax.ShapeDtypeStruct((SSM_GROUPS, c, bsz * CHUNK_COLS), BF16),
        grid=(SSM_GROUPS // GROUP_BLOCK, bsz),
        in_specs=[steps, pl.BlockSpec(pmat.shape, lambda gb, b: (0, 0, 0))],
        out_specs=groups,
        compiler_params=_cparams(("parallel", "parallel")),
        name="regroup_in",
    )(us, pmat)


def _regroup_out(yc, qmat, bsz):
    c = yc.shape[1]
    steps, groups = _regroup_specs(c)
    return pl.pallas_call(
        _regroup_out_kernel,
        out_shape=jax.ShapeDtypeStruct((SSM_CHUNK, c * bsz, SSM_WIDTH), BF16),
        grid=(SSM_GROUPS // GROUP_BLOCK, bsz),
        in_specs=[groups, pl.BlockSpec(qmat.shape, lambda gb, b: (0, 0, 0, 0))],
        out_specs=steps,
        compiler_params=_cparams(("parallel", "parallel")),
        name="regroup_out",
    )(yc, qmat)


def _alibi_tiles(slopes2):
    rel = np.arange(TQ, dtype=np.float32)[None, :] - np.arange(KV_CHUNK, dtype=np.float32)[:, None]
    tiles = [np.stack([-(np.float32(s) * rel), np.float32(s) * rel]) for s in slopes2]
    return jnp.asarray(np.stack(tiles), F32)


def _attn_kernel(par_ref, q_ref, k_ref, v_ref, bt_ref, g_ref, o_ref, vt_ref, qv_ref, acc_ref, sa_ref, sb_ref,
                 *, n_chunks):
    head = pl.program_id(1)
    qi = pl.program_id(2)
    lam = par_ref[0]
    out_scale = par_ref[1]
    slope = par_ref[2 + head]

    @pl.when(qi == 0)
    def _():
        ones_rows = (lax.broadcasted_iota(jnp.int32, (VT_ROWS - HEAD_V, KV_CHUNK), 0) == 0).astype(BF16)
        for c in range(n_chunks):
            vt_ref[c, :HEAD_V] = v_ref[0, c * KV_CHUNK:(c + 1) * KV_CHUNK, :].astype(F32).T.astype(BF16)
            vt_ref[c, HEAD_V:] = ones_rows

    qt = q_ref[0].astype(F32).T.astype(BF16)
    zero = jnp.zeros((HEAD_DIM, TQ), BF16)
    qv_ref[0, :HEAD_DIM] = qt[:HEAD_DIM]
    qv_ref[0, HEAD_DIM:] = zero
    qv_ref[1, :HEAD_DIM] = zero
    qv_ref[1, HEAD_DIM:] = qt[HEAD_DIM:]

    qd = qi // (KV_CHUNK // TQ)
    i0 = qi * TQ

    s_bufs = (sa_ref, sb_ref)

    def scores(t):
        buf = s_bufs[t % 2]
        if t == 0:
            c = qd
            c_off = jnp.float32(0.0)
            bias = -jnp.abs(bt_ref[0, 1] + slope * (i0 - qd * KV_CHUNK).astype(F32))
        else:
            c = (t - 1) + ((t - 1) >= qd).astype(jnp.int32)
            c_off = -slope * jnp.abs(i0 - c * KV_CHUNK).astype(F32)
            bias = bt_ref[0, (c > qd).astype(jnp.int32)]
        k = k_ref[0, pl.ds(pl.multiple_of(c * KV_CHUNK, KV_CHUNK), KV_CHUNK), :]
        smax = []
        for mi in range(2):
            s = jnp.dot(k, qv_ref[mi], preferred_element_type=F32) + bias
            buf[mi] = s
            smax.append(jnp.max(s, axis=0, keepdims=True) + c_off)
        return c, c_off, smax

    def accumulate(t, c, c_off, smax, m):
        buf = s_bufs[t % 2]
        vt = vt_ref[c]
        for mi in range(2):
            m_new = jnp.maximum(m[mi], smax[mi])
            p = jnp.exp2(buf[mi] - (m_new - c_off))
            alpha = jnp.exp2(m[mi] - m_new)
            acc_ref[mi] = alpha * acc_ref[mi] + jnp.dot(vt, p.astype(BF16), preferred_element_type=F32)
            m[mi] = m_new

    acc_ref[...] = jnp.zeros_like(acc_ref)
    m = [jnp.full((1, TQ), -1e30, F32)] * 2
    pending = scores(0)
    for t in range(n_chunks):
        nxt = scores(t + 1) if t + 1 < n_chunks else None
        accumulate(t, *pending, m)
        pending = nxt
    o = (acc_ref[0, :HEAD_V] / acc_ref[0, HEAD_V:HEAD_V + 1]
         - lam * (acc_ref[1, :HEAD_V] / acc_ref[1, HEAD_V:HEAD_V + 1]))
    o = o * lax.rsqrt(jnp.mean(o * o, axis=0, keepdims=True) + NORM_EPS)
    o = o * (g_ref[...] * out_scale)
    o_ref[0] = o.T.astype(BF16)


def _attention(par, q, k, v, bias_tiles, g_col):
    b, l, _ = q.shape
    n_chunks = l // KV_CHUNK
    return pl.pallas_call(
        functools.partial(_attn_kernel, n_chunks=n_chunks),
        out_shape=jax.ShapeDtypeStruct((b, l, ATTN_WIDTH), BF16),
        grid=(b, HEADS, l // TQ),
        in_specs=[
            pl.BlockSpec(memory_space=pltpu.SMEM),
            pl.BlockSpec((1, TQ, HEAD_V), lambda bi, h, i: (bi, i, h)),
            pl.BlockSpec((1, l, HEAD_V), lambda bi, h, i: (bi, 0, h)),
            pl.BlockSpec((1, l, HEAD_V), lambda bi, h, i: (bi, 0, h)),
            pl.BlockSpec((1, 2, KV_CHUNK, TQ), lambda bi, h, i: (h, 0, 0, 0)),
            pl.BlockSpec((HEAD_V, 1), lambda bi, h, i: (0, 0)),
        ],
        out_specs=pl.BlockSpec((1, TQ, HEAD_V), lambda bi, h, i: (bi, i, h)),
        scratch_shapes=[
            pltpu.VMEM((n_chunks, VT_ROWS, KV_CHUNK), BF16),
            pltpu.VMEM((2, HEAD_V, TQ), BF16),
            pltpu.VMEM((2, VT_ROWS, TQ), F32),
            pltpu.VMEM((2, KV_CHUNK, TQ), F32),
            pltpu.VMEM((2, KV_CHUNK, TQ), F32),
        ],
        compiler_params=_cparams(("parallel", "parallel", "arbitrary")),
        name="diff_attn",
    )(par, q, k, v, bias_tiles, g_col)


def _postmix_kernel(x_ref, ys_ref, ya_ref, wglu_ref, bglu_ref, gs_ref, wout_ref, gf_ref, *rest, router):
    if router:
        wr_ref, br_ref, xo_ref, h_ref, gate_ref, ysc_ref = rest
    else:
        xo_ref, h_ref, ysc_ref = rest
    n_chunks = ys_ref.shape[1]
    for j in range(SSM_WIDTH // LANES):
        for s in range(SSM_CHUNK):
            ysc_ref[j, pl.ds(s, n_chunks, stride=SSM_CHUNK), :] = ys_ref[s, :, j * LANES:(j + 1) * LANES].astype(F32)
    y = jnp.concatenate([ysc_ref[j] for j in range(SSM_WIDTH // LANES)], axis=1)
    t = jnp.dot(y.astype(BF16), wglu_ref[...], preferred_element_type=F32) + bglu_ref[...]
    y = y * jax.nn.sigmoid(t)
    y = _rms(y, gs_ref[...]).astype(BF16)
    mix = jnp.dot(y, wout_ref[:SSM_WIDTH, :], preferred_element_type=F32)
    mix = mix + jnp.dot(ya_ref[...], wout_ref[SSM_WIDTH:, :], preferred_element_type=F32)
    x = x_ref[...] + mix
    xo_ref[...] = x
    h = _rms(x, gf_ref[...])
    h_ref[...] = h.astype(h_ref.dtype)
    if router:
        logits = jnp.dot(h, wr_ref[...], precision=lax.Precision.HIGHEST, preferred_element_type=F32) + br_ref[...]
        lane = lax.broadcasted_iota(jnp.int32, logits.shape, 1)
        big = jnp.int32(LANES)
        m1 = jnp.max(logits, axis=-1, keepdims=True)
        i1 = jnp.min(jnp.where(logits == m1, lane, big), axis=-1, keepdims=True)
        rest_l = jnp.where(lane == i1, -jnp.inf, logits)
        m2 = jnp.max(rest_l, axis=-1, keepdims=True)
        i2 = jnp.min(jnp.where(rest_l == m2, lane, big), axis=-1, keepdims=True)
        e2 = jnp.exp(m2 - m1)
        w1 = 1.0 / (1.0 + e2)
        w2 = e2 * w1
        rec = jnp.where(lane == 0, w1, jnp.where(lane == 1, w2, jnp.where(
            lane == 2, i1.astype(F32), jnp.where(lane == 3, i2.astype(F32), 0.0))))
        gate_ref[...] = rec[:, :N_EXPERTS]


def _postmix(x2, ys, ya, wglu, bglu, gs, wout, gf, router_w, tm):
    n = x2.shape[0]
    router = router_w is not None
    row = lambda c: pl.BlockSpec((tm, c), lambda i: (i, 0))
    full = lambda r, c: pl.BlockSpec((r, c), lambda i: (0, 0))
    steps = pl.BlockSpec((SSM_CHUNK, tm // SSM_CHUNK, SSM_WIDTH), lambda i: (0, i, 0))
    in_specs = [row(D_MODEL), steps, row(512), full(512, 512), full(1, 512), full(1, 512),
                full(D_MODEL, D_MODEL), full(1, D_MODEL)]
    args = [x2, ys, ya, wglu, bglu, gs, wout, gf]
    out_shape = [jax.ShapeDtypeStruct((n, D_MODEL), F32), jax.ShapeDtypeStruct((n, D_MODEL), F32 if router else BF16)]
    out_specs = [row(D_MODEL), row(D_MODEL)]
    if router:
        in_specs += [full(D_MODEL, LANES), full(1, LANES)]
        args += list(router_w)
        out_shape.append(jax.ShapeDtypeStruct((n, N_EXPERTS), F32))
        out_specs.append(row(N_EXPERTS))
    return pl.pallas_call(
        functools.partial(_postmix_kernel, router=router),
        out_shape=tuple(out_shape),
        grid=(n // tm,),
        in_specs=in_specs,
        out_specs=tuple(out_specs),
        scratch_shapes=[pltpu.VMEM((SSM_WIDTH // LANES, tm, LANES), F32)],
        compiler_params=_cparams(("parallel",)),
        name="postmix_router" if router else "postmix",
    )(*args)


def _swiglu_partial(h, wg, wu, wd):
    g = jnp.dot(h, wg, preferred_element_type=F32)
    u = jnp.dot(h, wu, preferred_element_type=F32)
    return jnp.dot((jax.nn.silu(g) * u).astype(BF16), wd, preferred_element_type=F32)


def _ffn_kernel(h_ref, x_ref, wg_ref, wu_ref, wd_ref, o_ref):
    @pl.when(pl.program_id(1) == 0)
    def _():
        o_ref[...] = x_ref[...]

    o_ref[...] += _swiglu_partial(h_ref[...], wg_ref[...], wu_ref[...], wd_ref[...])


def _ffn(h, x2, wg, wu, wd, tm, tf):
    n = x2.shape[0]
    ff = wg.shape[1]
    row = lambda c: pl.BlockSpec((tm, c), lambda i, f: (i, 0))
    return pl.pallas_call(
        _ffn_kernel,
        out_shape=jax.ShapeDtypeStruct((n, D_MODEL), F32),
        grid=(n // tm, ff // tf),
        in_specs=[row(D_MODEL), row(D_MODEL),
                  pl.BlockSpec((D_MODEL, tf), lambda i, f: (0, f)),
                  pl.BlockSpec((D_MODEL, tf), lambda i, f: (0, f)),
                  pl.BlockSpec((tf, D_MODEL), lambda i, f: (f, 0))],
        out_specs=row(D_MODEL),
        compiler_params=_cparams(("parallel", "arbitrary")),
        name="ffn_dense",
    )(h, x2, wg, wu, wd)


def _route(rec, tm, tc):
    n = rec.shape[0]
    a = 2 * n
    n_tiles = a // tm + N_EXPERTS
    e = rec[:, 2:4].astype(jnp.int32).reshape(a)
    onehot = (e[:, None] == jnp.arange(N_EXPERTS, dtype=jnp.int32)[None, :]).astype(jnp.int32)
    csum = jnp.cumsum(onehot, axis=0)
    rank = jnp.sum(csum * onehot, axis=1) - 1
    padded = (csum[-1] + tm - 1) // tm * tm
    ends = jnp.cumsum(padded)
    pos = jnp.sum(onehot * (ends - padded)[None, :], axis=1) + rank
    n_used = (ends[-1] // tm).astype(jnp.int32)
    tile = jnp.arange(n_tiles, dtype=jnp.int32)
    texp = jnp.sum((tile[:, None] >= (ends // tm)[None, :]).astype(jnp.int32), axis=1)
    texp = jnp.minimum(texp, N_EXPERTS - 1)
    texp = jnp.where(tile < n_used, texp, texp[jnp.maximum(n_used - 1, 0)])
    pos2d = pos.reshape(n // tc, tc, 2).transpose(0, 2, 1).reshape(n // tc, 2 * tc)
    return pos2d, texp, n_used.reshape(1)


def _fetch_rows_index(pos_hbm, idx_smem, sem_idx):
    cp = pltpu.make_async_copy(pos_hbm.at[pl.program_id(0)], idx_smem, sem_idx)
    cp.start()
    cp.wait()


def _moe_dispatch_kernel(pos_hbm, h_ref, xs_init, xs_hbm, idx_smem, sem_idx, sem):
    del xs_init
    _fetch_rows_index(pos_hbm, idx_smem, sem_idx)
    tc = h_ref.shape[0]

    def issue(r, carry):
        src = h_ref.at[pl.ds(r, 1)]
        pltpu.make_async_copy(src, xs_hbm.at[pl.ds(idx_smem[r], 1)], sem).start()
        pltpu.make_async_copy(src, xs_hbm.at[pl.ds(idx_smem[tc + r], 1)], sem).start()
        return carry

    lax.fori_loop(0, tc, issue, 0, unroll=8)
    for _ in range(2):
        pltpu.make_async_copy(h_ref, xs_hbm.at[pl.ds(0, tc)], sem).wait()


def _moe_dispatch(pos2d, h, rows):
    n = h.shape[0]
    tc = pos2d.shape[1] // 2
    return pl.pallas_call(
        _moe_dispatch_kernel,
        out_shape=jax.ShapeDtypeStruct((rows, D_MODEL), F32),
        grid=(n // tc,),
        in_specs=[pl.BlockSpec(memory_space=pl.ANY), pl.BlockSpec((tc, D_MODEL), lambda i: (i, 0)),
                  pl.BlockSpec(memory_space=pl.ANY)],
        out_specs=pl.BlockSpec(memory_space=pl.ANY),
        scratch_shapes=[pltpu.SMEM((2 * tc,), jnp.int32), pltpu.SemaphoreType.DMA, pltpu.SemaphoreType.DMA],
        input_output_aliases={2: 0},
        compiler_params=_cparams(("arbitrary",)),
        name="moe_dispatch",
    )(pos2d, h, jnp.zeros((rows, D_MODEL), F32))


def _moe_ffn_kernel(texp_ref, nused_ref, xs_ref, wg_ref, wu_ref, wd_ref, o_ref, xb_ref):
    t = pl.program_id(0)
    f = pl.program_id(1)
    last = f == pl.num_programs(1) - 1
    valid = t < nused_ref[0]

    @pl.when(valid)
    def _():
        @pl.when(f == 0)
        def _():
            xb_ref[...] = xs_ref[...].astype(BF16)

        d = _swiglu_partial(xb_ref[...], wg_ref[0], wu_ref[0], wd_ref[0])

        @pl.when(f == 0)
        def _():
            o_ref[...] = d

        @pl.when(f > 0)
        def _():
            o_ref[...] += d

    @pl.when(jnp.logical_and(jnp.logical_not(valid), last))
    def _():
        o_ref[...] = jnp.zeros_like(o_ref)


def _moe_ffn(texp, n_used, xs, wg, wu, wd, tm, tf):
    rows = xs.shape[0]
    ff = wg.shape[-1]
    nf = ff // tf
    fidx = lambda t, f, nu: jnp.where(t < nu[0], f, nf - 1)
    row = pl.BlockSpec((tm, D_MODEL), lambda t, f, te, nu: (t, 0))
    return pl.pallas_call(
        _moe_ffn_kernel,
        out_shape=jax.ShapeDtypeStruct((rows, D_MODEL), F32),
        grid_spec=pltpu.PrefetchScalarGridSpec(
            num_scalar_prefetch=2,
            grid=(rows // tm, nf),
            in_specs=[row,
                      pl.BlockSpec((1, D_MODEL, tf), lambda t, f, te, nu: (te[t], 0, fidx(t, f, nu))),
                      pl.BlockSpec((1, D_MODEL, tf), lambda t, f, te, nu: (te[t], 0, fidx(t, f, nu))),
                      pl.BlockSpec((1, tf, D_MODEL), lambda t, f, te, nu: (te[t], fidx(t, f, nu), 0))],
            out_specs=row,
            scratch_shapes=[pltpu.VMEM((tm, D_MODEL), BF16)],
        ),
        compiler_params=_cparams(("arbitrary", "arbitrary")),
        name="moe_ffn",
    )(texp, n_used, xs, wg, wu, wd)


def _moe_combine_kernel(pos_hbm, ys_hbm, x_ref, rec_ref, gfin_ref, o_ref, idx_smem, buf, sem_idx, sem, *, final):
    _fetch_rows_index(pos_hbm, idx_smem, sem_idx)
    count = buf.shape[0]

    def issue(r, carry):
        pltpu.make_async_copy(ys_hbm.at[pl.ds(idx_smem[r], 1)], buf.at[pl.ds(r, 1)], sem).start()
        return carry

    lax.fori_loop(0, count, issue, 0, unroll=8)
    pltpu.make_async_copy(ys_hbm.at[pl.ds(0, count)], buf, sem).wait()
    tc = x_ref.shape[0]
    rec = rec_ref[...]
    out = x_ref[...] + rec[:, 0:1] * buf[:tc] + rec[:, 1:2] * buf[tc:]
    if final:
        out = _rms(out, gfin_ref[...])
    o_ref[...] = out


def _moe_combine(pos2d, ys, x2, rec, gfin, final):
    n = x2.shape[0]
    tc = pos2d.shape[1] // 2
    row = lambda c: pl.BlockSpec((tc, c), lambda i: (i, 0))
    return pl.pallas_call(
        functools.partial(_moe_combine_kernel, final=final),
        out_shape=jax.ShapeDtypeStruct((n, D_MODEL), F32),
        grid=(n // tc,),
        in_specs=[pl.BlockSpec(memory_space=pl.ANY), pl.BlockSpec(memory_space=pl.ANY), row(D_MODEL),
                  row(N_EXPERTS), pl.BlockSpec((1, D_MODEL), lambda i: (0, 0))],
        out_specs=row(D_MODEL),
        scratch_shapes=[pltpu.SMEM((2 * tc,), jnp.int32), pltpu.VMEM((2 * tc, D_MODEL), F32),
                        pltpu.SemaphoreType.DMA, pltpu.SemaphoreType.DMA],
        compiler_params=_cparams(("arbitrary",)),
        name="moe_combine",
    )(pos2d, ys, x2, rec, gfin)


def _pad_axis(a, axis, size):
    pad = [(0, 0)] * a.ndim
    pad[axis] = (0, size - a.shape[axis])
    return jnp.pad(a, pad)


def kernel(x, g_mix, w_in, ssm_lambda_re, ssm_lambda_im, ssm_log_step, ssm_b_re, ssm_b_im, ssm_c_re, ssm_c_im, ssm_d, w_glu, b_glu, g_ssm_out, lambda_q1, lambda_k1, lambda_q2, lambda_k2, g_subln, w_out, g_ffn, dense_w_gate, dense_w_up, dense_w_down, w_router, b_router, moe_w_gate, moe_w_up, moe_w_down, g_final):
    bsz, seq, _ = x.shape
    n = bsz * seq
    depth = w_in.shape[0]
    n_chunks = seq // SSM_CHUNK
    assert seq % KV_CHUNK == 0 and bsz == SUBLANES and depth % 2 == 0 and (2 * n) % MOE_TILE == 0
    tm = min(512, n)
    tm_ffn = min(1024, n)
    d_ff = dense_w_gate.shape[-1]
    d_ff_pad = -(-d_ff // 256) * 256
    slopes2 = [2.0 ** (-8.0 * (h + 1) / HEADS) * LOG2E for h in range(HEADS)]
    bias_tiles = _alibi_tiles(slopes2)
    pmat, qmat = _regroup_matrices()

    x2 = x.reshape(n, D_MODEL)
    for i in range(depth):
        lambda_init = 0.8 - 0.6 * math.exp(-0.3 * i)
        u, q, k, v = _inproj(x2, g_mix[i][None, :], w_in[i].astype(BF16), tm)

        m, w, vv, dec_re, dec_im = _ssm_matrices(
            ssm_lambda_re[i], ssm_lambda_im[i], ssm_log_step[i], ssm_b_re[i], ssm_b_im[i],
            ssm_c_re[i], ssm_c_im[i], ssm_d[i])
        uc = _regroup_in(u, pmat, bsz).reshape(SSM_GROUPS, n_chunks * bsz, CHUNK_COLS)
        yc = _ssm(uc, m, w, vv, dec_re, dec_im, n_chunks, bsz)
        ys = _regroup_out(yc.reshape(SSM_GROUPS, n_chunks, bsz * CHUNK_COLS), qmat, bsz)

        lam = (jnp.exp(jnp.sum(lambda_q1[i].astype(F32) * lambda_k1[i].astype(F32)))
               - jnp.exp(jnp.sum(lambda_q2[i].astype(F32) * lambda_k2[i].astype(F32))) + lambda_init)
        par = jnp.concatenate([jnp.stack([lam, jnp.asarray(1.0 - lambda_init, F32)]),
                               jnp.asarray(slopes2, F32), jnp.zeros((2,), F32)])
        ya = _attention(par, q.reshape(bsz, seq, ATTN_WIDTH), k.reshape(bsz, seq, ATTN_WIDTH),
                        v.reshape(bsz, seq, ATTN_WIDTH), bias_tiles, g_subln[i].astype(F32)[:, None])
        ya = ya.reshape(n, ATTN_WIDTH)

        j = i // 2
        if i % 2 == 0:
            router_w = None
        else:
            router_w = (_pad_axis(w_router[j].astype(F32), 1, LANES),
                        jnp.concatenate([b_router[j].astype(F32),
                                         jnp.full((LANES - N_EXPERTS,), -jnp.inf, F32)])[None, :])
        outs = _postmix(x2, ys, ya, w_glu[i].astype(BF16), b_glu[i][None, :], g_ssm_out[i][None, :],
                        w_out[i].astype(BF16), g_ffn[i][None, :], router_w, tm)
        if i % 2 == 0:
            x2, h = outs
            wg = _pad_axis(dense_w_gate[j].astype(BF16), 1, d_ff_pad)
            wu = _pad_axis(dense_w_up[j].astype(BF16), 1, d_ff_pad)
            wd = _pad_axis(dense_w_down[j].astype(BF16), 0, d_ff_pad)
            x2 = _ffn(h, x2, wg, wu, wd, tm, d_ff_pad // 2)
        else:
            x2, h, rec = outs
            pos2d, texp, n_used = _route(rec, MOE_TILE, min(MOE_TOKEN_TILE, n))
            xs = _moe_dispatch(pos2d, h, texp.shape[0] * MOE_TILE)
            ys = _moe_ffn(texp, n_used, xs, moe_w_gate[j].astype(BF16), moe_w_up[j].astype(BF16),
                          moe_w_down[j].astype(BF16), MOE_TILE, 512)
            x2 = _moe_combine(pos2d, ys, x2, rec, g_final[None, :], i == depth - 1)
    return x2.reshape(bsz, seq, D_MODEL)
```

```python
import functools
import math

import numpy as np
import jax
import jax.numpy as jnp
from jax import lax
from jax.experimental import pallas as pl
from jax.experimental.pallas import tpu as pltpu

F32 = jnp.float32
BF16 = jnp.bfloat16

D_MODEL = 1024
SSM_WIDTH = 512
SSM_GROUP = 16
SSM_GROUPS = 32
SSM_STATE = 64
ATTN_WIDTH = 512
HEAD_DIM = 64
HEADS = 4
HEAD_V = 2 * HEAD_DIM
N_EXPERTS = 8
NORM_EPS = 1e-6

LANES = 128
SUBLANES = 8
KV_CHUNK = 512
TQ = 512
VT_ROWS = HEAD_V + 16
LOG2E = math.log2(math.e)
MOE_TILE = 1024
MOE_TOKEN_TILE = 512
SSM_CHUNK = 16
CHUNK_COLS = SSM_CHUNK * SSM_GROUP
GROUP_BLOCK = LANES // SSM_GROUP
STEP_PAIRS = SSM_CHUNK // 2
VMEM_LIMIT = 56 * 1024 * 1024


def _cparams(sem):
    return pltpu.CompilerParams(dimension_semantics=sem, vmem_limit_bytes=VMEM_LIMIT)


def _rms(x, g):
    return x * lax.rsqrt(jnp.mean(x * x, axis=-1, keepdims=True) + NORM_EPS) * g


def _inproj_kernel(x_ref, g_ref, w_ref, u_ref, q_ref, k_ref, v_ref, us_ref):
    h = _rms(x_ref[...], g_ref[...]).astype(BF16)
    proj = lambda n: jnp.dot(h, w_ref[:, n * 512:(n + 1) * 512], preferred_element_type=F32)
    z = proj(0)
    n_chunks = us_ref.shape[1] // SSM_CHUNK
    for j in range(SSM_WIDTH // LANES):
        us_ref[j] = z[:, j * LANES:(j + 1) * LANES]
        for s in range(SSM_CHUNK):
            u_ref[s, :, j * LANES:(j + 1) * LANES] = us_ref[j, pl.ds(s, n_chunks, stride=SSM_CHUNK), :].astype(BF16)
    q_ref[...] = (proj(1) * (HEAD_DIM ** -0.5 * LOG2E)).astype(BF16)
    k_ref[...] = proj(2).astype(BF16)
    v_ref[...] = proj(3).astype(BF16)


def _inproj(x2, g, w_bf, tm):
    n = x2.shape[0]
    out = jax.ShapeDtypeStruct((n, 512), BF16)
    row = lambda c: pl.BlockSpec((tm, c), lambda i: (i, 0))
    return pl.pallas_call(
        _inproj_kernel,
        out_shape=(jax.ShapeDtypeStruct((SSM_CHUNK, n // SSM_CHUNK, SSM_WIDTH), BF16), out, out, out),
        grid=(n // tm,),
        in_specs=[
            row(D_MODEL),
            pl.BlockSpec((1, D_MODEL), lambda i: (0, 0)),
            pl.BlockSpec((D_MODEL, 2048), lambda i: (0, 0)),
        ],
        out_specs=(pl.BlockSpec((SSM_CHUNK, tm // SSM_CHUNK, SSM_WIDTH), lambda i: (0, i, 0)),
                   row(512), row(512), row(512)),
        scratch_shapes=[pltpu.VMEM((SSM_WIDTH // LANES, tm, LANES), F32)],
        compiler_params=_cparams(("parallel",)),
        name="inproj",
    )(x2, g, w_bf)


def _ssm_matrices(lam_re, lam_im, log_step, b_re, b_im, c_re, c_im, d_skip):
    t = SSM_CHUNK
    tau = jnp.arange(t + 1, dtype=F32)
    ks, ws, es, decs = [], [], [], []
    for d in range(2):
        lam = lax.complex(lam_re[d].astype(F32), lam_im[d].astype(F32))
        step = jnp.exp(log_step[d].astype(F32))[:, None]
        ls = lam * step
        lam_bar = jnp.exp(ls)
        pw = jnp.exp(ls[:, None, :] * tau[None, :, None])
        b_bar = ((lam_bar - 1.0) / lam)[:, :, None] * lax.complex(b_re[d].astype(F32), b_im[d].astype(F32))
        c = lax.complex(c_re[d].astype(F32), c_im[d].astype(F32))
        ks.append(jnp.real(jnp.einsum('gop,gtp,gpi->gtoi', c, pw[:, :t], b_bar)))
        if d == 0:
            wpow = pw[:, :t][:, ::-1]
            epow = pw[:, 1:t + 1]
        else:
            wpow = pw[:, :t]
            epow = pw[:, 1:t + 1][:, ::-1]
        ws.append(jnp.einsum('gsp,gpi->gsip', wpow, b_bar).reshape(SSM_GROUPS, CHUNK_COLS, SSM_STATE))
        es.append(jnp.einsum('gop,gtp->gpto', c, epow).reshape(SSM_GROUPS, SSM_STATE, CHUNK_COLS))
        decs.append(pw[:, t])
    kf, kb = ks
    s_idx = jnp.arange(t)[:, None]
    t_idx = jnp.arange(t)[None, :]
    lag_f = t_idx - s_idx
    lag_b = s_idx - t_idx
    lags = jnp.arange(t)[None, None, :]
    mf = jnp.einsum('stk,gkoi->gstoi', (lag_f[:, :, None] == lags).astype(F32), kf)
    mb = jnp.einsum('stk,gkoi->gstoi', (lag_b[:, :, None] == lags).astype(F32), kb)
    dmat = (jnp.eye(t, dtype=F32)[None, :, :, None, None]
            * (jnp.eye(SSM_GROUP, dtype=F32)[None, None, None] * d_skip.astype(F32).reshape(SSM_GROUPS, 1, 1, 1, SSM_GROUP)))
    m = (mf + mb + dmat).transpose(0, 1, 4, 2, 3).reshape(SSM_GROUPS, CHUNK_COLS, CHUNK_COLS)
    w = jnp.concatenate([jnp.real(ws[0]), jnp.real(ws[1]), jnp.imag(ws[0]), jnp.imag(ws[1])], axis=-1)
    v = jnp.concatenate([jnp.real(es[0]), jnp.real(es[1]), -jnp.imag(es[0]), -jnp.imag(es[1])], axis=1)
    dec_re = jnp.concatenate([jnp.real(decs[0]), jnp.real(decs[1])], axis=-1)[:, None, :]
    dec_im = jnp.concatenate([jnp.imag(decs[0]), jnp.imag(decs[1])], axis=-1)[:, None, :]
    return m.astype(BF16), w.astype(BF16), v.astype(BF16), dec_re, dec_im


def _ssm_kernel(u_ref, m_ref, w_ref, v_ref, ar_ref, ai_ref, y_ref, s_ref, hf_ref, hb_ref, *, n_chunks, rows):
    u = u_ref[0]
    s = jnp.dot(u, w_ref[0], preferred_element_type=F32)
    s_ref[0] = s[:, :LANES]
    s_ref[1] = s[:, LANES:]
    ar = jnp.broadcast_to(ar_ref[0], (rows, LANES))
    ai = jnp.broadcast_to(ai_ref[0], (rows, LANES))

    def body(c, carry):
        fr, fi, br, bi = carry
        at_f = pl.ds(c, rows, stride=n_chunks)
        at_b = pl.ds(n_chunks - 1 - c, rows, stride=n_chunks)
        hf_ref[0, at_f, :] = fr
        hf_ref[1, at_f, :] = fi
        hb_ref[0, at_b, :] = br
        hb_ref[1, at_b, :] = bi
        nfr = ar * fr - ai * fi + s_ref[0, at_f, :]
        nfi = ai * fr + ar * fi + s_ref[1, at_f, :]
        nbr = ar * br - ai * bi + s_ref[0, at_b, :]
        nbi = ai * br + ar * bi + s_ref[1, at_b, :]
        return nfr, nfi, nbr, nbi

    z = jnp.zeros((rows, LANES), F32)
    lax.fori_loop(0, n_chunks, body, (z, z, z, z), unroll=4)
    is_fwd = lax.broadcasted_iota(jnp.int32, (1, LANES), 1) < SSM_STATE
    hcat = jnp.concatenate([jnp.where(is_fwd, hf_ref[0], hb_ref[0]),
                            jnp.where(is_fwd, hf_ref[1], hb_ref[1])], axis=1).astype(BF16)
    y = jnp.dot(u, m_ref[0], preferred_element_type=F32)
    y = y + jnp.dot(hcat, v_ref[0], preferred_element_type=F32)
    y_ref[0] = jax.nn.gelu(y).astype(BF16)


def _ssm(uc, m, w, v, dec_re, dec_im, n_chunks, rows):
    g, r, _ = uc.shape
    mat = pl.BlockSpec((1, CHUNK_COLS, CHUNK_COLS), lambda i: (i, 0, 0))
    dec = pl.BlockSpec((1, 1, LANES), lambda i: (i, 0, 0))
    return pl.pallas_call(
        functools.partial(_ssm_kernel, n_chunks=n_chunks, rows=rows),
        out_shape=jax.ShapeDtypeStruct((g, r, CHUNK_COLS), BF16),
        grid=(g,),
        in_specs=[pl.BlockSpec((1, r, CHUNK_COLS), lambda i: (i, 0, 0)), mat, mat, mat, dec, dec],
        out_specs=pl.BlockSpec((1, r, CHUNK_COLS), lambda i: (i, 0, 0)),
        scratch_shapes=[pltpu.VMEM((2, r, LANES), F32)] * 3,
        compiler_params=_cparams(("parallel",)),
        name="ssm",
    )(uc, m, w, v, dec_re, dec_im)


def _regroup_matrices():
    p = np.zeros((STEP_PAIRS, 2, GROUP_BLOCK, SSM_GROUP, GROUP_BLOCK, SSM_CHUNK, SSM_GROUP), np.float32)
    gl = np.arange(GROUP_BLOCK)[:, None]
    h = np.arange(SSM_GROUP)[None, :]
    for sp in range(STEP_PAIRS):
        for half in range(2):
            p[sp, half, gl, h, gl, sp + half * STEP_PAIRS, h] = 1.0
    p = p.reshape(STEP_PAIRS, 2 * LANES, GROUP_BLOCK, CHUNK_COLS)
    q = p.transpose(0, 2, 3, 1)
    return (jnp.asarray(p.reshape(STEP_PAIRS, 2 * LANES, GROUP_BLOCK * CHUNK_COLS), BF16), jnp.asarray(q, BF16))


def _regroup_in_kernel(us_ref, p_ref, o_ref):
    pairs = [jnp.concatenate([us_ref[sp], us_ref[sp + STEP_PAIRS]], axis=1) for sp in range(STEP_PAIRS)]
    for gl in range(GROUP_BLOCK):
        acc = None
        for sp in range(STEP_PAIRS):
            d = jnp.dot(pairs[sp], p_ref[sp, :, gl * CHUNK_COLS:(gl + 1) * CHUNK_COLS], preferred_element_type=F32)
            acc = d if acc is None else acc + d
        o_ref[gl] = acc.astype(BF16)


def _regroup_out_kernel(yc_ref, q_ref, o_ref):
    for tp in range(STEP_PAIRS):
        acc = None
        for gl in range(GROUP_BLOCK):
            d = jnp.dot(yc_ref[gl], q_ref[tp, gl], preferred_element_type=F32)
            acc = d if acc is None else acc + d
        o_ref[tp] = acc[:, :LANES].astype(BF16)
        o_ref[tp + STEP_PAIRS] = acc[:, LANES:].astype(BF16)


def _regroup_specs(c):
    steps = lambda f: pl.BlockSpec((SSM_CHUNK, c, LANES), f)
    groups = lambda f: pl.BlockSpec((GROUP_BLOCK, c, CHUNK_COLS), f)
    return steps(lambda gb, b: (0, b, gb)), groups(lambda gb, b: (gb, b, 0))


def _regroup_in(us, pmat, bsz):
    c = us.shape[1] // bsz
    steps, groups = _regroup_specs(c)
    return pl.pallas_call(
        _regroup_in_kernel,
        out_shape=jax.ShapeDtypeStruct((SSM_GROUPS, bsz * c, CHUNK_COLS), BF16),
        grid=(SSM_GROUPS // GROUP_BLOCK, bsz),
        in_specs=[steps, pl.BlockSpec(pmat.shape, lambda gb, b: (0, 0, 0))],
        out_specs=groups,
        compiler_params=_cparams(("parallel", "parallel")),
        name="regroup_in",
    )(us, pmat)


def _regroup_out(yc, qmat, bsz):
    c = yc.shape[1] // bsz
    steps, groups = _regroup_specs(c)
    return pl.pallas_call(
        _regroup_out_kernel,
        out_shape=jax.ShapeDtypeStruct((SSM_CHUNK, c * bsz, SSM_WIDTH), BF16),
        grid=(SSM_GROUPS // GROUP_BLOCK, bsz),
        in_specs=[groups, pl.BlockSpec(qmat.shape, lambda gb, b: (0, 0, 0, 0))],
        out_specs=steps,
        compiler_params=_cparams(("parallel", "parallel")),
        name="regroup_out",
    )(yc, qmat)


def _alibi_tiles(slopes2):
    rel = np.arange(TQ, dtype=np.float32)[None, :] - np.arange(KV_CHUNK, dtype=np.float32)[:, None]
    tiles = [np.stack([-(np.float32(s) * rel), np.float32(s) * rel]) for s in slopes2]
    return jnp.asarray(np.stack(tiles), F32)


def _attn_kernel(par_ref, q_ref, k_ref, v_ref, bt_ref, g_ref, o_ref, vt_ref, qv_ref, acc_ref, sa_ref, sb_ref,
                 *, n_chunks):
    head = pl.program_id(1)
    qi = pl.program_id(2)
    lam = par_ref[0]
    out_scale = par_ref[1]
    slope = par_ref[2 + head]

    @pl.when(qi == 0)
    def _():
        ones_rows = (lax.broadcasted_iota(jnp.int32, (VT_ROWS - HEAD_V, KV_CHUNK), 0) == 0).astype(BF16)
        for c in range(n_chunks):
            vt_ref[c, :HEAD_V] = v_ref[0, c * KV_CHUNK:(c + 1) * KV_CHUNK, :].astype(F32).T.astype(BF16)
            vt_ref[c, HEAD_V:] = ones_rows

    qt = q_ref[0].astype(F32).T.astype(BF16)
    zero = jnp.zeros((HEAD_DIM, TQ), BF16)
    qv_ref[0, :HEAD_DIM] = qt[:HEAD_DIM]
    qv_ref[0, HEAD_DIM:] = zero
    qv_ref[1, :HEAD_DIM] = zero
    qv_ref[1, HEAD_DIM:] = qt[HEAD_DIM:]

    qd = qi // (KV_CHUNK // TQ)
    i0 = qi * TQ

    s_bufs = (sa_ref, sb_ref)

    def scores(t):
        buf = s_bufs[t % 2]
        if t == 0:
            c = qd
            c_off = jnp.float32(0.0)
            bias = -jnp.abs(bt_ref[0, 1] + slope * (i0 - qd * KV_CHUNK).astype(F32))
        else:
            c = (t - 1) + ((t - 1) >= qd).astype(jnp.int32)
            c_off = -slope * jnp.abs(i0 - c * KV_CHUNK).astype(F32)
            bias = bt_ref[0, (c > qd).astype(jnp.int32)]
        k = k_ref[0, pl.ds(pl.multiple_of(c * KV_CHUNK, KV_CHUNK), KV_CHUNK), :]
        smax = []
        for mi in range(2):
            s = jnp.dot(k, qv_ref[mi], preferred_element_type=F32) + bias
            buf[mi] = s
            smax.append(jnp.max(s, axis=0, keepdims=True) + c_off)
        return c, c_off, smax

    def accumulate(t, c, c_off, smax, m):
        buf = s_bufs[t % 2]
        vt = vt_ref[c]
        for mi in range(2):
            m_new = jnp.maximum(m[mi], smax[mi])
            p = jnp.exp2(buf[mi] - (m_new - c_off))
            alpha = jnp.exp2(m[mi] - m_new)
            acc_ref[mi] = alpha * acc_ref[mi] + jnp.dot(vt, p.astype(BF16), preferred_element_type=F32)
            m[mi] = m_new

    acc_ref[...] = jnp.zeros_like(acc_ref)
    m = [jnp.full((1, TQ), -1e30, F32)] * 2
    pending = scores(0)
    for t in range(n_chunks):
        nxt = scores(t + 1) if t + 1 < n_chunks else None
        accumulate(t, *pending, m)
        pending = nxt
    o = (acc_ref[0, :HEAD_V] / acc_ref[0, HEAD_V:HEAD_V + 1]
         - lam * (acc_ref[1, :HEAD_V] / acc_ref[1, HEAD_V:HEAD_V + 1]))
    o = o * lax.rsqrt(jnp.mean(o * o, axis=0, keepdims=True) + NORM_EPS)
    o = o * (g_ref[...] * out_scale)
    o_ref[0] = o.T.astype(BF16)


def _attention(par, q, k, v, bias_tiles, g_col):
    b, l, _ = q.shape
    n_chunks = l // KV_CHUNK
    return pl.pallas_call(
        functools.partial(_attn_kernel, n_chunks=n_chunks),
        out_shape=jax.ShapeDtypeStruct((b, l, ATTN_WIDTH), BF16),
        grid=(b, HEADS, l // TQ),
        in_specs=[
            pl.BlockSpec(memory_space=pltpu.SMEM),
            pl.BlockSpec((1, TQ, HEAD_V), lambda bi, h, i: (bi, i, h)),
            pl.BlockSpec((1, l, HEAD_V), lambda bi, h, i: (bi, 0, h)),
            pl.BlockSpec((1, l, HEAD_V), lambda bi, h, i: (bi, 0, h)),
            pl.BlockSpec((1, 2, KV_CHUNK, TQ), lambda bi, h, i: (h, 0, 0, 0)),
            pl.BlockSpec((HEAD_V, 1), lambda bi, h, i: (0, 0)),
        ],
        out_specs=pl.BlockSpec((1, TQ, HEAD_V), lambda bi, h, i: (bi, i, h)),
        scratch_shapes=[
            pltpu.VMEM((n_chunks, VT_ROWS, KV_CHUNK), BF16),
            pltpu.VMEM((2, HEAD_V, TQ), BF16),
            pltpu.VMEM((2, VT_ROWS, TQ), F32),
            pltpu.VMEM((2, KV_CHUNK, TQ), F32),
            pltpu.VMEM((2, KV_CHUNK, TQ), F32),
        ],
        compiler_params=_cparams(("parallel", "parallel", "arbitrary")),
        name="diff_attn",
    )(par, q, k, v, bias_tiles, g_col)


def _postmix_kernel(x_ref, ys_ref, ya_ref, wglu_ref, bglu_ref, gs_ref, wout_ref, gf_ref, *rest, router):
    if router:
        wr_ref, br_ref, xo_ref, h_ref, gate_ref, ysc_ref = rest
    else:
        xo_ref, h_ref, ysc_ref = rest
    n_chunks = ys_ref.shape[1]
    for j in range(SSM_WIDTH // LANES):
        for s in range(SSM_CHUNK):
            ysc_ref[j, pl.ds(s, n_chunks, stride=SSM_CHUNK), :] = ys_ref[s, :, j * LANES:(j + 1) * LANES].astype(F32)
    y = jnp.concatenate([ysc_ref[j] for j in range(SSM_WIDTH // LANES)], axis=1)
    t = jnp.dot(y.astype(BF16), wglu_ref[...], preferred_element_type=F32) + bglu_ref[...]
    y = y * jax.nn.sigmoid(t)
    y = _rms(y, gs_ref[...]).astype(BF16)
    mix = jnp.dot(y, wout_ref[:SSM_WIDTH, :], preferred_element_type=F32)
    mix = mix + jnp.dot(ya_ref[...], wout_ref[SSM_WIDTH:, :], preferred_element_type=F32)
    x = x_ref[...] + mix
    xo_ref[...] = x
    h = _rms(x, gf_ref[...])
    h_ref[...] = h.astype(h_ref.dtype)
    if router:
        lane = lax.broadcasted_iota(jnp.int32, (h.shape[0], LANES), 1)
        logits = jnp.broadcast_to(br_ref[...], lane.shape)
        for e in range(N_EXPERTS):
            le = jnp.sum(h * wr_ref[e:e + 1, :], axis=-1, keepdims=True)
            logits = jnp.where(lane == e, logits + le, logits)
        big = jnp.int32(LANES)
        m1 = jnp.max(logits, axis=-1, keepdims=True)
        i1 = jnp.min(jnp.where(logits == m1, lane, big), axis=-1, keepdims=True)
        rest_l = jnp.where(lane == i1, -jnp.inf, logits)
        m2 = jnp.max(rest_l, axis=-1, keepdims=True)
        i2 = jnp.min(jnp.where(rest_l == m2, lane, big), axis=-1, keepdims=True)
        e2 = jnp.exp(m2 - m1)
        w1 = 1.0 / (1.0 + e2)
        w2 = e2 * w1
        rec = jnp.where(lane == 0, w1, jnp.where(lane == 1, w2, jnp.where(
            lane == 2, i1.astype(F32), jnp.where(lane == 3, i2.astype(F32), 0.0))))
        gate_ref[...] = rec[:, :N_EXPERTS]


def _postmix(x2, ys, ya, wglu, bglu, gs, wout, gf, router_w, tm):
    n = x2.shape[0]
    router = router_w is not None
    row = lambda c: pl.BlockSpec((tm, c), lambda i: (i, 0))
    full = lambda r, c: pl.BlockSpec((r, c), lambda i: (0, 0))
    steps = pl.BlockSpec((SSM_CHUNK, tm // SSM_CHUNK, SSM_WIDTH), lambda i: (0, i, 0))
    in_specs = [row(D_MODEL), steps, row(512), full(512, 512), full(1, 512), full(1, 512),
                full(D_MODEL, D_MODEL), full(1, D_MODEL)]
    args = [x2, ys, ya, wglu, bglu, gs, wout, gf]
    out_shape = [jax.ShapeDtypeStruct((n, D_MODEL), F32), jax.ShapeDtypeStruct((n, D_MODEL), F32 if router else BF16)]
    out_specs = [row(D_MODEL), row(D_MODEL)]
    if router:
        in_specs += [full(N_EXPERTS, D_MODEL), full(1, LANES)]
        args += list(router_w)
        out_shape.append(jax.ShapeDtypeStruct((n, N_EXPERTS), F32))
        out_specs.append(row(N_EXPERTS))
    return pl.pallas_call(
        functools.partial(_postmix_kernel, router=router),
        out_shape=tuple(out_shape),
        grid=(n // tm,),
        in_specs=in_specs,
        out_specs=tuple(out_specs),
        scratch_shapes=[pltpu.VMEM((SSM_WIDTH // LANES, tm, LANES), F32)],
        compiler_params=_cparams(("parallel",)),
        name="postmix_router" if router else "postmix",
    )(*args)


def _swiglu_partial(h, wg, wu, wd):
    g = jnp.dot(h, wg, preferred_element_type=F32)
    u = jnp.dot(h, wu, preferred_element_type=F32)
    return jnp.dot((jax.nn.silu(g) * u).astype(BF16), wd, preferred_element_type=F32)


def _ffn_kernel(h_ref, x_ref, wg_ref, wu_ref, wd_ref, o_ref):
    @pl.when(pl.program_id(1) == 0)
    def _():
        o_ref[...] = x_ref[...]

    o_ref[...] += _swiglu_partial(h_ref[...], wg_ref[...], wu_ref[...], wd_ref[...])


def _ffn(h, x2, wg, wu, wd, tm, tf):
    n = x2.shape[0]
    ff = wg.shape[1]
    row = lambda c: pl.BlockSpec((tm, c), lambda i, f: (i, 0))
    return pl.pallas_call(
        _ffn_kernel,
        out_shape=jax.ShapeDtypeStruct((n, D_MODEL), F32),
        grid=(n // tm, ff // tf),
        in_specs=[row(D_MODEL), row(D_MODEL),
                  pl.BlockSpec((D_MODEL, tf), lambda i, f: (0, f)),
                  pl.BlockSpec((D_MODEL, tf), lambda i, f: (0, f)),
                  pl.BlockSpec((tf, D_MODEL), lambda i, f: (f, 0))],
        out_specs=row(D_MODEL),
        compiler_params=_cparams(("parallel", "arbitrary")),
        name="ffn_dense",
    )(h, x2, wg, wu, wd)


def _route(rec, tm, tc):
    n = rec.shape[0]
    a = 2 * n
    n_tiles = a // tm + N_EXPERTS
    e = rec[:, 2:4].astype(jnp.int32).reshape(a)
    onehot = (e[:, None] == jnp.arange(N_EXPERTS, dtype=jnp.int32)[None, :]).astype(jnp.int32)
    csum = jnp.cumsum(onehot, axis=0)
    rank = jnp.sum(csum * onehot, axis=1) - 1
    padded = (csum[-1] + tm - 1) // tm * tm
    ends = jnp.cumsum(padded)
    pos = jnp.sum(onehot * (ends - padded)[None, :], axis=1) + rank
    n_used = (ends[-1] // tm).astype(jnp.int32)
    tile = jnp.arange(n_tiles, dtype=jnp.int32)
    texp = jnp.sum((tile[:, None] >= (ends // tm)[None, :]).astype(jnp.int32), axis=1)
    texp = jnp.minimum(texp, N_EXPERTS - 1)
    texp = jnp.where(tile < n_used, texp, texp[jnp.maximum(n_used - 1, 0)])
    pos2d = pos.reshape(n // tc, tc, 2).transpose(0, 2, 1).reshape(n // tc, 2 * tc)
    return pos2d, texp, n_used.reshape(1)


def _fetch_rows_index(pos_hbm, idx_smem, sem_idx):
    cp = pltpu.make_async_copy(pos_hbm.at[pl.program_id(0)], idx_smem, sem_idx)
    cp.start()
    cp.wait()


def _moe_dispatch_kernel(pos_hbm, h_ref, xs_init, xs_hbm, idx_smem, sem_idx, sem):
    del xs_init
    _fetch_rows_index(pos_hbm, idx_smem, sem_idx)
    tc = h_ref.shape[0]

    def issue(r, carry):
        src = h_ref.at[pl.ds(r, 1)]
        pltpu.make_async_copy(src, xs_hbm.at[pl.ds(idx_smem[r], 1)], sem).start()
        pltpu.make_async_copy(src, xs_hbm.at[pl.ds(idx_smem[tc + r], 1)], sem).start()
        return carry

    lax.fori_loop(0, tc, issue, 0, unroll=8)
    for _ in range(2):
        pltpu.make_async_copy(h_ref, xs_hbm.at[pl.ds(0, tc)], sem).wait()


def _moe_dispatch(pos2d, h, rows):
    n = h.shape[0]
    tc = pos2d.shape[1] // 2
    return pl.pallas_call(
        _moe_dispatch_kernel,
        out_shape=jax.ShapeDtypeStruct((rows, D_MODEL), F32),
        grid=(n // tc,),
        in_specs=[pl.BlockSpec(memory_space=pl.ANY), pl.BlockSpec((tc, D_MODEL), lambda i: (i, 0)),
                  pl.BlockSpec(memory_space=pl.ANY)],
        out_specs=pl.BlockSpec(memory_space=pl.ANY),
        scratch_shapes=[pltpu.SMEM((2 * tc,), jnp.int32), pltpu.SemaphoreType.DMA, pltpu.SemaphoreType.DMA],
        input_output_aliases={2: 0},
        compiler_params=_cparams(("arbitrary",)),
        name="moe_dispatch",
    )(pos2d, h, jnp.zeros((rows, D_MODEL), F32))


def _moe_ffn_kernel(texp_ref, nused_ref, xs_ref, wg_ref, wu_ref, wd_ref, o_ref, xb_ref):
    t = pl.program_id(0)
    f = pl.program_id(1)
    last = f == pl.num_programs(1) - 1
    valid = t < nused_ref[0]

    @pl.when(valid)
    def _():
        @pl.when(f == 0)
        def _():
            xb_ref[...] = xs_ref[...].astype(BF16)

        d = _swiglu_partial(xb_ref[...], wg_ref[0], wu_ref[0], wd_ref[0])

        @pl.when(f == 0)
        def _():
            o_ref[...] = d

        @pl.when(f > 0)
        def _():
            o_ref[...] += d

    @pl.when(jnp.logical_and(jnp.logical_not(valid), last))
    def _():
        o_ref[...] = jnp.zeros_like(o_ref)


def _moe_ffn(texp, n_used, xs, wg, wu, wd, tm, tf):
    rows = xs.shape[0]
    ff = wg.shape[-1]
    nf = ff // tf
    fidx = lambda t, f, nu: jnp.where(t < nu[0], f, nf - 1)
    row = pl.BlockSpec((tm, D_MODEL), lambda t, f, te, nu: (t, 0))
    return pl.pallas_call(
        _moe_ffn_kernel,
        out_shape=jax.ShapeDtypeStruct((rows, D_MODEL), F32),
        grid_spec=pltpu.PrefetchScalarGridSpec(
            num_scalar_prefetch=2,
            grid=(rows // tm, nf),
            in_specs=[row,
                      pl.BlockSpec((1, D_MODEL, tf), lambda t, f, te, nu: (te[t], 0, fidx(t, f, nu))),
                      pl.BlockSpec((1, D_MODEL, tf), lambda t, f, te, nu: (te[t], 0, fidx(t, f, nu))),
                      pl.BlockSpec((1, tf, D_MODEL), lambda t, f, te, nu: (te[t], fidx(t, f, nu), 0))],
            out_specs=row,
            scratch_shapes=[pltpu.VMEM((tm, D_MODEL), BF16)],
        ),
        compiler_params=_cparams(("arbitrary", "arbitrary")),
        name="moe_ffn",
    )(texp, n_used, xs, wg, wu, wd)


def _moe_combine_kernel(pos_hbm, ys_hbm, x_ref, rec_ref, gfin_ref, o_ref, idx_smem, buf, sem_idx, sem, *, final):
    _fetch_rows_index(pos_hbm, idx_smem, sem_idx)
    count = buf.shape[0]

    def issue(r, carry):
        pltpu.make_async_copy(ys_hbm.at[pl.ds(idx_smem[r], 1)], buf.at[pl.ds(r, 1)], sem).start()
        return carry

    lax.fori_loop(0, count, issue, 0, unroll=8)
    pltpu.make_async_copy(ys_hbm.at[pl.ds(0, count)], buf, sem).wait()
    tc = x_ref.shape[0]
    rec = rec_ref[...]
    out = x_ref[...] + rec[:, 0:1] * buf[:tc] + rec[:, 1:2] * buf[tc:]
    if final:
        out = _rms(out, gfin_ref[...])
    o_ref[...] = out


def _moe_combine(pos2d, ys, x2, rec, gfin, final):
    n = x2.shape[0]
    tc = pos2d.shape[1] // 2
    row = lambda c: pl.BlockSpec((tc, c), lambda i: (i, 0))
    return pl.pallas_call(
        functools.partial(_moe_combine_kernel, final=final),
        out_shape=jax.ShapeDtypeStruct((n, D_MODEL), F32),
        grid=(n // tc,),
        in_specs=[pl.BlockSpec(memory_space=pl.ANY), pl.BlockSpec(memory_space=pl.ANY), row(D_MODEL),
                  row(N_EXPERTS), pl.BlockSpec((1, D_MODEL), lambda i: (0, 0))],
        out_specs=row(D_MODEL),
        scratch_shapes=[pltpu.SMEM((2 * tc,), jnp.int32), pltpu.VMEM((2 * tc, D_MODEL), F32),
                        pltpu.SemaphoreType.DMA, pltpu.SemaphoreType.DMA],
        compiler_params=_cparams(("arbitrary",)),
        name="moe_combine",
    )(pos2d, ys, x2, rec, gfin)


def _pad_axis(a, axis, size):
    pad = [(0, 0)] * a.ndim
    pad[axis] = (0, size - a.shape[axis])
    return jnp.pad(a, pad)


def kernel(x, g_mix, w_in, ssm_lambda_re, ssm_lambda_im, ssm_log_step, ssm_b_re, ssm_b_im, ssm_c_re, ssm_c_im, ssm_d, w_glu, b_glu, g_ssm_out, lambda_q1, lambda_k1, lambda_q2, lambda_k2, g_subln, w_out, g_ffn, dense_w_gate, dense_w_up, dense_w_down, w_router, b_router, moe_w_gate, moe_w_up, moe_w_down, g_final):
    bsz, seq, _ = x.shape
    n = bsz * seq
    depth = w_in.shape[0]
    n_chunks = seq // SSM_CHUNK
    assert seq % KV_CHUNK == 0 and bsz == SUBLANES and depth % 2 == 0 and (2 * n) % MOE_TILE == 0
    tm = min(512, n)
    tm_ffn = min(1024, n)
    d_ff = dense_w_gate.shape[-1]
    d_ff_pad = -(-d_ff // 256) * 256
    slopes2 = [2.0 ** (-8.0 * (h + 1) / HEADS) * LOG2E for h in range(HEADS)]
    bias_tiles = _alibi_tiles(slopes2)
    pmat, qmat = _regroup_matrices()

    x2 = x.reshape(n, D_MODEL)
    for i in range(depth):
        lambda_init = 0.8 - 0.6 * math.exp(-0.3 * i)
        u, q, k, v = _inproj(x2, g_mix[i][None, :], w_in[i].astype(BF16), tm)

        m, w, vv, dec_re, dec_im = _ssm_matrices(
            ssm_lambda_re[i], ssm_lambda_im[i], ssm_log_step[i], ssm_b_re[i], ssm_b_im[i],
            ssm_c_re[i], ssm_c_im[i], ssm_d[i])
        uc = _regroup_in(u, pmat, bsz)
        yc = _ssm(uc, m, w, vv, dec_re, dec_im, n_chunks, bsz)
        ys = _regroup_out(yc, qmat, bsz)

        lam = (jnp.exp(jnp.sum(lambda_q1[i].astype(F32) * lambda_k1[i].astype(F32)))
               - jnp.exp(jnp.sum(lambda_q2[i].astype(F32) * lambda_k2[i].astype(F32))) + lambda_init)
        par = jnp.concatenate([jnp.stack([lam, jnp.asarray(1.0 - lambda_init, F32)]),
                               jnp.asarray(slopes2, F32), jnp.zeros((2,), F32)])
        ya = _attention(par, q.reshape(bsz, seq, ATTN_WIDTH), k.reshape(bsz, seq, ATTN_WIDTH),
                        v.reshape(bsz, seq, ATTN_WIDTH), bias_tiles, g_subln[i].astype(F32)[:, None])
        ya = ya.reshape(n, ATTN_WIDTH)

        j = i // 2
        if i % 2 == 0:
            router_w = None
        else:
            router_w = (w_router[j].astype(F32).T,
                        jnp.concatenate([b_router[j].astype(F32),
                                         jnp.full((LANES - N_EXPERTS,), -jnp.inf, F32)])[None, :])
        outs = _postmix(x2, ys, ya, w_glu[i].astype(BF16), b_glu[i][None, :], g_ssm_out[i][None, :],
                        w_out[i].astype(BF16), g_ffn[i][None, :], router_w, tm)
        if i % 2 == 0:
            x2, h = outs
            wg = _pad_axis(dense_w_gate[j].astype(BF16), 1, d_ff_pad)
            wu = _pad_axis(dense_w_up[j].astype(BF16), 1, d_ff_pad)
            wd = _pad_axis(dense_w_down[j].astype(BF16), 0, d_ff_pad)
            x2 = _ffn(h, x2, wg, wu, wd, tm, d_ff_pad // 2)
        else:
            x2, h, rec = outs
            pos2d, texp, n_used = _route(rec, MOE_TILE, min(MOE_TOKEN_TILE, n))
            xs = _moe_dispatch(pos2d, h, texp.shape[0] * MOE_TILE)
            ys = _moe_ffn(texp, n_used, xs, moe_w_gate[j].astype(BF16), moe_w_up[j].astype(BF16),
                          moe_w_down[j].astype(BF16), MOE_TILE, 512)
            x2 = _moe_combine(pos2d, ys, x2, rec, g_final[None, :], i == depth - 1)
    return x2.reshape(bsz, seq, D_MODEL)
```

```python
import functools
import math

import numpy as np
import jax
import jax.numpy as jnp
from jax import lax
from jax.experimental import pallas as pl
from jax.experimental.pallas import tpu as pltpu

F32 = jnp.float32
BF16 = jnp.bfloat16

D_MODEL = 1024
SSM_WIDTH = 512
SSM_GROUP = 16
SSM_GROUPS = 32
SSM_STATE = 64
ATTN_WIDTH = 512
HEAD_DIM = 64
HEADS = 4
HEAD_V = 2 * HEAD_DIM
N_EXPERTS = 8
NORM_EPS = 1e-6

LANES = 128
SUBLANES = 8
KV_CHUNK = 512
TQ = 512
VT_ROWS = HEAD_V + 16
LOG2E = math.log2(math.e)
MOE_TILE = 1024
MOE_TOKEN_TILE = 512
SSM_CHUNK = 16
CHUNK_COLS = SSM_CHUNK * SSM_GROUP
GROUP_BLOCK = LANES // SSM_GROUP
STEP_PAIRS = SSM_CHUNK // 2
VMEM_LIMIT = 56 * 1024 * 1024


def _cparams(sem):
    return pltpu.CompilerParams(dimension_semantics=sem, vmem_limit_bytes=VMEM_LIMIT)


def _rms(x, g):
    return x * lax.rsqrt(jnp.mean(x * x, axis=-1, keepdims=True) + NORM_EPS) * g


def _inproj_kernel(x_ref, g_ref, w_ref, u_ref, q_ref, k_ref, v_ref, us_ref):
    h = _rms(x_ref[...], g_ref[...]).astype(BF16)
    proj = lambda n: jnp.dot(h, w_ref[:, n * 512:(n + 1) * 512], preferred_element_type=F32)
    z = proj(0)
    n_chunks = us_ref.shape[1] // SSM_CHUNK
    for j in range(SSM_WIDTH // LANES):
        us_ref[j] = z[:, j * LANES:(j + 1) * LANES]
        for s in range(SSM_CHUNK):
            u_ref[s, :, j * LANES:(j + 1) * LANES] = us_ref[j, pl.ds(s, n_chunks, stride=SSM_CHUNK), :].astype(BF16)
    q_ref[...] = (proj(1) * (HEAD_DIM ** -0.5 * LOG2E)).astype(BF16)
    k_ref[...] = proj(2).astype(BF16)
    v_ref[...] = proj(3).astype(BF16)


def _inproj(x2, g, w_bf, tm):
    n = x2.shape[0]
    out = jax.ShapeDtypeStruct((n, 512), BF16)
    row = lambda c: pl.BlockSpec((tm, c), lambda i: (i, 0))
    return pl.pallas_call(
        _inproj_kernel,
        out_shape=(jax.ShapeDtypeStruct((SSM_CHUNK, n // SSM_CHUNK, SSM_WIDTH), BF16), out, out, out),
        grid=(n // tm,),
        in_specs=[
            row(D_MODEL),
            pl.BlockSpec((1, D_MODEL), lambda i: (0, 0)),
            pl.BlockSpec((D_MODEL, 2048), lambda i: (0, 0)),
        ],
        out_specs=(pl.BlockSpec((SSM_CHUNK, tm // SSM_CHUNK, SSM_WIDTH), lambda i: (0, i, 0)),
                   row(512), row(512), row(512)),
        scratch_shapes=[pltpu.VMEM((SSM_WIDTH // LANES, tm, LANES), F32)],
        compiler_params=_cparams(("parallel",)),
        name="inproj",
    )(x2, g, w_bf)


def _ssm_matrices(lam_re, lam_im, log_step, b_re, b_im, c_re, c_im, d_skip):
    t = SSM_CHUNK
    tau = jnp.arange(t + 1, dtype=F32)
    ks, ws, es, decs = [], [], [], []
    for d in range(2):
        lam = lax.complex(lam_re[d].astype(F32), lam_im[d].astype(F32))
        step = jnp.exp(log_step[d].astype(F32))[:, None]
        ls = lam * step
        lam_bar = jnp.exp(ls)
        pw = jnp.exp(ls[:, None, :] * tau[None, :, None])
        b_bar = ((lam_bar - 1.0) / lam)[:, :, None] * lax.complex(b_re[d].astype(F32), b_im[d].astype(F32))
        c = lax.complex(c_re[d].astype(F32), c_im[d].astype(F32))
        ks.append(jnp.real(jnp.einsum('gop,gtp,gpi->gtoi', c, pw[:, :t], b_bar)))
        if d == 0:
            wpow = pw[:, :t][:, ::-1]
            epow = pw[:, 1:t + 1]
        else:
            wpow = pw[:, :t]
            epow = pw[:, 1:t + 1][:, ::-1]
        ws.append(jnp.einsum('gsp,gpi->gsip', wpow, b_bar).reshape(SSM_GROUPS, CHUNK_COLS, SSM_STATE))
        es.append(jnp.einsum('gop,gtp->gpto', c, epow).reshape(SSM_GROUPS, SSM_STATE, CHUNK_COLS))
        decs.append(pw[:, t])
    kf, kb = ks
    s_idx = jnp.arange(t)[:, None]
    t_idx = jnp.arange(t)[None, :]
    lag_f = t_idx - s_idx
    lag_b = s_idx - t_idx
    lags = jnp.arange(t)[None, None, :]
    mf = jnp.einsum('stk,gkoi->gstoi', (lag_f[:, :, None] == lags).astype(F32), kf)
    mb = jnp.einsum('stk,gkoi->gstoi', (lag_b[:, :, None] == lags).astype(F32), kb)
    dmat = (jnp.eye(t, dtype=F32)[None, :, :, None, None]
            * (jnp.eye(SSM_GROUP, dtype=F32)[None, None, None] * d_skip.astype(F32).reshape(SSM_GROUPS, 1, 1, 1, SSM_GROUP)))
    m = (mf + mb + dmat).transpose(0, 1, 4, 2, 3).reshape(SSM_GROUPS, CHUNK_COLS, CHUNK_COLS)
    w = jnp.concatenate([jnp.real(ws[0]), jnp.real(ws[1]), jnp.imag(ws[0]), jnp.imag(ws[1])], axis=-1)
    v = jnp.concatenate([jnp.real(es[0]), jnp.real(es[1]), -jnp.imag(es[0]), -jnp.imag(es[1])], axis=1)
    dec_re = jnp.concatenate([jnp.real(decs[0]), jnp.real(decs[1])], axis=-1)[:, None, :]
    dec_im = jnp.concatenate([jnp.imag(decs[0]), jnp.imag(decs[1])], axis=-1)[:, None, :]
    return m.astype(BF16), w.astype(BF16), v.astype(BF16), dec_re, dec_im


def _ssm_kernel(u_ref, m_ref, w_ref, v_ref, ar_ref, ai_ref, y_ref, s_ref, hf_ref, hb_ref, *, n_chunks, rows):
    u = u_ref[0]
    s = jnp.dot(u, w_ref[0], preferred_element_type=F32)
    pitch = n_chunks + 1
    for b in range(rows):
        s_ref[0, b * pitch:b * pitch + n_chunks, :] = s[b * n_chunks:(b + 1) * n_chunks, :LANES]
        s_ref[1, b * pitch:b * pitch + n_chunks, :] = s[b * n_chunks:(b + 1) * n_chunks, LANES:]
    ar = jnp.broadcast_to(ar_ref[0], (rows, LANES))
    ai = jnp.broadcast_to(ai_ref[0], (rows, LANES))

    def body(c, carry):
        fr, fi, br, bi = carry
        at_f = pl.ds(c, rows, stride=pitch)
        at_b = pl.ds(n_chunks - 1 - c, rows, stride=pitch)
        hf_ref[0, at_f, :] = fr
        hf_ref[1, at_f, :] = fi
        hb_ref[0, at_b, :] = br
        hb_ref[1, at_b, :] = bi
        nfr = ar * fr - ai * fi + s_ref[0, at_f, :]
        nfi = ai * fr + ar * fi + s_ref[1, at_f, :]
        nbr = ar * br - ai * bi + s_ref[0, at_b, :]
        nbi = ai * br + ar * bi + s_ref[1, at_b, :]
        return nfr, nfi, nbr, nbi

    z = jnp.zeros((rows, LANES), F32)
    lax.fori_loop(0, n_chunks, body, (z, z, z, z), unroll=4)
    is_fwd = lax.broadcasted_iota(jnp.int32, (1, LANES), 1) < SSM_STATE
    seq_rows = lambda ref, part: jnp.concatenate(
        [ref[part, b * pitch:b * pitch + n_chunks, :] for b in range(rows)], axis=0)
    hcat = jnp.concatenate([jnp.where(is_fwd, seq_rows(hf_ref, 0), seq_rows(hb_ref, 0)),
                            jnp.where(is_fwd, seq_rows(hf_ref, 1), seq_rows(hb_ref, 1))], axis=1).astype(BF16)
    y = jnp.dot(u, m_ref[0], preferred_element_type=F32)
    y = y + jnp.dot(hcat, v_ref[0], preferred_element_type=F32)
    y_ref[0] = jax.nn.gelu(y).astype(BF16)


def _ssm(uc, m, w, v, dec_re, dec_im, n_chunks, rows):
    g, r, _ = uc.shape
    mat = pl.BlockSpec((1, CHUNK_COLS, CHUNK_COLS), lambda i: (i, 0, 0))
    dec = pl.BlockSpec((1, 1, LANES), lambda i: (i, 0, 0))
    return pl.pallas_call(
        functools.partial(_ssm_kernel, n_chunks=n_chunks, rows=rows),
        out_shape=jax.ShapeDtypeStruct((g, r, CHUNK_COLS), BF16),
        grid=(g,),
        in_specs=[pl.BlockSpec((1, r, CHUNK_COLS), lambda i: (i, 0, 0)), mat, mat, mat, dec, dec],
        out_specs=pl.BlockSpec((1, r, CHUNK_COLS), lambda i: (i, 0, 0)),
        scratch_shapes=[pltpu.VMEM((2, rows * (n_chunks + 1), LANES), F32)] * 3,
        compiler_params=_cparams(("parallel",)),
        name="ssm",
    )(uc, m, w, v, dec_re, dec_im)


def _regroup_matrices():
    p = np.zeros((STEP_PAIRS, 2, GROUP_BLOCK, SSM_GROUP, GROUP_BLOCK, SSM_CHUNK, SSM_GROUP), np.float32)
    gl = np.arange(GROUP_BLOCK)[:, None]
    h = np.arange(SSM_GROUP)[None, :]
    for sp in range(STEP_PAIRS):
        for half in range(2):
            p[sp, half, gl, h, gl, sp + half * STEP_PAIRS, h] = 1.0
    p = p.reshape(STEP_PAIRS, 2 * LANES, GROUP_BLOCK, CHUNK_COLS)
    q = p.transpose(0, 2, 3, 1)
    return (jnp.asarray(p.reshape(STEP_PAIRS, 2 * LANES, GROUP_BLOCK * CHUNK_COLS), BF16), jnp.asarray(q, BF16))


def _regroup_in_kernel(us_ref, p_ref, o_ref):
    pairs = [jnp.concatenate([us_ref[sp], us_ref[sp + STEP_PAIRS]], axis=1) for sp in range(STEP_PAIRS)]
    for gl in range(GROUP_BLOCK):
        acc = None
        for sp in range(STEP_PAIRS):
            d = jnp.dot(pairs[sp], p_ref[sp, :, gl * CHUNK_COLS:(gl + 1) * CHUNK_COLS], preferred_element_type=F32)
            acc = d if acc is None else acc + d
        o_ref[gl] = acc.astype(BF16)


def _regroup_out_kernel(yc_ref, q_ref, o_ref):
    for tp in range(STEP_PAIRS):
        acc = None
        for gl in range(GROUP_BLOCK):
            d = jnp.dot(yc_ref[gl], q_ref[tp, gl], preferred_element_type=F32)
            acc = d if acc is None else acc + d
        o_ref[tp] = acc[:, :LANES].astype(BF16)
        o_ref[tp + STEP_PAIRS] = acc[:, LANES:].astype(BF16)


def _regroup_specs(c):
    steps = lambda f: pl.BlockSpec((SSM_CHUNK, c, LANES), f)
    groups = lambda f: pl.BlockSpec((GROUP_BLOCK, c, CHUNK_COLS), f)
    return steps(lambda gb, b: (0, b, gb)), groups(lambda gb, b: (gb, b, 0))


def _regroup_in(us, pmat, bsz):
    c = us.shape[1] // bsz
    steps, groups = _regroup_specs(c)
    return pl.pallas_call(
        _regroup_in_kernel,
        out_shape=jax.ShapeDtypeStruct((SSM_GROUPS, bsz * c, CHUNK_COLS), BF16),
        grid=(SSM_GROUPS // GROUP_BLOCK, bsz),
        in_specs=[steps, pl.BlockSpec(pmat.shape, lambda gb, b: (0, 0, 0))],
        out_specs=groups,
        compiler_params=_cparams(("parallel", "parallel")),
        name="regroup_in",
    )(us, pmat)


def _regroup_out(yc, qmat, bsz):
    c = yc.shape[1] // bsz
    steps, groups = _regroup_specs(c)
    return pl.pallas_call(
        _regroup_out_kernel,
        out_shape=jax.ShapeDtypeStruct((SSM_CHUNK, c * bsz, SSM_WIDTH), BF16),
        grid=(SSM_GROUPS // GROUP_BLOCK, bsz),
        in_specs=[groups, pl.BlockSpec(qmat.shape, lambda gb, b: (0, 0, 0, 0))],
        out_specs=steps,
        compiler_params=_cparams(("parallel", "parallel")),
        name="regroup_out",
    )(yc, qmat)


def _alibi_tiles(slopes2):
    rel = np.arange(TQ, dtype=np.float32)[None, :] - np.arange(KV_CHUNK, dtype=np.float32)[:, None]
    tiles = [np.stack([-(np.float32(s) * rel), np.float32(s) * rel]) for s in slopes2]
    return jnp.asarray(np.stack(tiles), F32)


def _attn_kernel(par_ref, q_ref, k_ref, v_ref, bt_ref, g_ref, o_ref, vt_ref, qv_ref, acc_ref, sa_ref, sb_ref,
                 *, n_chunks):
    head = pl.program_id(1)
    qi = pl.program_id(2)
    lam = par_ref[0]
    out_scale = par_ref[1]
    slope = par_ref[2 + head]

    @pl.when(qi == 0)
    def _():
        ones_rows = (lax.broadcasted_iota(jnp.int32, (VT_ROWS - HEAD_V, KV_CHUNK), 0) == 0).astype(BF16)
        for c in range(n_chunks):
            vt_ref[c, :HEAD_V] = v_ref[0, c * KV_CHUNK:(c + 1) * KV_CHUNK, :].astype(F32).T.astype(BF16)
            vt_ref[c, HEAD_V:] = ones_rows

    qt = q_ref[0].astype(F32).T.astype(BF16)
    zero = jnp.zeros((HEAD_DIM, TQ), BF16)
    qv_ref[0, :HEAD_DIM] = qt[:HEAD_DIM]
    qv_ref[0, HEAD_DIM:] = zero
    qv_ref[1, :HEAD_DIM] = zero
    qv_ref[1, HEAD_DIM:] = qt[HEAD_DIM:]

    qd = qi // (KV_CHUNK // TQ)
    i0 = qi * TQ

    s_bufs = (sa_ref, sb_ref)

    def scores(t):
        buf = s_bufs[t % 2]
        if t == 0:
            c = qd
            c_off = jnp.float32(0.0)
            bias = -jnp.abs(bt_ref[0, 1] + slope * (i0 - qd * KV_CHUNK).astype(F32))
        else:
            c = (t - 1) + ((t - 1) >= qd).astype(jnp.int32)
            c_off = -slope * jnp.abs(i0 - c * KV_CHUNK).astype(F32)
            bias = bt_ref[0, (c > qd).astype(jnp.int32)]
        k = k_ref[0, pl.ds(pl.multiple_of(c * KV_CHUNK, KV_CHUNK), KV_CHUNK), :]
        smax = []
        for mi in range(2):
            s = jnp.dot(k, qv_ref[mi], preferred_element_type=F32) + bias
            buf[mi] = s
            smax.append(jnp.max(s, axis=0, keepdims=True) + c_off)
        return c, c_off, smax

    def accumulate(t, c, c_off, smax, m):
        buf = s_bufs[t % 2]
        vt = vt_ref[c]
        for mi in range(2):
            m_new = jnp.maximum(m[mi], smax[mi])
            p = jnp.exp2(buf[mi] - (m_new - c_off))
            alpha = jnp.exp2(m[mi] - m_new)
            acc_ref[mi] = alpha * acc_ref[mi] + jnp.dot(vt, p.astype(BF16), preferred_element_type=F32)
            m[mi] = m_new

    acc_ref[...] = jnp.zeros_like(acc_ref)
    m = [jnp.full((1, TQ), -1e30, F32)] * 2
    pending = scores(0)
    for t in range(n_chunks):
        nxt = scores(t + 1) if t + 1 < n_chunks else None
        accumulate(t, *pending, m)
        pending = nxt
    o = (acc_ref[0, :HEAD_V] / acc_ref[0, HEAD_V:HEAD_V + 1]
         - lam * (acc_ref[1, :HEAD_V] / acc_ref[1, HEAD_V:HEAD_V + 1]))
    o = o * lax.rsqrt(jnp.mean(o * o, axis=0, keepdims=True) + NORM_EPS)
    o = o * (g_ref[...] * out_scale)
    o_ref[0] = o.T.astype(BF16)


def _attention(par, q, k, v, bias_tiles, g_col):
    b, l, _ = q.shape
    n_chunks = l // KV_CHUNK
    return pl.pallas_call(
        functools.partial(_attn_kernel, n_chunks=n_chunks),
        out_shape=jax.ShapeDtypeStruct((b, l, ATTN_WIDTH), BF16),
        grid=(b, HEADS, l // TQ),
        in_specs=[
            pl.BlockSpec(memory_space=pltpu.SMEM),
            pl.BlockSpec((1, TQ, HEAD_V), lambda bi, h, i: (bi, i, h)),
            pl.BlockSpec((1, l, HEAD_V), lambda bi, h, i: (bi, 0, h)),
            pl.BlockSpec((1, l, HEAD_V), lambda bi, h, i: (bi, 0, h)),
            pl.BlockSpec((1, 2, KV_CHUNK, TQ), lambda bi, h, i: (h, 0, 0, 0)),
            pl.BlockSpec((HEAD_V, 1), lambda bi, h, i: (0, 0)),
        ],
        out_specs=pl.BlockSpec((1, TQ, HEAD_V), lambda bi, h, i: (bi, i, h)),
        scratch_shapes=[
            pltpu.VMEM((n_chunks, VT_ROWS, KV_CHUNK), BF16),
            pltpu.VMEM((2, HEAD_V, TQ), BF16),
            pltpu.VMEM((2, VT_ROWS, TQ), F32),
            pltpu.VMEM((2, KV_CHUNK, TQ), F32),
            pltpu.VMEM((2, KV_CHUNK, TQ), F32),
        ],
        compiler_params=_cparams(("parallel", "parallel", "arbitrary")),
        name="diff_attn",
    )(par, q, k, v, bias_tiles, g_col)


def _postmix_kernel(x_ref, ys_ref, ya_ref, wglu_ref, bglu_ref, gs_ref, wout_ref, gf_ref, *rest, router):
    if router:
        wr_ref, br_ref, xo_ref, h_ref, gate_ref, ysc_ref = rest
    else:
        xo_ref, h_ref, ysc_ref = rest
    n_chunks = ys_ref.shape[1]
    for j in range(SSM_WIDTH // LANES):
        for s in range(SSM_CHUNK):
            ysc_ref[j, pl.ds(s, n_chunks, stride=SSM_CHUNK), :] = ys_ref[s, :, j * LANES:(j + 1) * LANES].astype(F32)
    y = jnp.concatenate([ysc_ref[j] for j in range(SSM_WIDTH // LANES)], axis=1)
    t = jnp.dot(y.astype(BF16), wglu_ref[...], preferred_element_type=F32) + bglu_ref[...]
    y = y * jax.nn.sigmoid(t)
    y = _rms(y, gs_ref[...]).astype(BF16)
    mix = jnp.dot(y, wout_ref[:SSM_WIDTH, :], preferred_element_type=F32)
    mix = mix + jnp.dot(ya_ref[...], wout_ref[SSM_WIDTH:, :], preferred_element_type=F32)
    x = x_ref[...] + mix
    xo_ref[...] = x
    h = _rms(x, gf_ref[...])
    h_ref[...] = h.astype(h_ref.dtype)
    if router:
        lane = lax.broadcasted_iota(jnp.int32, (h.shape[0], LANES), 1)
        logits = jnp.broadcast_to(br_ref[...], lane.shape)
        for e in range(N_EXPERTS):
            le = jnp.sum(h * wr_ref[e:e + 1, :], axis=-1, keepdims=True)
            logits = jnp.where(lane == e, logits + le, logits)
        big = jnp.int32(LANES)
        m1 = jnp.max(logits, axis=-1, keepdims=True)
        i1 = jnp.min(jnp.where(logits == m1, lane, big), axis=-1, keepdims=True)
        rest_l = jnp.where(lane == i1, -jnp.inf, logits)
        m2 = jnp.max(rest_l, axis=-1, keepdims=True)
        i2 = jnp.min(jnp.where(rest_l == m2, lane, big), axis=-1, keepdims=True)
        e2 = jnp.exp(m2 - m1)
        w1 = 1.0 / (1.0 + e2)
        w2 = e2 * w1
        rec = jnp.where(lane == 0, w1, jnp.where(lane == 1, w2, jnp.where(
            lane == 2, i1.astype(F32), jnp.where(lane == 3, i2.astype(F32), 0.0))))
        gate_ref[...] = rec[:, :N_EXPERTS]


def _postmix(x2, ys, ya, wglu, bglu, gs, wout, gf, router_w, tm):
    n = x2.shape[0]
    router = router_w is not None
    row = lambda c: pl.BlockSpec((tm, c), lambda i: (i, 0))
    full = lambda r, c: pl.BlockSpec((r, c), lambda i: (0, 0))
    steps = pl.BlockSpec((SSM_CHUNK, tm // SSM_CHUNK, SSM_WIDTH), lambda i: (0, i, 0))
    in_specs = [row(D_MODEL), steps, row(512), full(512, 512), full(1, 512), full(1, 512),
                full(D_MODEL, D_MODEL), full(1, D_MODEL)]
    args = [x2, ys, ya, wglu, bglu, gs, wout, gf]
    out_shape = [jax.ShapeDtypeStruct((n, D_MODEL), F32), jax.ShapeDtypeStruct((n, D_MODEL), F32 if router else BF16)]
    out_specs = [row(D_MODEL), row(D_MODEL)]
    if router:
        in_specs += [full(N_EXPERTS, D_MODEL), full(1, LANES)]
        args += list(router_w)
        out_shape.append(jax.ShapeDtypeStruct((n, N_EXPERTS), F32))
        out_specs.append(row(N_EXPERTS))
    return pl.pallas_call(
        functools.partial(_postmix_kernel, router=router),
        out_shape=tuple(out_shape),
        grid=(n // tm,),
        in_specs=in_specs,
        out_specs=tuple(out_specs),
        scratch_shapes=[pltpu.VMEM((SSM_WIDTH // LANES, tm, LANES), F32)],
        compiler_params=_cparams(("parallel",)),
        name="postmix_router" if router else "postmix",
    )(*args)


def _swiglu_partial(h, wg, wu, wd):
    g = jnp.dot(h, wg, preferred_element_type=F32)
    u = jnp.dot(h, wu, preferred_element_type=F32)
    return jnp.dot((jax.nn.silu(g) * u).astype(BF16), wd, preferred_element_type=F32)


def _ffn_kernel(h_ref, x_ref, wg_ref, wu_ref, wd_ref, o_ref):
    @pl.when(pl.program_id(1) == 0)
    def _():
        o_ref[...] = x_ref[...]

    o_ref[...] += _swiglu_partial(h_ref[...], wg_ref[...], wu_ref[...], wd_ref[...])


def _ffn(h, x2, wg, wu, wd, tm, tf):
    n = x2.shape[0]
    ff = wg.shape[1]
    row = lambda c: pl.BlockSpec((tm, c), lambda i, f: (i, 0))
    return pl.pallas_call(
        _ffn_kernel,
        out_shape=jax.ShapeDtypeStruct((n, D_MODEL), F32),
        grid=(n // tm, ff // tf),
        in_specs=[row(D_MODEL), row(D_MODEL),
                  pl.BlockSpec((D_MODEL, tf), lambda i, f: (0, f)),
                  pl.BlockSpec((D_MODEL, tf), lambda i, f: (0, f)),
                  pl.BlockSpec((tf, D_MODEL), lambda i, f: (f, 0))],
        out_specs=row(D_MODEL),
        compiler_params=_cparams(("parallel", "arbitrary")),
        name="ffn_dense",
    )(h, x2, wg, wu, wd)


def _route(rec, tm, tc):
    n = rec.shape[0]
    a = 2 * n
    n_tiles = a // tm + N_EXPERTS
    e = rec[:, 2:4].astype(jnp.int32).reshape(a)
    onehot = (e[:, None] == jnp.arange(N_EXPERTS, dtype=jnp.int32)[None, :]).astype(jnp.int32)
    csum = jnp.cumsum(onehot, axis=0)
    rank = jnp.sum(csum * onehot, axis=1) - 1
    padded = (csum[-1] + tm - 1) // tm * tm
    ends = jnp.cumsum(padded)
    pos = jnp.sum(onehot * (ends - padded)[None, :], axis=1) + rank
    n_used = (ends[-1] // tm).astype(jnp.int32)
    tile = jnp.arange(n_tiles, dtype=jnp.int32)
    texp = jnp.sum((tile[:, None] >= (ends // tm)[None, :]).astype(jnp.int32), axis=1)
    texp = jnp.minimum(texp, N_EXPERTS - 1)
    texp = jnp.where(tile < n_used, texp, texp[jnp.maximum(n_used - 1, 0)])
    pos2d = pos.reshape(n // tc, tc, 2).transpose(0, 2, 1).reshape(n // tc, 2 * tc)
    return pos2d, texp, n_used.reshape(1)


def _fetch_rows_index(pos_hbm, idx_smem, sem_idx):
    cp = pltpu.make_async_copy(pos_hbm.at[pl.program_id(0)], idx_smem, sem_idx)
    cp.start()
    cp.wait()


def _moe_dispatch_kernel(pos_hbm, h_ref, xs_init, xs_hbm, idx_smem, sem_idx, sem):
    del xs_init
    _fetch_rows_index(pos_hbm, idx_smem, sem_idx)
    tc = h_ref.shape[0]

    def issue(r, carry):
        src = h_ref.at[pl.ds(r, 1)]
        pltpu.make_async_copy(src, xs_hbm.at[pl.ds(idx_smem[r], 1)], sem).start()
        pltpu.make_async_copy(src, xs_hbm.at[pl.ds(idx_smem[tc + r], 1)], sem).start()
        return carry

    lax.fori_loop(0, tc, issue, 0, unroll=8)
    for _ in range(2):
        pltpu.make_async_copy(h_ref, xs_hbm.at[pl.ds(0, tc)], sem).wait()


def _moe_dispatch(pos2d, h, rows):
    n = h.shape[0]
    tc = pos2d.shape[1] // 2
    return pl.pallas_call(
        _moe_dispatch_kernel,
        out_shape=jax.ShapeDtypeStruct((rows, D_MODEL), F32),
        grid=(n // tc,),
        in_specs=[pl.BlockSpec(memory_space=pl.ANY), pl.BlockSpec((tc, D_MODEL), lambda i: (i, 0)),
                  pl.BlockSpec(memory_space=pl.ANY)],
        out_specs=pl.BlockSpec(memory_space=pl.ANY),
        scratch_shapes=[pltpu.SMEM((2 * tc,), jnp.int32), pltpu.SemaphoreType.DMA, pltpu.SemaphoreType.DMA],
        input_output_aliases={2: 0},
        compiler_params=_cparams(("arbitrary",)),
        name="moe_dispatch",
    )(pos2d, h, jnp.zeros((rows, D_MODEL), F32))


def _moe_ffn_kernel(texp_ref, nused_ref, xs_ref, wg_ref, wu_ref, wd_ref, o_ref, xb_ref):
    t = pl.program_id(0)
    f = pl.program_id(1)
    last = f == pl.num_programs(1) - 1
    valid = t < nused_ref[0]

    @pl.when(valid)
    def _():
        @pl.when(f == 0)
        def _():
            xb_ref[...] = xs_ref[...].astype(BF16)

        d = _swiglu_partial(xb_ref[...], wg_ref[0, 0].astype(BF16), wu_ref[0, 0].astype(BF16),
                            wd_ref[0, 0].astype(BF16))

        @pl.when(f == 0)
        def _():
            o_ref[...] = d

        @pl.when(f > 0)
        def _():
            o_ref[...] += d

    @pl.when(jnp.logical_and(jnp.logical_not(valid), last))
    def _():
        o_ref[...] = jnp.zeros_like(o_ref)


def _moe_ffn(texp, n_used, xs, wg, wu, wd, layer, tm, tf):
    rows = xs.shape[0]
    ff = wg.shape[-1]
    nf = ff // tf
    fidx = lambda t, f, nu: jnp.where(t < nu[0], f, nf - 1)
    row = pl.BlockSpec((tm, D_MODEL), lambda t, f, te, nu: (t, 0))
    return pl.pallas_call(
        _moe_ffn_kernel,
        out_shape=jax.ShapeDtypeStruct((rows, D_MODEL), F32),
        grid_spec=pltpu.PrefetchScalarGridSpec(
            num_scalar_prefetch=2,
            grid=(rows // tm, nf),
            in_specs=[row,
                      pl.BlockSpec((1, 1, D_MODEL, tf), lambda t, f, te, nu: (layer, te[t], 0, fidx(t, f, nu))),
                      pl.BlockSpec((1, 1, D_MODEL, tf), lambda t, f, te, nu: (layer, te[t], 0, fidx(t, f, nu))),
                      pl.BlockSpec((1, 1, tf, D_MODEL), lambda t, f, te, nu: (layer, te[t], fidx(t, f, nu), 0))],
            out_specs=row,
            scratch_shapes=[pltpu.VMEM((tm, D_MODEL), BF16)],
        ),
        compiler_params=_cparams(("arbitrary", "arbitrary")),
        name="moe_ffn",
    )(texp, n_used, xs, wg, wu, wd)


def _moe_combine_kernel(pos_hbm, ys_hbm, x_ref, rec_ref, gfin_ref, o_ref, idx_smem, buf, sem_idx, sem, *, final):
    _fetch_rows_index(pos_hbm, idx_smem, sem_idx)
    count = buf.shape[0]

    def issue(r, carry):
        pltpu.make_async_copy(ys_hbm.at[pl.ds(idx_smem[r], 1)], buf.at[pl.ds(r, 1)], sem).start()
        return carry

    lax.fori_loop(0, count, issue, 0, unroll=8)
    pltpu.make_async_copy(ys_hbm.at[pl.ds(0, count)], buf, sem).wait()
    tc = x_ref.shape[0]
    rec = rec_ref[...]
    out = x_ref[...] + rec[:, 0:1] * buf[:tc] + rec[:, 1:2] * buf[tc:]
    if final:
        out = _rms(out, gfin_ref[...])
    o_ref[...] = out


def _moe_combine(pos2d, ys, x2, rec, gfin, final):
    n = x2.shape[0]
    tc = pos2d.shape[1] // 2
    row = lambda c: pl.BlockSpec((tc, c), lambda i: (i, 0))
    return pl.pallas_call(
        functools.partial(_moe_combine_kernel, final=final),
        out_shape=jax.ShapeDtypeStruct((n, D_MODEL), F32),
        grid=(n // tc,),
        in_specs=[pl.BlockSpec(memory_space=pl.ANY), pl.BlockSpec(memory_space=pl.ANY), row(D_MODEL),
                  row(N_EXPERTS), pl.BlockSpec((1, D_MODEL), lambda i: (0, 0))],
        out_specs=row(D_MODEL),
        scratch_shapes=[pltpu.SMEM((2 * tc,), jnp.int32), pltpu.VMEM((2 * tc, D_MODEL), F32),
                        pltpu.SemaphoreType.DMA, pltpu.SemaphoreType.DMA],
        compiler_params=_cparams(("arbitrary",)),
        name="moe_combine",
    )(pos2d, ys, x2, rec, gfin)


def _pad_axis(a, axis, size):
    pad = [(0, 0)] * a.ndim
    pad[axis] = (0, size - a.shape[axis])
    return jnp.pad(a, pad)


def kernel(x, g_mix, w_in, ssm_lambda_re, ssm_lambda_im, ssm_log_step, ssm_b_re, ssm_b_im, ssm_c_re, ssm_c_im, ssm_d, w_glu, b_glu, g_ssm_out, lambda_q1, lambda_k1, lambda_q2, lambda_k2, g_subln, w_out, g_ffn, dense_w_gate, dense_w_up, dense_w_down, w_router, b_router, moe_w_gate, moe_w_up, moe_w_down, g_final):
    bsz, seq, _ = x.shape
    n = bsz * seq
    depth = w_in.shape[0]
    n_chunks = seq // SSM_CHUNK
    assert seq % KV_CHUNK == 0 and bsz == SUBLANES and depth % 2 == 0 and (2 * n) % MOE_TILE == 0
    tm = min(512, n)
    tm_ffn = min(1024, n)
    d_ff = dense_w_gate.shape[-1]
    d_ff_pad = -(-d_ff // 256) * 256
    slopes2 = [2.0 ** (-8.0 * (h + 1) / HEADS) * LOG2E for h in range(HEADS)]
    bias_tiles = _alibi_tiles(slopes2)
    pmat, qmat = _regroup_matrices()

    x2 = x.reshape(n, D_MODEL)
    for i in range(depth):
        lambda_init = 0.8 - 0.6 * math.exp(-0.3 * i)
        u, q, k, v = _inproj(x2, g_mix[i][None, :], w_in[i].astype(BF16), tm)

        m, w, vv, dec_re, dec_im = _ssm_matrices(
            ssm_lambda_re[i], ssm_lambda_im[i], ssm_log_step[i], ssm_b_re[i], ssm_b_im[i],
            ssm_c_re[i], ssm_c_im[i], ssm_d[i])
        uc = _regroup_in(u, pmat, bsz)
        yc = _ssm(uc, m, w, vv, dec_re, dec_im, n_chunks, bsz)
        ys = _regroup_out(yc, qmat, bsz)

        lam = (jnp.exp(jnp.sum(lambda_q1[i].astype(F32) * lambda_k1[i].astype(F32)))
               - jnp.exp(jnp.sum(lambda_q2[i].astype(F32) * lambda_k2[i].astype(F32))) + lambda_init)
        par = jnp.concatenate([jnp.stack([lam, jnp.asarray(1.0 - lambda_init, F32)]),
                               jnp.asarray(slopes2, F32), jnp.zeros((2,), F32)])
        ya = _attention(par, q.reshape(bsz, seq, ATTN_WIDTH), k.reshape(bsz, seq, ATTN_WIDTH),
                        v.reshape(bsz, seq, ATTN_WIDTH), bias_tiles, g_subln[i].astype(F32)[:, None])
        ya = ya.reshape(n, ATTN_WIDTH)

        j = i // 2
        if i % 2 == 0:
            router_w = None
        else:
            router_w = (w_router[j].astype(F32).T,
                        jnp.concatenate([b_router[j].astype(F32),
                                         jnp.full((LANES - N_EXPERTS,), -jnp.inf, F32)])[None, :])
        outs = _postmix(x2, ys, ya, w_glu[i].astype(BF16), b_glu[i][None, :], g_ssm_out[i][None, :],
                        w_out[i].astype(BF16), g_ffn[i][None, :], router_w, tm)
        if i % 2 == 0:
            x2, h = outs
            wg = _pad_axis(dense_w_gate[j].astype(BF16), 1, d_ff_pad)
            wu = _pad_axis(dense_w_up[j].astype(BF16), 1, d_ff_pad)
            wd = _pad_axis(dense_w_down[j].astype(BF16), 0, d_ff_pad)
            x2 = _ffn(h, x2, wg, wu, wd, tm, d_ff_pad // 2)
        else:
            x2, h, rec = outs
            pos2d, texp, n_used = _route(rec, MOE_TILE, min(MOE_TOKEN_TILE, n))
            xs = _moe_dispatch(pos2d, h, texp.shape[0] * MOE_TILE)
            ys = _moe_ffn(texp, n_used, xs, moe_w_gate, moe_w_up, moe_w_down, j, MOE_TILE, 512)
            x2 = _moe_combine(pos2d, ys, x2, rec, g_final[None, :], i == depth - 1)
    return x2.reshape(bsz, seq, D_MODEL)
```

```python
import functools
import math

import numpy as np
import jax
import jax.numpy as jnp
from jax import lax
from jax.experimental import pallas as pl
from jax.experimental.pallas import tpu as pltpu

F32 = jnp.float32
BF16 = jnp.bfloat16

D_MODEL = 1024
SSM_WIDTH = 512
SSM_GROUP = 16
SSM_GROUPS = 32
SSM_STATE = 64
ATTN_WIDTH = 512
HEAD_DIM = 64
HEADS = 4
HEAD_V = 2 * HEAD_DIM
N_EXPERTS = 8
NORM_EPS = 1e-6

LANES = 128
SUBLANES = 8
KV_CHUNK = 512
TQ = 512
VT_ROWS = HEAD_V + 16
ALIBI_PIECES = 3
LOG2E = math.log2(math.e)
MOE_TILE = 1024
MOE_TOKEN_TILE = 512
SSM_CHUNK = 16
CHUNK_COLS = SSM_CHUNK * SSM_GROUP
GROUP_BLOCK = LANES // SSM_GROUP
STEP_PAIRS = SSM_CHUNK // 2
VMEM_LIMIT = 56 * 1024 * 1024


def _cparams(sem):
    return pltpu.CompilerParams(dimension_semantics=sem, vmem_limit_bytes=VMEM_LIMIT)


def _rms(x, g):
    return x * lax.rsqrt(jnp.mean(x * x, axis=-1, keepdims=True) + NORM_EPS) * g


def _inproj_kernel(x_ref, g_ref, w_ref, kcols_ref, u_ref, q_ref, k_ref, v_ref, us_ref):
    h = _rms(x_ref[...], g_ref[...]).astype(BF16)
    proj = lambda n: jnp.dot(h, w_ref[:, n * 512:(n + 1) * 512], preferred_element_type=F32)
    z = proj(0)
    n_chunks = us_ref.shape[1] // SSM_CHUNK
    for j in range(SSM_WIDTH // LANES):
        us_ref[j] = z[:, j * LANES:(j + 1) * LANES]
        for s in range(SSM_CHUNK):
            u_ref[s, :, j * LANES:(j + 1) * LANES] = us_ref[j, pl.ds(s, n_chunks, stride=SSM_CHUNK), :].astype(BF16)
    q_ref[...] = (proj(1) * (HEAD_DIM ** -0.5 * LOG2E)).astype(BF16)
    k = proj(2).astype(BF16)
    for hd in range(HEADS):
        k_ref[:, 2 * hd * HEAD_V:(2 * hd + 1) * HEAD_V] = k[:, hd * HEAD_V:(hd + 1) * HEAD_V]
        k_ref[:, (2 * hd + 1) * HEAD_V:(2 * hd + 2) * HEAD_V] = kcols_ref[:, hd * HEAD_V:(hd + 1) * HEAD_V]
    v_ref[...] = proj(3).astype(BF16)


def _inproj(x2, g, w_bf, kcols):
    n = x2.shape[0]
    tm = KV_CHUNK
    out = jax.ShapeDtypeStruct((n, 512), BF16)
    row = lambda c: pl.BlockSpec((tm, c), lambda i: (i, 0))
    return pl.pallas_call(
        _inproj_kernel,
        out_shape=(jax.ShapeDtypeStruct((SSM_CHUNK, n // SSM_CHUNK, SSM_WIDTH), BF16), out,
                   jax.ShapeDtypeStruct((n, 2 * ATTN_WIDTH), BF16), out),
        grid=(n // tm,),
        in_specs=[
            row(D_MODEL),
            pl.BlockSpec((1, D_MODEL), lambda i: (0, 0)),
            pl.BlockSpec((D_MODEL, 2048), lambda i: (0, 0)),
            pl.BlockSpec((tm, ATTN_WIDTH), lambda i: (0, 0)),
        ],
        out_specs=(pl.BlockSpec((SSM_CHUNK, tm // SSM_CHUNK, SSM_WIDTH), lambda i: (0, i, 0)),
                   row(512), row(2 * ATTN_WIDTH), row(512)),
        scratch_shapes=[pltpu.VMEM((SSM_WIDTH // LANES, tm, LANES), F32)],
        compiler_params=_cparams(("parallel",)),
        name="inproj",
    )(x2, g, w_bf, kcols)


def _ssm_matrices(lam_re, lam_im, log_step, b_re, b_im, c_re, c_im, d_skip):
    t = SSM_CHUNK
    tau = jnp.arange(t + 1, dtype=F32)
    ks, ws, es, decs = [], [], [], []
    for d in range(2):
        lam = lax.complex(lam_re[d].astype(F32), lam_im[d].astype(F32))
        step = jnp.exp(log_step[d].astype(F32))[:, None]
        ls = lam * step
        lam_bar = jnp.exp(ls)
        pw = jnp.exp(ls[:, None, :] * tau[None, :, None])
        b_bar = ((lam_bar - 1.0) / lam)[:, :, None] * lax.complex(b_re[d].astype(F32), b_im[d].astype(F32))
        c = lax.complex(c_re[d].astype(F32), c_im[d].astype(F32))
        ks.append(jnp.real(jnp.einsum('gop,gtp,gpi->gtoi', c, pw[:, :t], b_bar)))
        if d == 0:
            wpow = pw[:, :t][:, ::-1]
            epow = pw[:, 1:t + 1]
        else:
            wpow = pw[:, :t]
            epow = pw[:, 1:t + 1][:, ::-1]
        ws.append(jnp.einsum('gsp,gpi->gsip', wpow, b_bar).reshape(SSM_GROUPS, CHUNK_COLS, SSM_STATE))
        es.append(jnp.einsum('gop,gtp->gpto', c, epow).reshape(SSM_GROUPS, SSM_STATE, CHUNK_COLS))
        decs.append(pw[:, t])
    kf, kb = ks
    s_idx = jnp.arange(t)[:, None]
    t_idx = jnp.arange(t)[None, :]
    lag_f = t_idx - s_idx
    lag_b = s_idx - t_idx
    lags = jnp.arange(t)[None, None, :]
    mf = jnp.einsum('stk,gkoi->gstoi', (lag_f[:, :, None] == lags).astype(F32), kf)
    mb = jnp.einsum('stk,gkoi->gstoi', (lag_b[:, :, None] == lags).astype(F32), kb)
    dmat = (jnp.eye(t, dtype=F32)[None, :, :, None, None]
            * (jnp.eye(SSM_GROUP, dtype=F32)[None, None, None] * d_skip.astype(F32).reshape(SSM_GROUPS, 1, 1, 1, SSM_GROUP)))
    m = (mf + mb + dmat).transpose(0, 1, 4, 2, 3).reshape(SSM_GROUPS, CHUNK_COLS, CHUNK_COLS)
    w = jnp.concatenate([jnp.real(ws[0]), jnp.real(ws[1]), jnp.imag(ws[0]), jnp.imag(ws[1])], axis=-1)
    v = jnp.concatenate([jnp.real(es[0]), jnp.real(es[1]), -jnp.imag(es[0]), -jnp.imag(es[1])], axis=1)
    dec_re = jnp.concatenate([jnp.real(decs[0]), jnp.real(decs[1])], axis=-1)[:, None, :]
    dec_im = jnp.concatenate([jnp.imag(decs[0]), jnp.imag(decs[1])], axis=-1)[:, None, :]
    return m.astype(BF16), w.astype(BF16), v.astype(BF16), dec_re, dec_im


def _ssm_kernel(u_ref, m_ref, w_ref, v_ref, ar_ref, ai_ref, y_ref, s_ref, hf_ref, hb_ref, *, n_chunks, rows):
    u = u_ref[0]
    s = jnp.dot(u, w_ref[0], preferred_element_type=F32)
    pitch = n_chunks + 1
    for b in range(rows):
        s_ref[0, b * pitch:b * pitch + n_chunks, :] = s[b * n_chunks:(b + 1) * n_chunks, :LANES]
        s_ref[1, b * pitch:b * pitch + n_chunks, :] = s[b * n_chunks:(b + 1) * n_chunks, LANES:]
    ar = jnp.broadcast_to(ar_ref[0], (rows, LANES))
    ai = jnp.broadcast_to(ai_ref[0], (rows, LANES))

    def body(c, carry):
        fr, fi, br, bi = carry
        at_f = pl.ds(c, rows, stride=pitch)
        at_b = pl.ds(n_chunks - 1 - c, rows, stride=pitch)
        hf_ref[0, at_f, :] = fr
        hf_ref[1, at_f, :] = fi
        hb_ref[0, at_b, :] = br
        hb_ref[1, at_b, :] = bi
        nfr = ar * fr - ai * fi + s_ref[0, at_f, :]
        nfi = ai * fr + ar * fi + s_ref[1, at_f, :]
        nbr = ar * br - ai * bi + s_ref[0, at_b, :]
        nbi = ai * br + ar * bi + s_ref[1, at_b, :]
        return nfr, nfi, nbr, nbi

    z = jnp.zeros((rows, LANES), F32)
    lax.fori_loop(0, n_chunks, body, (z, z, z, z), unroll=4)
    is_fwd = lax.broadcasted_iota(jnp.int32, (1, LANES), 1) < SSM_STATE
    seq_rows = lambda ref, part: jnp.concatenate(
        [ref[part, b * pitch:b * pitch + n_chunks, :] for b in range(rows)], axis=0)
    hcat = jnp.concatenate([jnp.where(is_fwd, seq_rows(hf_ref, 0), seq_rows(hb_ref, 0)),
                            jnp.where(is_fwd, seq_rows(hf_ref, 1), seq_rows(hb_ref, 1))], axis=1).astype(BF16)
    y = jnp.dot(u, m_ref[0], preferred_element_type=F32)
    y = y + jnp.dot(hcat, v_ref[0], preferred_element_type=F32)
    y_ref[0] = jax.nn.gelu(y).astype(BF16)


def _ssm(uc, m, w, v, dec_re, dec_im, n_chunks, rows):
    g, r, _ = uc.shape
    mat = pl.BlockSpec((1, CHUNK_COLS, CHUNK_COLS), lambda i: (i, 0, 0))
    dec = pl.BlockSpec((1, 1, LANES), lambda i: (i, 0, 0))
    return pl.pallas_call(
        functools.partial(_ssm_kernel, n_chunks=n_chunks, rows=rows),
        out_shape=jax.ShapeDtypeStruct((g, r, CHUNK_COLS), BF16),
        grid=(g,),
        in_specs=[pl.BlockSpec((1, r, CHUNK_COLS), lambda i: (i, 0, 0)), mat, mat, mat, dec, dec],
        out_specs=pl.BlockSpec((1, r, CHUNK_COLS), lambda i: (i, 0, 0)),
        scratch_shapes=[pltpu.VMEM((2, rows * (n_chunks + 1), LANES), F32)] * 3,
        compiler_params=_cparams(("parallel",)),
        name="ssm",
    )(uc, m, w, v, dec_re, dec_im)


def _regroup_matrices():
    p = np.zeros((STEP_PAIRS, 2, GROUP_BLOCK, SSM_GROUP, GROUP_BLOCK, SSM_CHUNK, SSM_GROUP), np.float32)
    gl = np.arange(GROUP_BLOCK)[:, None]
    h = np.arange(SSM_GROUP)[None, :]
    for sp in range(STEP_PAIRS):
        for half in range(2):
            p[sp, half, gl, h, gl, sp + half * STEP_PAIRS, h] = 1.0
    p = p.reshape(STEP_PAIRS, 2 * LANES, GROUP_BLOCK, CHUNK_COLS)
    q = p.transpose(0, 2, 3, 1)
    return (jnp.asarray(p.reshape(STEP_PAIRS, 2 * LANES, GROUP_BLOCK * CHUNK_COLS), BF16), jnp.asarray(q, BF16))


def _regroup_in_kernel(us_ref, p_ref, o_ref):
    pairs = [jnp.concatenate([us_ref[sp], us_ref[sp + STEP_PAIRS]], axis=1) for sp in range(STEP_PAIRS)]
    for gl in range(GROUP_BLOCK):
        acc = None
        for sp in range(STEP_PAIRS):
            d = jnp.dot(pairs[sp], p_ref[sp, :, gl * CHUNK_COLS:(gl + 1) * CHUNK_COLS], preferred_element_type=F32)
            acc = d if acc is None else acc + d
        o_ref[gl] = acc.astype(BF16)


def _regroup_out_kernel(yc_ref, q_ref, o_ref):
    for tp in range(STEP_PAIRS):
        acc = None
        for gl in range(GROUP_BLOCK):
            d = jnp.dot(yc_ref[gl], q_ref[tp, gl], preferred_element_type=F32)
            acc = d if acc is None else acc + d
        o_ref[tp] = acc[:, :LANES].astype(BF16)
        o_ref[tp + STEP_PAIRS] = acc[:, LANES:].astype(BF16)


def _regroup_specs(c):
    steps = lambda f: pl.BlockSpec((SSM_CHUNK, c, LANES), f)
    groups = lambda f: pl.BlockSpec((GROUP_BLOCK, c, CHUNK_COLS), f)
    return steps(lambda gb, b: (0, b, gb)), groups(lambda gb, b: (gb, b, 0))


def _regroup_in(us, pmat, bsz):
    c = us.shape[1] // bsz
    steps, groups = _regroup_specs(c)
    return pl.pallas_call(
        _regroup_in_kernel,
        out_shape=jax.ShapeDtypeStruct((SSM_GROUPS, bsz * c, CHUNK_COLS), BF16),
        grid=(SSM_GROUPS // GROUP_BLOCK, bsz),
        in_specs=[steps, pl.BlockSpec(pmat.shape, lambda gb, b: (0, 0, 0))],
        out_specs=groups,
        compiler_params=_cparams(("parallel", "parallel")),
        name="regroup_in",
    )(us, pmat)


def _regroup_out(yc, qmat, bsz):
    c = yc.shape[1] // bsz
    steps, groups = _regroup_specs(c)
    return pl.pallas_call(
        _regroup_out_kernel,
        out_shape=jax.ShapeDtypeStruct((SSM_CHUNK, c * bsz, SSM_WIDTH), BF16),
        grid=(SSM_GROUPS // GROUP_BLOCK, bsz),
        in_specs=[groups, pl.BlockSpec(qmat.shape, lambda gb, b: (0, 0, 0, 0))],
        out_specs=steps,
        compiler_params=_cparams(("parallel", "parallel")),
        name="regroup_out",
    )(yc, qmat)


def _alibi_constants(slopes2):
    bf = jnp.bfloat16
    jj = np.arange(KV_CHUNK, dtype=np.float32)
    rel = np.arange(TQ, dtype=np.float32)[None, :] - jj[:, None]
    key_cols = np.zeros((KV_CHUNK, HEADS, HEAD_V), np.float32)
    for hd, s in enumerate(slopes2):
        rest = np.float32(s) * jj
        for piece in range(ALIBI_PIECES):
            part = rest.astype(bf).astype(np.float32)
            key_cols[:, hd, piece] = part
            rest = rest - part
    tiles = np.stack([np.float32(s) * rel for s in slopes2])
    return jnp.asarray(key_cols.reshape(KV_CHUNK, HEADS * HEAD_V), BF16), jnp.asarray(tiles, F32)


def _attn_kernel(par_ref, q_ref, k_ref, v_ref, bt_ref, g_ref, o_ref, vt_ref, qv_ref, acc_ref, sa_ref, sb_ref,
                 *, n_chunks):
    head = pl.program_id(1)
    qi = pl.program_id(2)
    lam = par_ref[0]
    out_scale = par_ref[1]
    slope = par_ref[2 + head]
    q_pos = slope * lax.broadcasted_iota(jnp.int32, (1, TQ), 1).astype(F32)

    @pl.when(qi == 0)
    def _():
        ones_rows = (lax.broadcasted_iota(jnp.int32, (VT_ROWS - HEAD_V, KV_CHUNK), 0) == 0).astype(BF16)
        for c in range(n_chunks):
            vt_ref[c, :HEAD_V] = v_ref[0, c * KV_CHUNK:(c + 1) * KV_CHUNK, :].astype(F32).T.astype(BF16)
            vt_ref[c, HEAD_V:] = ones_rows

    qt = q_ref[0].astype(F32).T.astype(BF16)
    zero = jnp.zeros((HEAD_DIM, TQ), BF16)
    pick = (lax.broadcasted_iota(jnp.int32, (HEAD_V, TQ), 0) < ALIBI_PIECES).astype(F32)
    for var, sign in enumerate((1.0, -1.0, 0.0)):
        qv_ref[var, :HEAD_DIM] = qt[:HEAD_DIM]
        qv_ref[var, HEAD_DIM:HEAD_V] = zero
        qv_ref[3 + var, :HEAD_DIM] = zero
        qv_ref[3 + var, HEAD_DIM:HEAD_V] = qt[HEAD_DIM:]
        qv_ref[var, HEAD_V:] = (sign * pick).astype(BF16)
        qv_ref[3 + var, HEAD_V:] = (sign * pick).astype(BF16)

    qd = qi // (KV_CHUNK // TQ)
    i0 = qi * TQ

    s_bufs = (sa_ref, sb_ref)

    def scores(t):
        buf = s_bufs[t % 2]
        if t == 0:
            c = qd
            var = 2
            off = jnp.zeros((1, TQ), F32)
            bias = -jnp.abs(bt_ref[0] + slope * (i0 - qd * KV_CHUNK).astype(F32))
        else:
            c = (t - 1) + ((t - 1) >= qd).astype(jnp.int32)
            after = c < qd
            var = jnp.where(after, 0, 1)
            off = -slope * jnp.abs(i0 - c * KV_CHUNK).astype(F32) + jnp.where(after, -q_pos, q_pos)
            bias = None
        k = k_ref[0, pl.ds(pl.multiple_of(c * KV_CHUNK, KV_CHUNK), KV_CHUNK), :]
        smax = []
        for mi in range(2):
            s = jnp.dot(k, qv_ref[3 * mi + var], preferred_element_type=F32)
            if bias is not None:
                s = s + bias
            buf[mi] = s
            smax.append(jnp.max(s, axis=0, keepdims=True) + off)
        return c, off, smax

    def accumulate(t, c, c_off, smax, m):
        buf = s_bufs[t % 2]
        vt = vt_ref[c]
        for mi in range(2):
            m_new = jnp.maximum(m[mi], smax[mi])
            p = jnp.exp2(buf[mi] - (m_new - c_off))
            alpha = jnp.exp2(m[mi] - m_new)
            acc_ref[mi] = alpha * acc_ref[mi] + jnp.dot(vt, p.astype(BF16), preferred_element_type=F32)
            m[mi] = m_new

    acc_ref[...] = jnp.zeros_like(acc_ref)
    m = [jnp.full((1, TQ), -1e30, F32)] * 2
    pending = scores(0)
    for t in range(n_chunks):
        nxt = scores(t + 1) if t + 1 < n_chunks else None
        accumulate(t, *pending, m)
        pending = nxt
    o = (acc_ref[0, :HEAD_V] / acc_ref[0, HEAD_V:HEAD_V + 1]
         - lam * (acc_ref[1, :HEAD_V] / acc_ref[1, HEAD_V:HEAD_V + 1]))
    o = o * lax.rsqrt(jnp.mean(o * o, axis=0, keepdims=True) + NORM_EPS)
    o = o * (g_ref[...] * out_scale)
    o_ref[0] = o.T.astype(BF16)


def _attention(par, q, k, v, bias_tiles, g_col):
    b, l, _ = q.shape
    n_chunks = l // KV_CHUNK
    return pl.pallas_call(
        functools.partial(_attn_kernel, n_chunks=n_chunks),
        out_shape=jax.ShapeDtypeStruct((b, l, ATTN_WIDTH), BF16),
        grid=(b, HEADS, l // TQ),
        in_specs=[
            pl.BlockSpec(memory_space=pltpu.SMEM),
            pl.BlockSpec((1, TQ, HEAD_V), lambda bi, h, i: (bi, i, h)),
            pl.BlockSpec((1, l, 2 * HEAD_V), lambda bi, h, i: (bi, 0, h)),
            pl.BlockSpec((1, l, HEAD_V), lambda bi, h, i: (bi, 0, h)),
            pl.BlockSpec((1, KV_CHUNK, TQ), lambda bi, h, i: (h, 0, 0)),
            pl.BlockSpec((HEAD_V, 1), lambda bi, h, i: (0, 0)),
        ],
        out_specs=pl.BlockSpec((1, TQ, HEAD_V), lambda bi, h, i: (bi, i, h)),
        scratch_shapes=[
            pltpu.VMEM((n_chunks, VT_ROWS, KV_CHUNK), BF16),
            pltpu.VMEM((6, 2 * HEAD_V, TQ), BF16),
            pltpu.VMEM((2, VT_ROWS, TQ), F32),
            pltpu.VMEM((2, KV_CHUNK, TQ), F32),
            pltpu.VMEM((2, KV_CHUNK, TQ), F32),
        ],
        compiler_params=_cparams(("parallel", "parallel", "arbitrary")),
        name="diff_attn",
    )(par, q, k, v, bias_tiles, g_col)


def _postmix_kernel(x_ref, ys_ref, ya_ref, wglu_ref, bglu_ref, gs_ref, wout_ref, gf_ref, *rest, router):
    if router:
        wr_ref, br_ref, xo_ref, h_ref, gate_ref, ysc_ref = rest
    else:
        xo_ref, h_ref, ysc_ref = rest
    n_chunks = ys_ref.shape[1]
    for j in range(SSM_WIDTH // LANES):
        for s in range(SSM_CHUNK):
            ysc_ref[j, pl.ds(s, n_chunks, stride=SSM_CHUNK), :] = ys_ref[s, :, j * LANES:(j + 1) * LANES].astype(F32)
    y = jnp.concatenate([ysc_ref[j] for j in range(SSM_WIDTH // LANES)], axis=1)
    t = jnp.dot(y.astype(BF16), wglu_ref[...], preferred_element_type=F32) + bglu_ref[...]
    y = y * jax.nn.sigmoid(t)
    y = _rms(y, gs_ref[...]).astype(BF16)
    mix = jnp.dot(y, wout_ref[:SSM_WIDTH, :], preferred_element_type=F32)
    mix = mix + jnp.dot(ya_ref[...], wout_ref[SSM_WIDTH:, :], preferred_element_type=F32)
    x = x_ref[...] + mix
    xo_ref[...] = x
    h = _rms(x, gf_ref[...])
    h_ref[...] = h.astype(h_ref.dtype)
    if router:
        lane = lax.broadcasted_iota(jnp.int32, (h.shape[0], LANES), 1)
        logits = jnp.broadcast_to(br_ref[...], lane.shape)
        for e in range(N_EXPERTS):
            le = jnp.sum(h * wr_ref[e:e + 1, :], axis=-1, keepdims=True)
            logits = jnp.where(lane == e, logits + le, logits)
        big = jnp.int32(LANES)
        m1 = jnp.max(logits, axis=-1, keepdims=True)
        i1 = jnp.min(jnp.where(logits == m1, lane, big), axis=-1, keepdims=True)
        rest_l = jnp.where(lane == i1, -jnp.inf, logits)
        m2 = jnp.max(rest_l, axis=-1, keepdims=True)
        i2 = jnp.min(jnp.where(rest_l == m2, lane, big), axis=-1, keepdims=True)
        e2 = jnp.exp(m2 - m1)
        w1 = 1.0 / (1.0 + e2)
        w2 = e2 * w1
        rec = jnp.where(lane == 0, w1, jnp.where(lane == 1, w2, jnp.where(
            lane == 2, i1.astype(F32), jnp.where(lane == 3, i2.astype(F32), 0.0))))
        gate_ref[...] = rec[:, :N_EXPERTS]


def _postmix(x2, ys, ya, wglu, bglu, gs, wout, gf, router_w, tm):
    n = x2.shape[0]
    router = router_w is not None
    row = lambda c: pl.BlockSpec((tm, c), lambda i: (i, 0))
    full = lambda r, c: pl.BlockSpec((r, c), lambda i: (0, 0))
    steps = pl.BlockSpec((SSM_CHUNK, tm // SSM_CHUNK, SSM_WIDTH), lambda i: (0, i, 0))
    in_specs = [row(D_MODEL), steps, row(512), full(512, 512), full(1, 512), full(1, 512),
                full(D_MODEL, D_MODEL), full(1, D_MODEL)]
    args = [x2, ys, ya, wglu, bglu, gs, wout, gf]
    out_shape = [jax.ShapeDtypeStruct((n, D_MODEL), F32), jax.ShapeDtypeStruct((n, D_MODEL), F32 if router else BF16)]
    out_specs = [row(D_MODEL), row(D_MODEL)]
    if router:
        in_specs += [full(N_EXPERTS, D_MODEL), full(1, LANES)]
        args += list(router_w)
        out_shape.append(jax.ShapeDtypeStruct((n, N_EXPERTS), F32))
        out_specs.append(row(N_EXPERTS))
    return pl.pallas_call(
        functools.partial(_postmix_kernel, router=router),
        out_shape=tuple(out_shape),
        grid=(n // tm,),
        in_specs=in_specs,
        out_specs=tuple(out_specs),
        scratch_shapes=[pltpu.VMEM((SSM_WIDTH // LANES, tm, LANES), F32)],
        compiler_params=_cparams(("parallel",)),
        name="postmix_router" if router else "postmix",
    )(*args)


def _swiglu_partial(h, wg, wu, wd):
    g = jnp.dot(h, wg, preferred_element_type=F32)
    u = jnp.dot(h, wu, preferred_element_type=F32)
    return jnp.dot((jax.nn.silu(g) * u).astype(BF16), wd, preferred_element_type=F32)


def _ffn_kernel(h_ref, x_ref, wg_ref, wu_ref, wd_ref, o_ref):
    @pl.when(pl.program_id(1) == 0)
    def _():
        o_ref[...] = x_ref[...]

    o_ref[...] += _swiglu_partial(h_ref[...], wg_ref[...], wu_ref[...], wd_ref[...])


def _ffn(h, x2, wg, wu, wd, tm, tf):
    n = x2.shape[0]
    ff = wg.shape[1]
    row = lambda c: pl.BlockSpec((tm, c), lambda i, f: (i, 0))
    return pl.pallas_call(
        _ffn_kernel,
        out_shape=jax.ShapeDtypeStruct((n, D_MODEL), F32),
        grid=(n // tm, ff // tf),
        in_specs=[row(D_MODEL), row(D_MODEL),
                  pl.BlockSpec((D_MODEL, tf), lambda i, f: (0, f)),
                  pl.BlockSpec((D_MODEL, tf), lambda i, f: (0, f)),
                  pl.BlockSpec((tf, D_MODEL), lambda i, f: (f, 0))],
        out_specs=row(D_MODEL),
        compiler_params=_cparams(("parallel", "arbitrary")),
        name="ffn_dense",
    )(h, x2, wg, wu, wd)


def _route(rec, tm, tc):
    n = rec.shape[0]
    a = 2 * n
    n_tiles = a // tm + N_EXPERTS
    e = rec[:, 2:4].astype(jnp.int32).reshape(a)
    onehot = (e[:, None] == jnp.arange(N_EXPERTS, dtype=jnp.int32)[None, :]).astype(jnp.int32)
    csum = jnp.cumsum(onehot, axis=0)
    rank = jnp.sum(csum * onehot, axis=1) - 1
    padded = (csum[-1] + tm - 1) // tm * tm
    ends = jnp.cumsum(padded)
    pos = jnp.sum(onehot * (ends - padded)[None, :], axis=1) + rank
    n_used = (ends[-1] // tm).astype(jnp.int32)
    tile = jnp.arange(n_tiles, dtype=jnp.int32)
    texp = jnp.sum((tile[:, None] >= (ends // tm)[None, :]).astype(jnp.int32), axis=1)
    texp = jnp.minimum(texp, N_EXPERTS - 1)
    texp = jnp.where(tile < n_used, texp, texp[jnp.maximum(n_used - 1, 0)])
    pos2d = pos.reshape(n // tc, tc, 2).transpose(0, 2, 1).reshape(n // tc, 2 * tc)
    return pos2d, texp, n_used.reshape(1)


def _fetch_rows_index(pos_hbm, idx_smem, sem_idx):
    cp = pltpu.make_async_copy(pos_hbm.at[pl.program_id(0)], idx_smem, sem_idx)
    cp.start()
    cp.wait()


def _moe_dispatch_kernel(pos_hbm, h_ref, xs_init, xs_hbm, idx_smem, sem_idx, sem):
    del xs_init
    _fetch_rows_index(pos_hbm, idx_smem, sem_idx)
    tc = h_ref.shape[0]

    def issue(r, carry):
        src = h_ref.at[pl.ds(r, 1)]
        pltpu.make_async_copy(src, xs_hbm.at[pl.ds(idx_smem[r], 1)], sem).start()
        pltpu.make_async_copy(src, xs_hbm.at[pl.ds(idx_smem[tc + r], 1)], sem).start()
        return carry

    lax.fori_loop(0, tc, issue, 0, unroll=8)
    for _ in range(2):
        pltpu.make_async_copy(h_ref, xs_hbm.at[pl.ds(0, tc)], sem).wait()


def _moe_dispatch(pos2d, h, rows):
    n = h.shape[0]
    tc = pos2d.shape[1] // 2
    return pl.pallas_call(
        _moe_dispatch_kernel,
        out_shape=jax.ShapeDtypeStruct((rows, D_MODEL), F32),
        grid=(n // tc,),
        in_specs=[pl.BlockSpec(memory_space=pl.ANY), pl.BlockSpec((tc, D_MODEL), lambda i: (i, 0)),
                  pl.BlockSpec(memory_space=pl.ANY)],
        out_specs=pl.BlockSpec(memory_space=pl.ANY),
        scratch_shapes=[pltpu.SMEM((2 * tc,), jnp.int32), pltpu.SemaphoreType.DMA, pltpu.SemaphoreType.DMA],
        input_output_aliases={2: 0},
        compiler_params=_cparams(("arbitrary",)),
        name="moe_dispatch",
    )(pos2d, h, jnp.zeros((rows, D_MODEL), F32))


def _moe_ffn_kernel(texp_ref, nused_ref, xs_ref, wg_ref, wu_ref, wd_ref, o_ref, xb_ref):
    t = pl.program_id(0)
    f = pl.program_id(1)
    last = f == pl.num_programs(1) - 1
    valid = t < nused_ref[0]

    @pl.when(valid)
    def _():
        @pl.when(f == 0)
        def _():
            xb_ref[...] = xs_ref[...].astype(BF16)
            o_ref[...] = jnp.zeros_like(o_ref)

        o_ref[...] += _swiglu_partial(xb_ref[...], wg_ref[0, 0].astype(BF16), wu_ref[0, 0].astype(BF16),
                                      wd_ref[0, 0].astype(BF16))

    @pl.when(jnp.logical_and(jnp.logical_not(valid), last))
    def _():
        o_ref[...] = jnp.zeros_like(o_ref)


def _moe_ffn(texp, n_used, xs, wg, wu, wd, layer, tm, tf):
    rows = xs.shape[0]
    ff = wg.shape[-1]
    nf = ff // tf
    fidx = lambda t, f, nu: jnp.where(t < nu[0], f, nf - 1)
    row = pl.BlockSpec((tm, D_MODEL), lambda t, f, te, nu: (t, 0))
    return pl.pallas_call(
        _moe_ffn_kernel,
        out_shape=jax.ShapeDtypeStruct((rows, D_MODEL), F32),
        grid_spec=pltpu.PrefetchScalarGridSpec(
            num_scalar_prefetch=2,
            grid=(rows // tm, nf),
            in_specs=[row,
                      pl.BlockSpec((1, 1, D_MODEL, tf), lambda t, f, te, nu: (layer, te[t], 0, fidx(t, f, nu))),
                      pl.BlockSpec((1, 1, D_MODEL, tf), lambda t, f, te, nu: (layer, te[t], 0, fidx(t, f, nu))),
                      pl.BlockSpec((1, 1, tf, D_MODEL), lambda t, f, te, nu: (layer, te[t], fidx(t, f, nu), 0))],
            out_specs=row,
            scratch_shapes=[pltpu.VMEM((tm, D_MODEL), BF16)],
        ),
        compiler_params=_cparams(("arbitrary", "arbitrary")),
        name="moe_ffn",
    )(texp, n_used, xs, wg, wu, wd)


def _moe_combine_kernel(pos_hbm, ys_hbm, x_ref, rec_ref, gfin_ref, o_ref, idx_smem, buf, sem_idx, sem, *, final):
    _fetch_rows_index(pos_hbm, idx_smem, sem_idx)
    count = buf.shape[0]

    def issue(r, carry):
        pltpu.make_async_copy(ys_hbm.at[pl.ds(idx_smem[r], 1)], buf.at[pl.ds(r, 1)], sem).start()
        return carry

    lax.fori_loop(0, count, issue, 0, unroll=8)
    pltpu.make_async_copy(ys_hbm.at[pl.ds(0, count)], buf, sem).wait()
    tc = x_ref.shape[0]
    rec = rec_ref[...]
    out = x_ref[...] + rec[:, 0:1] * buf[:tc] + rec[:, 1:2] * buf[tc:]
    if final:
        out = _rms(out, gfin_ref[...])
    o_ref[...] = out


def _moe_combine(pos2d, ys, x2, rec, gfin, final):
    n = x2.shape[0]
    tc = pos2d.shape[1] // 2
    row = lambda c: pl.BlockSpec((tc, c), lambda i: (i, 0))
    return pl.pallas_call(
        functools.partial(_moe_combine_kernel, final=final),
        out_shape=jax.ShapeDtypeStruct((n, D_MODEL), F32),
        grid=(n // tc,),
        in_specs=[pl.BlockSpec(memory_space=pl.ANY), pl.BlockSpec(memory_space=pl.ANY), row(D_MODEL),
                  row(N_EXPERTS), pl.BlockSpec((1, D_MODEL), lambda i: (0, 0))],
        out_specs=row(D_MODEL),
        scratch_shapes=[pltpu.SMEM((2 * tc,), jnp.int32), pltpu.VMEM((2 * tc, D_MODEL), F32),
                        pltpu.SemaphoreType.DMA, pltpu.SemaphoreType.DMA],
        compiler_params=_cparams(("arbitrary",)),
        name="moe_combine",
    )(pos2d, ys, x2, rec, gfin)


def _pad_axis(a, axis, size):
    pad = [(0, 0)] * a.ndim
    pad[axis] = (0, size - a.shape[axis])
    return jnp.pad(a, pad)


def kernel(x, g_mix, w_in, ssm_lambda_re, ssm_lambda_im, ssm_log_step, ssm_b_re, ssm_b_im, ssm_c_re, ssm_c_im, ssm_d, w_glu, b_glu, g_ssm_out, lambda_q1, lambda_k1, lambda_q2, lambda_k2, g_subln, w_out, g_ffn, dense_w_gate, dense_w_up, dense_w_down, w_router, b_router, moe_w_gate, moe_w_up, moe_w_down, g_final):
    bsz, seq, _ = x.shape
    n = bsz * seq
    depth = w_in.shape[0]
    n_chunks = seq // SSM_CHUNK
    assert seq % KV_CHUNK == 0 and bsz == SUBLANES and depth % 2 == 0 and (2 * n) % MOE_TILE == 0
    tm = min(512, n)
    tm_ffn = min(1024, n)
    d_ff = dense_w_gate.shape[-1]
    d_ff_pad = -(-d_ff // 256) * 256
    slopes2 = [2.0 ** (-8.0 * (h + 1) / HEADS) * LOG2E for h in range(HEADS)]
    kcols, bias_tiles = _alibi_constants(slopes2)
    pmat, qmat = _regroup_matrices()

    x2 = x.reshape(n, D_MODEL)
    for i in range(depth):
        lambda_init = 0.8 - 0.6 * math.exp(-0.3 * i)
        u, q, k, v = _inproj(x2, g_mix[i][None, :], w_in[i].astype(BF16), kcols)

        m, w, vv, dec_re, dec_im = _ssm_matrices(
            ssm_lambda_re[i], ssm_lambda_im[i], ssm_log_step[i], ssm_b_re[i], ssm_b_im[i],
            ssm_c_re[i], ssm_c_im[i], ssm_d[i])
        uc = _regroup_in(u, pmat, bsz)
        yc = _ssm(uc, m, w, vv, dec_re, dec_im, n_chunks, bsz)
        ys = _regroup_out(yc, qmat, bsz)

        lam = (jnp.exp(jnp.sum(lambda_q1[i].astype(F32) * lambda_k1[i].astype(F32)))
               - jnp.exp(jnp.sum(lambda_q2[i].astype(F32) * lambda_k2[i].astype(F32))) + lambda_init)
        par = jnp.concatenate([jnp.stack([lam, jnp.asarray(1.0 - lambda_init, F32)]),
                               jnp.asarray(slopes2, F32), jnp.zeros((2,), F32)])
        ya = _attention(par, q.reshape(bsz, seq, ATTN_WIDTH), k.reshape(bsz, seq, 2 * ATTN_WIDTH),
                        v.reshape(bsz, seq, ATTN_WIDTH), bias_tiles, g_subln[i].astype(F32)[:, None])
        ya = ya.reshape(n, ATTN_WIDTH)

        j = i // 2
        if i % 2 == 0:
            router_w = None
        else:
            router_w = (w_router[j].astype(F32).T,
                        jnp.concatenate([b_router[j].astype(F32),
                                         jnp.full((LANES - N_EXPERTS,), -jnp.inf, F32)])[None, :])
        outs = _postmix(x2, ys, ya, w_glu[i].astype(BF16), b_glu[i][None, :], g_ssm_out[i][None, :],
                        w_out[i].astype(BF16), g_ffn[i][None, :], router_w, tm)
        if i % 2 == 0:
            x2, h = outs
            wg = _pad_axis(dense_w_gate[j].astype(BF16), 1, d_ff_pad)
            wu = _pad_axis(dense_w_up[j].astype(BF16), 1, d_ff_pad)
            wd = _pad_axis(dense_w_down[j].astype(BF16), 0, d_ff_pad)
            x2 = _ffn(h, x2, wg, wu, wd, tm, d_ff_pad // 2)
        else:
            x2, h, rec = outs
            pos2d, texp, n_used = _route(rec, MOE_TILE, min(MOE_TOKEN_TILE, n))
            xs = _moe_dispatch(pos2d, h, texp.shape[0] * MOE_TILE)
            ys = _moe_ffn(texp, n_used, xs, moe_w_gate, moe_w_up, moe_w_down, j, MOE_TILE, 512)
            x2 = _moe_combine(pos2d, ys, x2, rec, g_final[None, :], i == depth - 1)
    return x2.reshape(bsz, seq, D_MODEL)
```

```python
import functools
import math

import numpy as np
import jax
import jax.numpy as jnp
from jax import lax
from jax.experimental import pallas as pl
from jax.experimental.pallas import tpu as pltpu

F32 = jnp.float32
BF16 = jnp.bfloat16

D_MODEL = 1024
SSM_WIDTH = 512
SSM_GROUP = 16
SSM_GROUPS = 32
SSM_STATE = 64
ATTN_WIDTH = 512
HEAD_DIM = 64
HEADS = 4
HEAD_V = 2 * HEAD_DIM
N_EXPERTS = 8
NORM_EPS = 1e-6

LANES = 128
SUBLANES = 8
KV_CHUNK = 512
TQ = 512
VT_ROWS = HEAD_V + 16
ALIBI_PIECES = 3
LOG2E = math.log2(math.e)
MOE_TILE = 1024
MOE_TOKEN_TILE = 512
SSM_CHUNK = 16
CHUNK_COLS = SSM_CHUNK * SSM_GROUP
GROUP_BLOCK = LANES // SSM_GROUP
STEP_PAIRS = SSM_CHUNK // 2
VMEM_LIMIT = 56 * 1024 * 1024


def _cparams(sem):
    return pltpu.CompilerParams(dimension_semantics=sem, vmem_limit_bytes=VMEM_LIMIT)


def _rms(x, g):
    return x * lax.rsqrt(jnp.mean(x * x, axis=-1, keepdims=True) + NORM_EPS) * g


def _inproj_kernel(x_ref, g_ref, w_ref, kcols_ref, u_ref, q_ref, k_ref, v_ref, us_ref):
    h = _rms(x_ref[...], g_ref[...]).astype(BF16)
    proj = lambda n: jnp.dot(h, w_ref[:, n * 512:(n + 1) * 512], preferred_element_type=F32)
    z = proj(0)
    n_chunks = us_ref.shape[1] // SSM_CHUNK
    for j in range(SSM_WIDTH // LANES):
        us_ref[j] = z[:, j * LANES:(j + 1) * LANES]
        for s in range(SSM_CHUNK):
            u_ref[s, :, j * LANES:(j + 1) * LANES] = us_ref[j, pl.ds(s, n_chunks, stride=SSM_CHUNK), :].astype(BF16)
    q_ref[0] = (proj(1) * (HEAD_DIM ** -0.5 * LOG2E)).T.astype(BF16)
    k = proj(2).astype(BF16)
    for hd in range(HEADS):
        k_ref[:, 2 * hd * HEAD_V:(2 * hd + 1) * HEAD_V] = k[:, hd * HEAD_V:(hd + 1) * HEAD_V]
        k_ref[:, (2 * hd + 1) * HEAD_V:(2 * hd + 2) * HEAD_V] = kcols_ref[:, hd * HEAD_V:(hd + 1) * HEAD_V]
    v_ref[...] = proj(3).astype(BF16)


def _inproj(x2, g, w_bf, kcols, seq):
    n = x2.shape[0]
    tm = KV_CHUNK
    tiles_per_seq = seq // tm
    out = jax.ShapeDtypeStruct((n, 512), BF16)
    row = lambda c: pl.BlockSpec((tm, c), lambda i: (i, 0))
    return pl.pallas_call(
        _inproj_kernel,
        out_shape=(jax.ShapeDtypeStruct((SSM_CHUNK, n // SSM_CHUNK, SSM_WIDTH), BF16),
                   jax.ShapeDtypeStruct((n // seq, ATTN_WIDTH, seq), BF16),
                   jax.ShapeDtypeStruct((n, 2 * ATTN_WIDTH), BF16), out),
        grid=(n // tm,),
        in_specs=[
            row(D_MODEL),
            pl.BlockSpec((1, D_MODEL), lambda i: (0, 0)),
            pl.BlockSpec((D_MODEL, 2048), lambda i: (0, 0)),
            pl.BlockSpec((tm, ATTN_WIDTH), lambda i: (0, 0)),
        ],
        out_specs=(pl.BlockSpec((SSM_CHUNK, tm // SSM_CHUNK, SSM_WIDTH), lambda i: (0, i, 0)),
                   pl.BlockSpec((1, ATTN_WIDTH, tm), lambda i: (i // tiles_per_seq, 0, i % tiles_per_seq)),
                   row(2 * ATTN_WIDTH), row(512)),
        scratch_shapes=[pltpu.VMEM((SSM_WIDTH // LANES, tm, LANES), F32)],
        compiler_params=_cparams(("parallel",)),
        name="inproj",
    )(x2, g, w_bf, kcols)


def _ssm_matrices(lam_re, lam_im, log_step, b_re, b_im, c_re, c_im, d_skip):
    t = SSM_CHUNK
    tau = jnp.arange(t + 1, dtype=F32)
    ks, ws, es, decs = [], [], [], []
    for d in range(2):
        lam = lax.complex(lam_re[d].astype(F32), lam_im[d].astype(F32))
        step = jnp.exp(log_step[d].astype(F32))[:, None]
        ls = lam * step
        lam_bar = jnp.exp(ls)
        pw = jnp.exp(ls[:, None, :] * tau[None, :, None])
        b_bar = ((lam_bar - 1.0) / lam)[:, :, None] * lax.complex(b_re[d].astype(F32), b_im[d].astype(F32))
        c = lax.complex(c_re[d].astype(F32), c_im[d].astype(F32))
        ks.append(jnp.real(jnp.einsum('gop,gtp,gpi->gtoi', c, pw[:, :t], b_bar)))
        if d == 0:
            wpow = pw[:, :t][:, ::-1]
            epow = pw[:, 1:t + 1]
        else:
            wpow = pw[:, :t]
            epow = pw[:, 1:t + 1][:, ::-1]
        ws.append(jnp.einsum('gsp,gpi->gsip', wpow, b_bar).reshape(SSM_GROUPS, CHUNK_COLS, SSM_STATE))
        es.append(jnp.einsum('gop,gtp->gpto', c, epow).reshape(SSM_GROUPS, SSM_STATE, CHUNK_COLS))
        decs.append(pw[:, t])
    kf, kb = ks
    s_idx = jnp.arange(t)[:, None]
    t_idx = jnp.arange(t)[None, :]
    lag_f = t_idx - s_idx
    lag_b = s_idx - t_idx
    lags = jnp.arange(t)[None, None, :]
    mf = jnp.einsum('stk,gkoi->gstoi', (lag_f[:, :, None] == lags).astype(F32), kf)
    mb = jnp.einsum('stk,gkoi->gstoi', (lag_b[:, :, None] == lags).astype(F32), kb)
    dmat = (jnp.eye(t, dtype=F32)[None, :, :, None, None]
            * (jnp.eye(SSM_GROUP, dtype=F32)[None, None, None] * d_skip.astype(F32).reshape(SSM_GROUPS, 1, 1, 1, SSM_GROUP)))
    m = (mf + mb + dmat).transpose(0, 1, 4, 2, 3).reshape(SSM_GROUPS, CHUNK_COLS, CHUNK_COLS)
    w = jnp.concatenate([jnp.real(ws[0]), jnp.real(ws[1]), jnp.imag(ws[0]), jnp.imag(ws[1])], axis=-1)
    v = jnp.concatenate([jnp.real(es[0]), jnp.real(es[1]), -jnp.imag(es[0]), -jnp.imag(es[1])], axis=1)
    dec_re = jnp.concatenate([jnp.real(decs[0]), jnp.real(decs[1])], axis=-1)[:, None, :]
    dec_im = jnp.concatenate([jnp.imag(decs[0]), jnp.imag(decs[1])], axis=-1)[:, None, :]
    return m.astype(BF16), w.astype(BF16), v.astype(BF16), dec_re, dec_im


def _ssm_kernel(u_ref, m_ref, w_ref, v_ref, ar_ref, ai_ref, y_ref, s_ref, hf_ref, hb_ref, *, n_chunks, rows):
    u = u_ref[0]
    s = jnp.dot(u, w_ref[0], preferred_element_type=F32)
    pitch = n_chunks + 1
    for b in range(rows):
        s_ref[0, b * pitch:b * pitch + n_chunks, :] = s[b * n_chunks:(b + 1) * n_chunks, :LANES]
        s_ref[1, b * pitch:b * pitch + n_chunks, :] = s[b * n_chunks:(b + 1) * n_chunks, LANES:]
    ar = jnp.broadcast_to(ar_ref[0], (rows, LANES))
    ai = jnp.broadcast_to(ai_ref[0], (rows, LANES))

    def body(c, carry):
        fr, fi, br, bi = carry
        at_f = pl.ds(c, rows, stride=pitch)
        at_b = pl.ds(n_chunks - 1 - c, rows, stride=pitch)
        hf_ref[0, at_f, :] = fr
        hf_ref[1, at_f, :] = fi
        hb_ref[0, at_b, :] = br
        hb_ref[1, at_b, :] = bi
        nfr = ar * fr - ai * fi + s_ref[0, at_f, :]
        nfi = ai * fr + ar * fi + s_ref[1, at_f, :]
        nbr = ar * br - ai * bi + s_ref[0, at_b, :]
        nbi = ai * br + ar * bi + s_ref[1, at_b, :]
        return nfr, nfi, nbr, nbi

    z = jnp.zeros((rows, LANES), F32)
    lax.fori_loop(0, n_chunks, body, (z, z, z, z), unroll=4)
    is_fwd = lax.broadcasted_iota(jnp.int32, (1, LANES), 1) < SSM_STATE
    seq_rows = lambda ref, part: jnp.concatenate(
        [ref[part, b * pitch:b * pitch + n_chunks, :] for b in range(rows)], axis=0)
    hcat = jnp.concatenate([jnp.where(is_fwd, seq_rows(hf_ref, 0), seq_rows(hb_ref, 0)),
                            jnp.where(is_fwd, seq_rows(hf_ref, 1), seq_rows(hb_ref, 1))], axis=1).astype(BF16)
    y = jnp.dot(u, m_ref[0], preferred_element_type=F32)
    y = y + jnp.dot(hcat, v_ref[0], preferred_element_type=F32)
    y_ref[0] = jax.nn.gelu(y).astype(BF16)


def _ssm(uc, m, w, v, dec_re, dec_im, n_chunks, rows):
    g, r, _ = uc.shape
    mat = pl.BlockSpec((1, CHUNK_COLS, CHUNK_COLS), lambda i: (i, 0, 0))
    dec = pl.BlockSpec((1, 1, LANES), lambda i: (i, 0, 0))
    return pl.pallas_call(
        functools.partial(_ssm_kernel, n_chunks=n_chunks, rows=rows),
        out_shape=jax.ShapeDtypeStruct((g, r, CHUNK_COLS), BF16),
        grid=(g,),
        in_specs=[pl.BlockSpec((1, r, CHUNK_COLS), lambda i: (i, 0, 0)), mat, mat, mat, dec, dec],
        out_specs=pl.BlockSpec((1, r, CHUNK_COLS), lambda i: (i, 0, 0)),
        scratch_shapes=[pltpu.VMEM((2, rows * (n_chunks + 1), LANES), F32)] * 3,
        compiler_params=_cparams(("parallel",)),
        name="ssm",
    )(uc, m, w, v, dec_re, dec_im)


def _regroup_matrices():
    p = np.zeros((STEP_PAIRS, 2, GROUP_BLOCK, SSM_GROUP, GROUP_BLOCK, SSM_CHUNK, SSM_GROUP), np.float32)
    gl = np.arange(GROUP_BLOCK)[:, None]
    h = np.arange(SSM_GROUP)[None, :]
    for sp in range(STEP_PAIRS):
        for half in range(2):
            p[sp, half, gl, h, gl, sp + half * STEP_PAIRS, h] = 1.0
    p = p.reshape(STEP_PAIRS, 2 * LANES, GROUP_BLOCK, CHUNK_COLS)
    q = p.transpose(0, 2, 3, 1)
    return (jnp.asarray(p.reshape(STEP_PAIRS, 2 * LANES, GROUP_BLOCK * CHUNK_COLS), BF16), jnp.asarray(q, BF16))


def _regroup_in_kernel(us_ref, p_ref, o_ref):
    pairs = [jnp.concatenate([us_ref[sp], us_ref[sp + STEP_PAIRS]], axis=1) for sp in range(STEP_PAIRS)]
    for gl in range(GROUP_BLOCK):
        acc = None
        for sp in range(STEP_PAIRS):
            d = jnp.dot(pairs[sp], p_ref[sp, :, gl * CHUNK_COLS:(gl + 1) * CHUNK_COLS], preferred_element_type=F32)
            acc = d if acc is None else acc + d
        o_ref[gl] = acc.astype(BF16)


def _regroup_out_kernel(yc_ref, q_ref, o_ref):
    for tp in range(STEP_PAIRS):
        acc = None
        for gl in range(GROUP_BLOCK):
            d = jnp.dot(yc_ref[gl], q_ref[tp, gl], preferred_element_type=F32)
            acc = d if acc is None else acc + d
        o_ref[tp] = acc[:, :LANES].astype(BF16)
        o_ref[tp + STEP_PAIRS] = acc[:, LANES:].astype(BF16)


def _regroup_specs(c):
    steps = lambda f: pl.BlockSpec((SSM_CHUNK, c, LANES), f)
    groups = lambda f: pl.BlockSpec((GROUP_BLOCK, c, CHUNK_COLS), f)
    return steps(lambda gb, b: (0, b, gb)), groups(lambda gb, b: (gb, b, 0))


def _regroup_in(us, pmat, bsz):
    c = us.shape[1] // bsz
    steps, groups = _regroup_specs(c)
    return pl.pallas_call(
        _regroup_in_kernel,
        out_shape=jax.ShapeDtypeStruct((SSM_GROUPS, bsz * c, CHUNK_COLS), BF16),
        grid=(SSM_GROUPS // GROUP_BLOCK, bsz),
        in_specs=[steps, pl.BlockSpec(pmat.shape, lambda gb, b: (0, 0, 0))],
        out_specs=groups,
        compiler_params=_cparams(("parallel", "parallel")),
        name="regroup_in",
    )(us, pmat)


def _regroup_out(yc, qmat, bsz):
    c = yc.shape[1] // bsz
    steps, groups = _regroup_specs(c)
    return pl.pallas_call(
        _regroup_out_kernel,
        out_shape=jax.ShapeDtypeStruct((SSM_CHUNK, c * bsz, SSM_WIDTH), BF16),
        grid=(SSM_GROUPS // GROUP_BLOCK, bsz),
        in_specs=[groups, pl.BlockSpec(qmat.shape, lambda gb, b: (0, 0, 0, 0))],
        out_specs=steps,
        compiler_params=_cparams(("parallel", "parallel")),
        name="regroup_out",
    )(yc, qmat)


def _alibi_constants(slopes2):
    bf = jnp.bfloat16
    jj = np.arange(KV_CHUNK, dtype=np.float32)
    rel = np.arange(TQ, dtype=np.float32)[None, :] - jj[:, None]
    key_cols = np.zeros((KV_CHUNK, HEADS, HEAD_V), np.float32)
    for hd, s in enumerate(slopes2):
        rest = np.float32(s) * jj
        for piece in range(ALIBI_PIECES):
            part = rest.astype(bf).astype(np.float32)
            key_cols[:, hd, piece] = part
            rest = rest - part
    tiles = np.stack([np.float32(s) * rel for s in slopes2])
    return jnp.asarray(key_cols.reshape(KV_CHUNK, HEADS * HEAD_V), BF16), jnp.asarray(tiles, F32)


def _attn_kernel(par_ref, q_ref, k_ref, v_ref, bt_ref, sg_ref, g_ref, o_ref, vt_ref, qv_ref, acc_ref, sa_ref, sb_ref,
                 *, n_chunks):
    head = pl.program_id(1)
    qi = pl.program_id(2)
    lam = par_ref[0]
    out_scale = par_ref[1]
    slope = par_ref[2 + head]
    q_pos = slope * lax.broadcasted_iota(jnp.int32, (1, TQ), 1).astype(F32)

    @pl.when(qi == 0)
    def _():
        ones_rows = (lax.broadcasted_iota(jnp.int32, (VT_ROWS - HEAD_V, KV_CHUNK), 0) == 0).astype(BF16)
        for c in range(n_chunks):
            vt_ref[c, :HEAD_V] = v_ref[0, c * KV_CHUNK:(c + 1) * KV_CHUNK, :].astype(F32).T.astype(BF16)
            vt_ref[c, HEAD_V:] = ones_rows

    qt = q_ref[0]
    zero = jnp.zeros((HEAD_DIM, TQ), BF16)
    for var in range(3):
        qv_ref[var, :HEAD_DIM] = qt[:HEAD_DIM]
        qv_ref[var, HEAD_DIM:HEAD_V] = zero
        qv_ref[3 + var, :HEAD_DIM] = zero
        qv_ref[3 + var, HEAD_DIM:HEAD_V] = qt[HEAD_DIM:]
        qv_ref[var, HEAD_V:] = sg_ref[var]
        qv_ref[3 + var, HEAD_V:] = sg_ref[var]

    qd = qi // (KV_CHUNK // TQ)
    i0 = qi * TQ

    s_bufs = (sa_ref, sb_ref)

    def scores(t):
        buf = s_bufs[t % 2]
        if t == 0:
            c = qd
            var = 2
            off = jnp.zeros((1, TQ), F32)
            bias = -jnp.abs(bt_ref[0] + slope * (i0 - qd * KV_CHUNK).astype(F32))
        else:
            c = (t - 1) + ((t - 1) >= qd).astype(jnp.int32)
            after = c < qd
            var = jnp.where(after, 0, 1)
            off = -slope * jnp.abs(i0 - c * KV_CHUNK).astype(F32) + jnp.where(after, -q_pos, q_pos)
            bias = None
        k = k_ref[0, pl.ds(pl.multiple_of(c * KV_CHUNK, KV_CHUNK), KV_CHUNK), :]
        smax = []
        for mi in range(2):
            s = jnp.dot(k, qv_ref[3 * mi + var], preferred_element_type=F32)
            if bias is not None:
                s = s + bias
            buf[mi] = s
            smax.append(jnp.max(s, axis=0, keepdims=True) + off)
        return c, off, smax

    def accumulate(t, c, c_off, smax, m):
        buf = s_bufs[t % 2]
        vt = vt_ref[c]
        for mi in range(2):
            m_new = jnp.maximum(m[mi], smax[mi])
            p = jnp.exp2(buf[mi] - (m_new - c_off))
            alpha = jnp.exp2(m[mi] - m_new)
            acc_ref[mi] = alpha * acc_ref[mi] + jnp.dot(vt, p.astype(BF16), preferred_element_type=F32)
            m[mi] = m_new

    acc_ref[...] = jnp.zeros_like(acc_ref)
    m = [jnp.full((1, TQ), -1e30, F32)] * 2
    pending = scores(0)
    for t in range(n_chunks):
        nxt = scores(t + 1) if t + 1 < n_chunks else None
        accumulate(t, *pending, m)
        pending = nxt
    o = (acc_ref[0, :HEAD_V] / acc_ref[0, HEAD_V:HEAD_V + 1]
         - lam * (acc_ref[1, :HEAD_V] / acc_ref[1, HEAD_V:HEAD_V + 1]))
    o = o * lax.rsqrt(jnp.mean(o * o, axis=0, keepdims=True) + NORM_EPS)
    o = o * (g_ref[...] * out_scale)
    o_ref[0] = o.T.astype(BF16)


def _attention(par, qt, k, v, bias_tiles, g_col):
    b, l, _ = v.shape
    n_chunks = l // KV_CHUNK
    signs = np.zeros((3, HEAD_V, TQ), np.float32)
    signs[0, :ALIBI_PIECES] = 1.0
    signs[1, :ALIBI_PIECES] = -1.0
    return pl.pallas_call(
        functools.partial(_attn_kernel, n_chunks=n_chunks),
        out_shape=jax.ShapeDtypeStruct((b, l, ATTN_WIDTH), BF16),
        grid=(b, HEADS, l // TQ),
        in_specs=[
            pl.BlockSpec(memory_space=pltpu.SMEM),
            pl.BlockSpec((1, HEAD_V, TQ), lambda bi, h, i: (bi, h, i)),
            pl.BlockSpec((1, l, 2 * HEAD_V), lambda bi, h, i: (bi, 0, h)),
            pl.BlockSpec((1, l, HEAD_V), lambda bi, h, i: (bi, 0, h)),
            pl.BlockSpec((1, KV_CHUNK, TQ), lambda bi, h, i: (h, 0, 0)),
            pl.BlockSpec((3, HEAD_V, TQ), lambda bi, h, i: (0, 0, 0)),
            pl.BlockSpec((HEAD_V, 1), lambda bi, h, i: (0, 0)),
        ],
        out_specs=pl.BlockSpec((1, TQ, HEAD_V), lambda bi, h, i: (bi, i, h)),
        scratch_shapes=[
            pltpu.VMEM((n_chunks, VT_ROWS, KV_CHUNK), BF16),
            pltpu.VMEM((6, 2 * HEAD_V, TQ), BF16),
            pltpu.VMEM((2, VT_ROWS, TQ), F32),
            pltpu.VMEM((2, KV_CHUNK, TQ), F32),
            pltpu.VMEM((2, KV_CHUNK, TQ), F32),
        ],
        compiler_params=_cparams(("parallel", "parallel", "arbitrary")),
        name="diff_attn",
    )(par, qt, k, v, bias_tiles, jnp.asarray(signs, BF16), g_col)


def _postmix_kernel(x_ref, ys_ref, ya_ref, wglu_ref, bglu_ref, gs_ref, wout_ref, gf_ref, *rest, router):
    if router:
        wr_ref, br_ref, xo_ref, h_ref, gate_ref, ysc_ref = rest
    else:
        xo_ref, h_ref, ysc_ref = rest
    n_chunks = ys_ref.shape[1]
    for j in range(SSM_WIDTH // LANES):
        for s in range(SSM_CHUNK):
            ysc_ref[j, pl.ds(s, n_chunks, stride=SSM_CHUNK), :] = ys_ref[s, :, j * LANES:(j + 1) * LANES].astype(F32)
    y = jnp.concatenate([ysc_ref[j] for j in range(SSM_WIDTH // LANES)], axis=1)
    t = jnp.dot(y.astype(BF16), wglu_ref[...], preferred_element_type=F32) + bglu_ref[...]
    y = y * jax.nn.sigmoid(t)
    y = _rms(y, gs_ref[...]).astype(BF16)
    mix = jnp.dot(y, wout_ref[:SSM_WIDTH, :], preferred_element_type=F32)
    mix = mix + jnp.dot(ya_ref[...], wout_ref[SSM_WIDTH:, :], preferred_element_type=F32)
    x = x_ref[...] + mix
    xo_ref[...] = x
    h = _rms(x, gf_ref[...])
    h_ref[...] = h.astype(h_ref.dtype)
    if router:
        lane = lax.broadcasted_iota(jnp.int32, (h.shape[0], LANES), 1)
        logits = jnp.broadcast_to(br_ref[...], lane.shape)
        for e in range(N_EXPERTS):
            le = jnp.sum(h * wr_ref[e:e + 1, :], axis=-1, keepdims=True)
            logits = jnp.where(lane == e, logits + le, logits)
        big = jnp.int32(LANES)
        m1 = jnp.max(logits, axis=-1, keepdims=True)
        i1 = jnp.min(jnp.where(logits == m1, lane, big), axis=-1, keepdims=True)
        rest_l = jnp.where(lane == i1, -jnp.inf, logits)
        m2 = jnp.max(rest_l, axis=-1, keepdims=True)
        i2 = jnp.min(jnp.where(rest_l == m2, lane, big), axis=-1, keepdims=True)
        e2 = jnp.exp(m2 - m1)
        w1 = 1.0 / (1.0 + e2)
        w2 = e2 * w1
        rec = jnp.where(lane == 0, w1, jnp.where(lane == 1, w2, jnp.where(
            lane == 2, i1.astype(F32), jnp.where(lane == 3, i2.astype(F32), 0.0))))
        gate_ref[...] = rec[:, :N_EXPERTS]


def _postmix(x2, ys, ya, wglu, bglu, gs, wout, gf, router_w, tm):
    n = x2.shape[0]
    router = router_w is not None
    row = lambda c: pl.BlockSpec((tm, c), lambda i: (i, 0))
    full = lambda r, c: pl.BlockSpec((r, c), lambda i: (0, 0))
    steps = pl.BlockSpec((SSM_CHUNK, tm // SSM_CHUNK, SSM_WIDTH), lambda i: (0, i, 0))
    in_specs = [row(D_MODEL), steps, row(512), full(512, 512), full(1, 512), full(1, 512),
                full(D_MODEL, D_MODEL), full(1, D_MODEL)]
    args = [x2, ys, ya, wglu, bglu, gs, wout, gf]
    out_shape = [jax.ShapeDtypeStruct((n, D_MODEL), F32), jax.ShapeDtypeStruct((n, D_MODEL), F32 if router else BF16)]
    out_specs = [row(D_MODEL), row(D_MODEL)]
    if router:
        in_specs += [full(N_EXPERTS, D_MODEL), full(1, LANES)]
        args += list(router_w)
        out_shape.append(jax.ShapeDtypeStruct((n, N_EXPERTS), F32))
        out_specs.append(row(N_EXPERTS))
    return pl.pallas_call(
        functools.partial(_postmix_kernel, router=router),
        out_shape=tuple(out_shape),
        grid=(n // tm,),
        in_specs=in_specs,
        out_specs=tuple(out_specs),
        scratch_shapes=[pltpu.VMEM((SSM_WIDTH // LANES, tm, LANES), F32)],
        compiler_params=_cparams(("parallel",)),
        name="postmix_router" if router else "postmix",
    )(*args)


def _swiglu_partial(h, wg, wu, wd):
    g = jnp.dot(h, wg, preferred_element_type=F32)
    u = jnp.dot(h, wu, preferred_element_type=F32)
    return jnp.dot((jax.nn.silu(g) * u).astype(BF16), wd, preferred_element_type=F32)


def _ffn_kernel(h_ref, x_ref, wg_ref, wu_ref, wd_ref, o_ref):
    @pl.when(pl.program_id(1) == 0)
    def _():
        o_ref[...] = x_ref[...]

    o_ref[...] += _swiglu_partial(h_ref[...], wg_ref[...], wu_ref[...], wd_ref[...])


def _ffn(h, x2, wg, wu, wd, tm, tf):
    n = x2.shape[0]
    ff = wg.shape[1]
    row = lambda c: pl.BlockSpec((tm, c), lambda i, f: (i, 0))
    return pl.pallas_call(
        _ffn_kernel,
        out_shape=jax.ShapeDtypeStruct((n, D_MODEL), F32),
        grid=(n // tm, ff // tf),
        in_specs=[row(D_MODEL), row(D_MODEL),
                  pl.BlockSpec((D_MODEL, tf), lambda i, f: (0, f)),
                  pl.BlockSpec((D_MODEL, tf), lambda i, f: (0, f)),
                  pl.BlockSpec((tf, D_MODEL), lambda i, f: (f, 0))],
        out_specs=row(D_MODEL),
        compiler_params=_cparams(("parallel", "arbitrary")),
        name="ffn_dense",
    )(h, x2, wg, wu, wd)


def _route(rec, tm, tc):
    n = rec.shape[0]
    a = 2 * n
    n_tiles = a // tm + N_EXPERTS
    e = rec[:, 2:4].astype(jnp.int32).reshape(a)
    onehot = (e[:, None] == jnp.arange(N_EXPERTS, dtype=jnp.int32)[None, :]).astype(jnp.int32)
    csum = jnp.cumsum(onehot, axis=0)
    rank = jnp.sum(csum * onehot, axis=1) - 1
    count = csum[-1]
    padded = (count + tm - 1) // tm * tm
    ends = jnp.cumsum(padded)
    pad_start = ends - padded + count
    pad_aligned = (pad_start + SUBLANES - 1) // SUBLANES * SUBLANES
    pad = jnp.concatenate([pad_start, pad_aligned, ends - pad_aligned, ends[-1:] // tm]).astype(jnp.int32)
    pos = jnp.sum(onehot * (ends - padded)[None, :], axis=1) + rank
    n_used = (ends[-1] // tm).astype(jnp.int32)
    tile = jnp.arange(n_tiles, dtype=jnp.int32)
    texp = jnp.sum((tile[:, None] >= (ends // tm)[None, :]).astype(jnp.int32), axis=1)
    texp = jnp.minimum(texp, N_EXPERTS - 1)
    texp = jnp.where(tile < n_used, texp, texp[jnp.maximum(n_used - 1, 0)])
    pos2d = pos.reshape(n // tc, tc, 2).transpose(0, 2, 1).reshape(n // tc, 2 * tc)
    return pos2d, texp, n_used.reshape(1), pad


def _fetch_rows_index(pos_hbm, idx_smem, sem_idx):
    cp = pltpu.make_async_copy(pos_hbm.at[pl.program_id(0)], idx_smem, sem_idx)
    cp.start()
    cp.wait()


def _moe_dispatch_kernel(pad_ref, pos_hbm, h_ref, xs_hbm, idx_smem, zeros_ref, sem_idx, sem, sem_pad):
    @pl.when(pl.program_id(0) == 0)
    def _():
        zeros_ref[...] = jnp.zeros_like(zeros_ref)

        def padding_copies(act):
            zero_rows = lambda off, size: pltpu.make_async_copy(
                zeros_ref.at[pl.ds(0, size)], xs_hbm.at[pl.ds(off, size)], sem_pad)
            for e in range(N_EXPERTS):
                first = pad_ref[e]
                aligned = pad_ref[N_EXPERTS + e]
                length = pad_ref[2 * N_EXPERTS + e]
                for j in range(SUBLANES - 1):
                    @pl.when(first + j < aligned)
                    def _(row=first + j):
                        act(zero_rows(row, 1))

                off = aligned
                for bit in reversed(range(3, MOE_TILE.bit_length() - 1)):
                    size = 1 << bit

                    @pl.when((length & size) != 0)
                    def _(off=off, size=size):
                        act(zero_rows(pl.multiple_of(off, SUBLANES), size))

                    off = off + (length & size)
            n_tiles = xs_hbm.shape[0] // MOE_TILE
            half = zeros_ref.shape[0]
            for t in range(n_tiles - N_EXPERTS, n_tiles):
                @pl.when(t >= pad_ref[3 * N_EXPERTS])
                def _(t=t):
                    for part in range(MOE_TILE // half):
                        act(zero_rows(t * MOE_TILE + part * half, half))

        padding_copies(lambda cp: cp.start())
        padding_copies(lambda cp: cp.wait())

    _fetch_rows_index(pos_hbm, idx_smem, sem_idx)
    tc = h_ref.shape[0]

    def issue(r, carry):
        src = h_ref.at[pl.ds(r, 1)]
        pltpu.make_async_copy(src, xs_hbm.at[pl.ds(idx_smem[r], 1)], sem).start()
        pltpu.make_async_copy(src, xs_hbm.at[pl.ds(idx_smem[tc + r], 1)], sem).start()
        return carry

    lax.fori_loop(0, tc, issue, 0, unroll=8)
    for _ in range(2):
        pltpu.make_async_copy(h_ref, xs_hbm.at[pl.ds(0, tc)], sem).wait()


def _moe_dispatch(pad, pos2d, h, rows):
    n = h.shape[0]
    tc = pos2d.shape[1] // 2
    return pl.pallas_call(
        _moe_dispatch_kernel,
        out_shape=jax.ShapeDtypeStruct((rows, D_MODEL), F32),
        grid_spec=pltpu.PrefetchScalarGridSpec(
            num_scalar_prefetch=1,
            grid=(n // tc,),
            in_specs=[pl.BlockSpec(memory_space=pl.ANY), pl.BlockSpec((tc, D_MODEL), lambda i, pad: (i, 0))],
            out_specs=pl.BlockSpec(memory_space=pl.ANY),
            scratch_shapes=[pltpu.SMEM((2 * tc,), jnp.int32), pltpu.VMEM((MOE_TILE // 2, D_MODEL), F32),
                            pltpu.SemaphoreType.DMA, pltpu.SemaphoreType.DMA, pltpu.SemaphoreType.DMA],
        ),
        compiler_params=_cparams(("arbitrary",)),
        name="moe_dispatch",
    )(pad, pos2d, h)


def _moe_ffn_kernel(texp_ref, nused_ref, xs_ref, wg_ref, wu_ref, wd_ref, o_ref, xb_ref):
    t = pl.program_id(0)
    f = pl.program_id(1)
    last = f == pl.num_programs(1) - 1
    valid = t < nused_ref[0]

    @pl.when(valid)
    def _():
        @pl.when(f == 0)
        def _():
            xb_ref[...] = xs_ref[...].astype(BF16)
            o_ref[...] = jnp.zeros_like(o_ref)

        o_ref[...] += _swiglu_partial(xb_ref[...], wg_ref[0, 0].astype(BF16), wu_ref[0, 0].astype(BF16),
                                      wd_ref[0, 0].astype(BF16))

    @pl.when(jnp.logical_and(jnp.logical_not(valid), last))
    def _():
        o_ref[...] = jnp.zeros_like(o_ref)


def _moe_ffn(texp, n_used, xs, wg, wu, wd, layer, tm, tf):
    rows = xs.shape[0]
    ff = wg.shape[-1]
    nf = ff // tf
    fidx = lambda t, f, nu: jnp.where(t < nu[0], f, nf - 1)
    row = pl.BlockSpec((tm, D_MODEL), lambda t, f, te, nu: (t, 0))
    xs_row = pl.BlockSpec((tm, D_MODEL), lambda t, f, te, nu: (jnp.minimum(t, jnp.maximum(nu[0] - 1, 0)), 0))
    return pl.pallas_call(
        _moe_ffn_kernel,
        out_shape=jax.ShapeDtypeStruct((rows, D_MODEL), F32),
        grid_spec=pltpu.PrefetchScalarGridSpec(
            num_scalar_prefetch=2,
            grid=(rows // tm, nf),
            in_specs=[xs_row,
                      pl.BlockSpec((1, 1, D_MODEL, tf), lambda t, f, te, nu: (layer, te[t], 0, fidx(t, f, nu))),
                      pl.BlockSpec((1, 1, D_MODEL, tf), lambda t, f, te, nu: (layer, te[t], 0, fidx(t, f, nu))),
                      pl.BlockSpec((1, 1, tf, D_MODEL), lambda t, f, te, nu: (layer, te[t], fidx(t, f, nu), 0))],
            out_specs=row,
            scratch_shapes=[pltpu.VMEM((tm, D_MODEL), BF16)],
        ),
        compiler_params=_cparams(("arbitrary", "arbitrary")),
        name="moe_ffn",
    )(texp, n_used, xs, wg, wu, wd)


def _moe_combine_kernel(pos_hbm, ys_hbm, x_ref, rec_ref, gfin_ref, o_ref, idx_smem, buf, sem_idx, sem, *, final):
    _fetch_rows_index(pos_hbm, idx_smem, sem_idx)
    count = buf.shape[0]

    def issue(r, carry):
        pltpu.make_async_copy(ys_hbm.at[pl.ds(idx_smem[r], 1)], buf.at[pl.ds(r, 1)], sem).start()
        return carry

    lax.fori_loop(0, count, issue, 0, unroll=8)
    pltpu.make_async_copy(ys_hbm.at[pl.ds(0, count)], buf, sem).wait()
    tc = x_ref.shape[0]
    rec = rec_ref[...]
    out = x_ref[...] + rec[:, 0:1] * buf[:tc] + rec[:, 1:2] * buf[tc:]
    if final:
        out = _rms(out, gfin_ref[...])
    o_ref[...] = out


def _moe_combine(pos2d, ys, x2, rec, gfin, final):
    n = x2.shape[0]
    tc = pos2d.shape[1] // 2
    row = lambda c: pl.BlockSpec((tc, c), lambda i: (i, 0))
    return pl.pallas_call(
        functools.partial(_moe_combine_kernel, final=final),
        out_shape=jax.ShapeDtypeStruct((n, D_MODEL), F32),
        grid=(n // tc,),
        in_specs=[pl.BlockSpec(memory_space=pl.ANY), pl.BlockSpec(memory_space=pl.ANY), row(D_MODEL),
                  row(N_EXPERTS), pl.BlockSpec((1, D_MODEL), lambda i: (0, 0))],
        out_specs=row(D_MODEL),
        scratch_shapes=[pltpu.SMEM((2 * tc,), jnp.int32), pltpu.VMEM((2 * tc, D_MODEL), F32),
                        pltpu.SemaphoreType.DMA, pltpu.SemaphoreType.DMA],
        compiler_params=_cparams(("arbitrary",)),
        name="moe_combine",
    )(pos2d, ys, x2, rec, gfin)


def _pad_axis(a, axis, size):
    pad = [(0, 0)] * a.ndim
    pad[axis] = (0, size - a.shape[axis])
    return jnp.pad(a, pad)


def kernel(x, g_mix, w_in, ssm_lambda_re, ssm_lambda_im, ssm_log_step, ssm_b_re, ssm_b_im, ssm_c_re, ssm_c_im, ssm_d, w_glu, b_glu, g_ssm_out, lambda_q1, lambda_k1, lambda_q2, lambda_k2, g_subln, w_out, g_ffn, dense_w_gate, dense_w_up, dense_w_down, w_router, b_router, moe_w_gate, moe_w_up, moe_w_down, g_final):
    bsz, seq, _ = x.shape
    n = bsz * seq
    depth = w_in.shape[0]
    n_chunks = seq // SSM_CHUNK
    assert seq % KV_CHUNK == 0 and bsz == SUBLANES and depth % 2 == 0 and (2 * n) % MOE_TILE == 0
    tm = min(512, n)
    tm_ffn = min(1024, n)
    d_ff = dense_w_gate.shape[-1]
    d_ff_pad = -(-d_ff // 256) * 256
    slopes2 = [2.0 ** (-8.0 * (h + 1) / HEADS) * LOG2E for h in range(HEADS)]
    kcols, bias_tiles = _alibi_constants(slopes2)
    pmat, qmat = _regroup_matrices()

    x2 = x.reshape(n, D_MODEL)
    for i in range(depth):
        lambda_init = 0.8 - 0.6 * math.exp(-0.3 * i)
        u, qt, k, v = _inproj(x2, g_mix[i][None, :], w_in[i].astype(BF16), kcols, seq)

        m, w, vv, dec_re, dec_im = _ssm_matrices(
            ssm_lambda_re[i], ssm_lambda_im[i], ssm_log_step[i], ssm_b_re[i], ssm_b_im[i],
            ssm_c_re[i], ssm_c_im[i], ssm_d[i])
        uc = _regroup_in(u, pmat, bsz)
        yc = _ssm(uc, m, w, vv, dec_re, dec_im, n_chunks, bsz)
        ys = _regroup_out(yc, qmat, bsz)

        lam = (jnp.exp(jnp.sum(lambda_q1[i].astype(F32) * lambda_k1[i].astype(F32)))
               - jnp.exp(jnp.sum(lambda_q2[i].astype(F32) * lambda_k2[i].astype(F32))) + lambda_init)
        par = jnp.concatenate([jnp.stack([lam, jnp.asarray(1.0 - lambda_init, F32)]),
                               jnp.asarray(slopes2, F32), jnp.zeros((2,), F32)])
        ya = _attention(par, qt, k.reshape(bsz, seq, 2 * ATTN_WIDTH),
                        v.reshape(bsz, seq, ATTN_WIDTH), bias_tiles, g_subln[i].astype(F32)[:, None])
        ya = ya.reshape(n, ATTN_WIDTH)

        j = i // 2
        if i % 2 == 0:
            router_w = None
        else:
            router_w = (w_router[j].astype(F32).T,
                        jnp.concatenate([b_router[j].astype(F32),
                                         jnp.full((LANES - N_EXPERTS,), -jnp.inf, F32)])[None, :])
        outs = _postmix(x2, ys, ya, w_glu[i].astype(BF16), b_glu[i][None, :], g_ssm_out[i][None, :],
                        w_out[i].astype(BF16), g_ffn[i][None, :], router_w, tm)
        if i % 2 == 0:
            x2, h = outs
            wg = _pad_axis(dense_w_gate[j].astype(BF16), 1, d_ff_pad)
            wu = _pad_axis(dense_w_up[j].astype(BF16), 1, d_ff_pad)
            wd = _pad_axis(dense_w_down[j].astype(BF16), 0, d_ff_pad)
            x2 = _ffn(h, x2, wg, wu, wd, tm, d_ff_pad // 2)
        else:
            x2, h, rec = outs
            pos2d, texp, n_used, pad = _route(rec, MOE_TILE, min(MOE_TOKEN_TILE, n))
            xs = _moe_dispatch(pad, pos2d, h, texp.shape[0] * MOE_TILE)
            ys = _moe_ffn(texp, n_used, xs, moe_w_gate, moe_w_up, moe_w_down, j, MOE_TILE, 512)
            x2 = _moe_combine(pos2d, ys, x2, rec, g_final[None, :], i == depth - 1)
    return x2.reshape(bsz, seq, D_MODEL)
```

```python
import functools
import math

import numpy as np
import jax
import jax.numpy as jnp
from jax import lax
from jax.experimental import pallas as pl
from jax.experimental.pallas import tpu as pltpu

F32 = jnp.float32
BF16 = jnp.bfloat16

D_MODEL = 1024
SSM_WIDTH = 512
SSM_GROUP = 16
SSM_GROUPS = 32
SSM_STATE = 64
ATTN_WIDTH = 512
HEAD_DIM = 64
HEADS = 4
HEAD_V = 2 * HEAD_DIM
N_EXPERTS = 8
NORM_EPS = 1e-6

LANES = 128
SUBLANES = 8
KV_CHUNK = 512
TQ = 512
VT_ROWS = HEAD_V + 16
ALIBI_PIECES = 3
LOG2E = math.log2(math.e)
MOE_TILE = 1024
MOE_TOKEN_TILE = 1024
SSM_CHUNK = 16
CHUNK_COLS = SSM_CHUNK * SSM_GROUP
GROUP_BLOCK = LANES // SSM_GROUP
STEP_PAIRS = SSM_CHUNK // 2
VMEM_LIMIT = 56 * 1024 * 1024


def _cparams(sem):
    return pltpu.CompilerParams(dimension_semantics=sem, vmem_limit_bytes=VMEM_LIMIT)


def _rms(x, g):
    return x * lax.rsqrt(jnp.mean(x * x, axis=-1, keepdims=True) + NORM_EPS) * g


def _inproj_kernel(x_ref, g_ref, w_ref, kcols_ref, u_ref, q_ref, k_ref, v_ref, us_ref):
    h = _rms(x_ref[...], g_ref[...]).astype(BF16)
    proj = lambda n: jnp.dot(h, w_ref[:, n * 512:(n + 1) * 512], preferred_element_type=F32)
    z = proj(0)
    n_chunks = us_ref.shape[1] // SSM_CHUNK
    for j in range(SSM_WIDTH // LANES):
        us_ref[j] = z[:, j * LANES:(j + 1) * LANES]
        for s in range(SSM_CHUNK):
            u_ref[s, :, j * LANES:(j + 1) * LANES] = us_ref[j, pl.ds(s, n_chunks, stride=SSM_CHUNK), :].astype(BF16)
    q_ref[...] = (proj(1) * (HEAD_DIM ** -0.5 * LOG2E)).astype(BF16)
    k = proj(2).astype(BF16)
    for hd in range(HEADS):
        k_ref[:, 2 * hd * HEAD_V:(2 * hd + 1) * HEAD_V] = k[:, hd * HEAD_V:(hd + 1) * HEAD_V]
        k_ref[:, (2 * hd + 1) * HEAD_V:(2 * hd + 2) * HEAD_V] = kcols_ref[:, hd * HEAD_V:(hd + 1) * HEAD_V]
    v_ref[...] = proj(3).astype(BF16)


def _inproj(x2, g, w_bf, kcols):
    n = x2.shape[0]
    tm = KV_CHUNK
    out = jax.ShapeDtypeStruct((n, 512), BF16)
    row = lambda c: pl.BlockSpec((tm, c), lambda i: (i, 0))
    return pl.pallas_call(
        _inproj_kernel,
        out_shape=(jax.ShapeDtypeStruct((SSM_CHUNK, n // SSM_CHUNK, SSM_WIDTH), BF16), out,
                   jax.ShapeDtypeStruct((n, 2 * ATTN_WIDTH), BF16), out),
        grid=(n // tm,),
        in_specs=[
            row(D_MODEL),
            pl.BlockSpec((1, D_MODEL), lambda i: (0, 0)),
            pl.BlockSpec((D_MODEL, 2048), lambda i: (0, 0)),
            pl.BlockSpec((tm, ATTN_WIDTH), lambda i: (0, 0)),
        ],
        out_specs=(pl.BlockSpec((SSM_CHUNK, tm // SSM_CHUNK, SSM_WIDTH), lambda i: (0, i, 0)),
                   row(512), row(2 * ATTN_WIDTH), row(512)),
        scratch_shapes=[pltpu.VMEM((SSM_WIDTH // LANES, tm, LANES), F32)],
        compiler_params=_cparams(("parallel",)),
        name="inproj",
    )(x2, g, w_bf, kcols)


def _ssm_matrices(lam_re, lam_im, log_step, b_re, b_im, c_re, c_im, d_skip):
    t = SSM_CHUNK
    tau = jnp.arange(t + 1, dtype=F32)
    ks, ws, es, decs = [], [], [], []
    for d in range(2):
        lam = lax.complex(lam_re[d].astype(F32), lam_im[d].astype(F32))
        step = jnp.exp(log_step[d].astype(F32))[:, None]
        ls = lam * step
        lam_bar = jnp.exp(ls)
        pw = jnp.exp(ls[:, None, :] * tau[None, :, None])
        b_bar = ((lam_bar - 1.0) / lam)[:, :, None] * lax.complex(b_re[d].astype(F32), b_im[d].astype(F32))
        c = lax.complex(c_re[d].astype(F32), c_im[d].astype(F32))
        ks.append(jnp.real(jnp.einsum('gop,gtp,gpi->gtoi', c, pw[:, :t], b_bar)))
        if d == 0:
            wpow = pw[:, :t][:, ::-1]
            epow = pw[:, 1:t + 1]
        else:
            wpow = pw[:, :t]
            epow = pw[:, 1:t + 1][:, ::-1]
        ws.append(jnp.einsum('gsp,gpi->gsip', wpow, b_bar).reshape(SSM_GROUPS, CHUNK_COLS, SSM_STATE))
        es.append(jnp.einsum('gop,gtp->gpto', c, epow).reshape(SSM_GROUPS, SSM_STATE, CHUNK_COLS))
        decs.append(pw[:, t])
    kf, kb = ks
    s_idx = jnp.arange(t)[:, None]
    t_idx = jnp.arange(t)[None, :]
    lag_f = t_idx - s_idx
    lag_b = s_idx - t_idx
    lags = jnp.arange(t)[None, None, :]
    mf = jnp.einsum('stk,gkoi->gstoi', (lag_f[:, :, None] == lags).astype(F32), kf)
    mb = jnp.einsum('stk,gkoi->gstoi', (lag_b[:, :, None] == lags).astype(F32), kb)
    dmat = (jnp.eye(t, dtype=F32)[None, :, :, None, None]
            * (jnp.eye(SSM_GROUP, dtype=F32)[None, None, None] * d_skip.astype(F32).reshape(SSM_GROUPS, 1, 1, 1, SSM_GROUP)))
    m = (mf + mb + dmat).transpose(0, 1, 4, 2, 3).reshape(SSM_GROUPS, CHUNK_COLS, CHUNK_COLS)
    w = jnp.concatenate([jnp.real(ws[0]), jnp.real(ws[1]), jnp.imag(ws[0]), jnp.imag(ws[1])], axis=-1)
    v = jnp.concatenate([jnp.real(es[0]), jnp.real(es[1]), -jnp.imag(es[0]), -jnp.imag(es[1])], axis=1)
    dec_re = jnp.concatenate([jnp.real(decs[0]), jnp.real(decs[1])], axis=-1)[:, None, :]
    dec_im = jnp.concatenate([jnp.imag(decs[0]), jnp.imag(decs[1])], axis=-1)[:, None, :]
    return m.astype(BF16), w.astype(BF16), v.astype(BF16), dec_re, dec_im


def _ssm_kernel(u_ref, m_ref, w_ref, v_ref, ar_ref, ai_ref, y_ref, s_ref, hf_ref, hb_ref, *, n_chunks, rows):
    u = u_ref[0]
    s = jnp.dot(u, w_ref[0], preferred_element_type=F32)
    pitch = n_chunks + 1
    for b in range(rows):
        s_ref[0, b * pitch:b * pitch + n_chunks, :] = s[b * n_chunks:(b + 1) * n_chunks, :LANES]
        s_ref[1, b * pitch:b * pitch + n_chunks, :] = s[b * n_chunks:(b + 1) * n_chunks, LANES:]
    ar = jnp.broadcast_to(ar_ref[0], (rows, LANES))
    ai = jnp.broadcast_to(ai_ref[0], (rows, LANES))

    def body(c, carry):
        fr, fi, br, bi = carry
        at_f = pl.ds(c, rows, stride=pitch)
        at_b = pl.ds(n_chunks - 1 - c, rows, stride=pitch)
        hf_ref[0, at_f, :] = fr
        hf_ref[1, at_f, :] = fi
        hb_ref[0, at_b, :] = br
        hb_ref[1, at_b, :] = bi
        nfr = ar * fr - ai * fi + s_ref[0, at_f, :]
        nfi = ai * fr + ar * fi + s_ref[1, at_f, :]
        nbr = ar * br - ai * bi + s_ref[0, at_b, :]
        nbi = ai * br + ar * bi + s_ref[1, at_b, :]
        return nfr, nfi, nbr, nbi

    z = jnp.zeros((rows, LANES), F32)
    lax.fori_loop(0, n_chunks, body, (z, z, z, z), unroll=4)
    is_fwd = lax.broadcasted_iota(jnp.int32, (1, LANES), 1) < SSM_STATE
    seq_rows = lambda ref, part: jnp.concatenate(
        [ref[part, b * pitch:b * pitch + n_chunks, :] for b in range(rows)], axis=0)
    hcat = jnp.concatenate([jnp.where(is_fwd, seq_rows(hf_ref, 0), seq_rows(hb_ref, 0)),
                            jnp.where(is_fwd, seq_rows(hf_ref, 1), seq_rows(hb_ref, 1))], axis=1).astype(BF16)
    y = jnp.dot(u, m_ref[0], preferred_element_type=F32)
    y = y + jnp.dot(hcat, v_ref[0], preferred_element_type=F32)
    y_ref[0] = jax.nn.gelu(y).astype(BF16)


def _ssm(uc, m, w, v, dec_re, dec_im, n_chunks, rows):
    g, r, _ = uc.shape
    mat = pl.BlockSpec((1, CHUNK_COLS, CHUNK_COLS), lambda i: (i, 0, 0))
    dec = pl.BlockSpec((1, 1, LANES), lambda i: (i, 0, 0))
    return pl.pallas_call(
        functools.partial(_ssm_kernel, n_chunks=n_chunks, rows=rows),
        out_shape=jax.ShapeDtypeStruct((g, r, CHUNK_COLS), BF16),
        grid=(g,),
        in_specs=[pl.BlockSpec((1, r, CHUNK_COLS), lambda i: (i, 0, 0)), mat, mat, mat, dec, dec],
        out_specs=pl.BlockSpec((1, r, CHUNK_COLS), lambda i: (i, 0, 0)),
        scratch_shapes=[pltpu.VMEM((2, rows * (n_chunks + 1), LANES), F32)] * 3,
        compiler_params=_cparams(("parallel",)),
        name="ssm",
    )(uc, m, w, v, dec_re, dec_im)


def _regroup_matrices():
    p = np.zeros((STEP_PAIRS, 2, GROUP_BLOCK, SSM_GROUP, GROUP_BLOCK, SSM_CHUNK, SSM_GROUP), np.float32)
    gl = np.arange(GROUP_BLOCK)[:, None]
    h = np.arange(SSM_GROUP)[None, :]
    for sp in range(STEP_PAIRS):
        for half in range(2):
            p[sp, half, gl, h, gl, sp + half * STEP_PAIRS, h] = 1.0
    p = p.reshape(STEP_PAIRS, 2 * LANES, GROUP_BLOCK, CHUNK_COLS)
    q = p.transpose(0, 2, 3, 1)
    return (jnp.asarray(p.reshape(STEP_PAIRS, 2 * LANES, GROUP_BLOCK * CHUNK_COLS), BF16), jnp.asarray(q, BF16))


def _regroup_in_kernel(us_ref, p_ref, o_ref):
    pairs = [jnp.concatenate([us_ref[sp], us_ref[sp + STEP_PAIRS]], axis=1) for sp in range(STEP_PAIRS)]
    for gl in range(GROUP_BLOCK):
        acc = None
        for sp in range(STEP_PAIRS):
            d = jnp.dot(pairs[sp], p_ref[sp, :, gl * CHUNK_COLS:(gl + 1) * CHUNK_COLS], preferred_element_type=F32)
            acc = d if acc is None else acc + d
        o_ref[gl] = acc.astype(BF16)


def _regroup_out_kernel(yc_ref, q_ref, o_ref):
    for tp in range(STEP_PAIRS):
        acc = None
        for gl in range(GROUP_BLOCK):
            d = jnp.dot(yc_ref[gl], q_ref[tp, gl], preferred_element_type=F32)
            acc = d if acc is None else acc + d
        o_ref[tp] = acc[:, :LANES].astype(BF16)
        o_ref[tp + STEP_PAIRS] = acc[:, LANES:].astype(BF16)


def _regroup_specs(c):
    steps = lambda f: pl.BlockSpec((SSM_CHUNK, c, LANES), f)
    groups = lambda f: pl.BlockSpec((GROUP_BLOCK, c, CHUNK_COLS), f)
    return steps(lambda gb, b: (0, b, gb)), groups(lambda gb, b: (gb, b, 0))


def _regroup_in(us, pmat, bsz):
    c = us.shape[1] // bsz
    steps, groups = _regroup_specs(c)
    return pl.pallas_call(
        _regroup_in_kernel,
        out_shape=jax.ShapeDtypeStruct((SSM_GROUPS, bsz * c, CHUNK_COLS), BF16),
        grid=(SSM_GROUPS // GROUP_BLOCK, bsz),
        in_specs=[steps, pl.BlockSpec(pmat.shape, lambda gb, b: (0, 0, 0))],
        out_specs=groups,
        compiler_params=_cparams(("parallel", "parallel")),
        name="regroup_in",
    )(us, pmat)


def _regroup_out(yc, qmat, bsz):
    c = yc.shape[1] // bsz
    steps, groups = _regroup_specs(c)
    return pl.pallas_call(
        _regroup_out_kernel,
        out_shape=jax.ShapeDtypeStruct((SSM_CHUNK, c * bsz, SSM_WIDTH), BF16),
        grid=(SSM_GROUPS // GROUP_BLOCK, bsz),
        in_specs=[groups, pl.BlockSpec(qmat.shape, lambda gb, b: (0, 0, 0, 0))],
        out_specs=steps,
        compiler_params=_cparams(("parallel", "parallel")),
        name="regroup_out",
    )(yc, qmat)


def _alibi_constants(slopes2):
    bf = jnp.bfloat16
    jj = np.arange(KV_CHUNK, dtype=np.float32)
    rel = np.arange(TQ, dtype=np.float32)[None, :] - jj[:, None]
    key_cols = np.zeros((KV_CHUNK, HEADS, HEAD_V), np.float32)
    for hd, s in enumerate(slopes2):
        rest = np.float32(s) * jj
        for piece in range(ALIBI_PIECES):
            part = rest.astype(bf).astype(np.float32)
            key_cols[:, hd, piece] = part
            rest = rest - part
    tiles = np.stack([np.float32(s) * rel for s in slopes2])
    return jnp.asarray(key_cols.reshape(KV_CHUNK, HEADS * HEAD_V), BF16), jnp.asarray(tiles, F32)


def _attn_kernel(par_ref, q_ref, k_ref, v_ref, bt_ref, sg_ref, g_ref, o_ref, vt_ref, qv_ref, acc_ref, sa_ref, sb_ref,
                 *, n_chunks):
    head = pl.program_id(1)
    qi = pl.program_id(2)
    lam = par_ref[0]
    out_scale = par_ref[1]
    slope = par_ref[2 + head]
    q_pos = slope * lax.broadcasted_iota(jnp.int32, (1, TQ), 1).astype(F32)

    @pl.when(qi == 0)
    def _():
        ones_rows = (lax.broadcasted_iota(jnp.int32, (VT_ROWS - HEAD_V, KV_CHUNK), 0) == 0).astype(BF16)
        for c in range(n_chunks):
            vt_ref[c, :HEAD_V] = v_ref[0, c * KV_CHUNK:(c + 1) * KV_CHUNK, :].astype(F32).T.astype(BF16)
            vt_ref[c, HEAD_V:] = ones_rows

    qt = q_ref[0].astype(F32).T.astype(BF16)
    zero = jnp.zeros((HEAD_DIM, TQ), BF16)
    for var in range(3):
        qv_ref[var, :HEAD_DIM] = qt[:HEAD_DIM]
        qv_ref[var, HEAD_DIM:HEAD_V] = zero
        qv_ref[3 + var, :HEAD_DIM] = zero
        qv_ref[3 + var, HEAD_DIM:HEAD_V] = qt[HEAD_DIM:]
        qv_ref[var, HEAD_V:] = sg_ref[var]
        qv_ref[3 + var, HEAD_V:] = sg_ref[var]

    qd = qi // (KV_CHUNK // TQ)
    i0 = qi * TQ

    s_bufs = (sa_ref, sb_ref)

    def scores(t):
        buf = s_bufs[t % 2]
        if t == 0:
            c = qd
            var = 2
            off = jnp.zeros((1, TQ), F32)
            bias = -jnp.abs(bt_ref[0] + slope * (i0 - qd * KV_CHUNK).astype(F32))
        else:
            c = (t - 1) + ((t - 1) >= qd).astype(jnp.int32)
            after = c < qd
            var = jnp.where(after, 0, 1)
            off = -slope * jnp.abs(i0 - c * KV_CHUNK).astype(F32) + jnp.where(after, -q_pos, q_pos)
            bias = None
        k = k_ref[0, pl.ds(pl.multiple_of(c * KV_CHUNK, KV_CHUNK), KV_CHUNK), :]
        smax = []
        for mi in range(2):
            s = jnp.dot(k, qv_ref[3 * mi + var], preferred_element_type=F32)
            if bias is not None:
                s = s + bias
            buf[mi] = s
            smax.append(jnp.max(s, axis=0, keepdims=True) + off)
        return c, off, smax

    def accumulate(t, c, c_off, smax, m):
        buf = s_bufs[t % 2]
        vt = vt_ref[c]
        for mi in range(2):
            m_new = jnp.maximum(m[mi], smax[mi])
            p = jnp.exp2(buf[mi] - (m_new - c_off))
            alpha = jnp.exp2(m[mi] - m_new)
            acc_ref[mi] = alpha * acc_ref[mi] + jnp.dot(vt, p.astype(BF16), preferred_element_type=F32)
            m[mi] = m_new

    acc_ref[...] = jnp.zeros_like(acc_ref)
    m = [jnp.full((1, TQ), -1e30, F32)] * 2
    pending = scores(0)
    for t in range(n_chunks):
        nxt = scores(t + 1) if t + 1 < n_chunks else None
        accumulate(t, *pending, m)
        pending = nxt
    o = (acc_ref[0, :HEAD_V] / acc_ref[0, HEAD_V:HEAD_V + 1]
         - lam * (acc_ref[1, :HEAD_V] / acc_ref[1, HEAD_V:HEAD_V + 1]))
    o = o * lax.rsqrt(jnp.mean(o * o, axis=0, keepdims=True) + NORM_EPS)
    o = o * (g_ref[...] * out_scale)
    o_ref[0] = o.T.astype(BF16)


def _attention(par, q, k, v, bias_tiles, g_col):
    b, l, _ = v.shape
    n_chunks = l // KV_CHUNK
    signs = np.zeros((3, HEAD_V, TQ), np.float32)
    signs[0, :ALIBI_PIECES] = 1.0
    signs[1, :ALIBI_PIECES] = -1.0
    return pl.pallas_call(
        functools.partial(_attn_kernel, n_chunks=n_chunks),
        out_shape=jax.ShapeDtypeStruct((b, l, ATTN_WIDTH), BF16),
        grid=(b, HEADS, l // TQ),
        in_specs=[
            pl.BlockSpec(memory_space=pltpu.SMEM),
            pl.BlockSpec((1, TQ, HEAD_V), lambda bi, h, i: (bi, i, h)),
            pl.BlockSpec((1, l, 2 * HEAD_V), lambda bi, h, i: (bi, 0, h)),
            pl.BlockSpec((1, l, HEAD_V), lambda bi, h, i: (bi, 0, h)),
            pl.BlockSpec((1, KV_CHUNK, TQ), lambda bi, h, i: (h, 0, 0)),
            pl.BlockSpec((3, HEAD_V, TQ), lambda bi, h, i: (0, 0, 0)),
            pl.BlockSpec((HEAD_V, 1), lambda bi, h, i: (0, 0)),
        ],
        out_specs=pl.BlockSpec((1, TQ, HEAD_V), lambda bi, h, i: (bi, i, h)),
        scratch_shapes=[
            pltpu.VMEM((n_chunks, VT_ROWS, KV_CHUNK), BF16),
            pltpu.VMEM((6, 2 * HEAD_V, TQ), BF16),
            pltpu.VMEM((2, VT_ROWS, TQ), F32),
            pltpu.VMEM((2, KV_CHUNK, TQ), F32),
            pltpu.VMEM((2, KV_CHUNK, TQ), F32),
        ],
        compiler_params=_cparams(("parallel", "parallel", "arbitrary")),
        name="diff_attn",
    )(par, q, k, v, bias_tiles, jnp.asarray(signs, BF16), g_col)


def _postmix_kernel(x_ref, ys_ref, ya_ref, wglu_ref, bglu_ref, gs_ref, wout_ref, gf_ref, *rest, router):
    if router:
        wr_ref, br_ref, xo_ref, h_ref, gate_ref, ysc_ref = rest
    else:
        xo_ref, h_ref, ysc_ref = rest
    n_chunks = ys_ref.shape[1]
    for j in range(SSM_WIDTH // LANES):
        for s in range(SSM_CHUNK):
            ysc_ref[j, pl.ds(s, n_chunks, stride=SSM_CHUNK), :] = ys_ref[s, :, j * LANES:(j + 1) * LANES].astype(F32)
    y = jnp.concatenate([ysc_ref[j] for j in range(SSM_WIDTH // LANES)], axis=1)
    t = jnp.dot(y.astype(BF16), wglu_ref[...], preferred_element_type=F32) + bglu_ref[...]
    y = y * jax.nn.sigmoid(t)
    y = _rms(y, gs_ref[...]).astype(BF16)
    mix = jnp.dot(y, wout_ref[:SSM_WIDTH, :], preferred_element_type=F32)
    mix = mix + jnp.dot(ya_ref[...], wout_ref[SSM_WIDTH:, :], preferred_element_type=F32)
    x = x_ref[...] + mix
    xo_ref[...] = x
    h = _rms(x, gf_ref[...])
    h_ref[...] = h.astype(h_ref.dtype)
    if router:
        lane = lax.broadcasted_iota(jnp.int32, (h.shape[0], LANES), 1)
        logits = jnp.broadcast_to(br_ref[...], lane.shape)
        for e in range(N_EXPERTS):
            le = jnp.sum(h * wr_ref[e:e + 1, :], axis=-1, keepdims=True)
            logits = jnp.where(lane == e, logits + le, logits)
        big = jnp.int32(LANES)
        m1 = jnp.max(logits, axis=-1, keepdims=True)
        i1 = jnp.min(jnp.where(logits == m1, lane, big), axis=-1, keepdims=True)
        rest_l = jnp.where(lane == i1, -jnp.inf, logits)
        m2 = jnp.max(rest_l, axis=-1, keepdims=True)
        i2 = jnp.min(jnp.where(rest_l == m2, lane, big), axis=-1, keepdims=True)
        e2 = jnp.exp(m2 - m1)
        w1 = 1.0 / (1.0 + e2)
        w2 = e2 * w1
        rec = jnp.where(lane == 0, w1, jnp.where(lane == 1, w2, jnp.where(
            lane == 2, i1.astype(F32), jnp.where(lane == 3, i2.astype(F32), 0.0))))
        gate_ref[...] = rec[:, :N_EXPERTS]


def _postmix(x2, ys, ya, wglu, bglu, gs, wout, gf, router_w, tm):
    n = x2.shape[0]
    router = router_w is not None
    row = lambda c: pl.BlockSpec((tm, c), lambda i: (i, 0))
    full = lambda r, c: pl.BlockSpec((r, c), lambda i: (0, 0))
    steps = pl.BlockSpec((SSM_CHUNK, tm // SSM_CHUNK, SSM_WIDTH), lambda i: (0, i, 0))
    in_specs = [row(D_MODEL), steps, row(512), full(512, 512), full(1, 512), full(1, 512),
                full(D_MODEL, D_MODEL), full(1, D_MODEL)]
    args = [x2, ys, ya, wglu, bglu, gs, wout, gf]
    out_shape = [jax.ShapeDtypeStruct((n, D_MODEL), F32), jax.ShapeDtypeStruct((n, D_MODEL), F32 if router else BF16)]
    out_specs = [row(D_MODEL), row(D_MODEL)]
    if router:
        in_specs += [full(N_EXPERTS, D_MODEL), full(1, LANES)]
        args += list(router_w)
        out_shape.append(jax.ShapeDtypeStruct((n, N_EXPERTS), F32))
        out_specs.append(row(N_EXPERTS))
    return pl.pallas_call(
        functools.partial(_postmix_kernel, router=router),
        out_shape=tuple(out_shape),
        grid=(n // tm,),
        in_specs=in_specs,
        out_specs=tuple(out_specs),
        scratch_shapes=[pltpu.VMEM((SSM_WIDTH // LANES, tm, LANES), F32)],
        compiler_params=_cparams(("parallel",)),
        name="postmix_router" if router else "postmix",
    )(*args)


def _swiglu_partial(h, wg, wu, wd):
    g = jnp.dot(h, wg, preferred_element_type=F32)
    u = jnp.dot(h, wu, preferred_element_type=F32)
    return jnp.dot((jax.nn.silu(g) * u).astype(BF16), wd, preferred_element_type=F32)


def _ffn_kernel(h_ref, x_ref, wg_ref, wu_ref, wd_ref, o_ref):
    @pl.when(pl.program_id(1) == 0)
    def _():
        o_ref[...] = x_ref[...]

    o_ref[...] += _swiglu_partial(h_ref[...], wg_ref[...], wu_ref[...], wd_ref[...])


def _ffn(h, x2, wg, wu, wd, tm, tf):
    n = x2.shape[0]
    ff = wg.shape[1]
    row = lambda c: pl.BlockSpec((tm, c), lambda i, f: (i, 0))
    return pl.pallas_call(
        _ffn_kernel,
        out_shape=jax.ShapeDtypeStruct((n, D_MODEL), F32),
        grid=(n // tm, ff // tf),
        in_specs=[row(D_MODEL), row(D_MODEL),
                  pl.BlockSpec((D_MODEL, tf), lambda i, f: (0, f)),
                  pl.BlockSpec((D_MODEL, tf), lambda i, f: (0, f)),
                  pl.BlockSpec((tf, D_MODEL), lambda i, f: (f, 0))],
        out_specs=row(D_MODEL),
        compiler_params=_cparams(("parallel", "arbitrary")),
        name="ffn_dense",
    )(h, x2, wg, wu, wd)


def _route(rec, tm, tc):
    n = rec.shape[0]
    a = 2 * n
    n_tiles = a // tm + N_EXPERTS
    e = rec[:, 2:4].astype(jnp.int32).reshape(a)
    onehot = (e[:, None] == jnp.arange(N_EXPERTS, dtype=jnp.int32)[None, :]).astype(jnp.int32)
    csum = jnp.cumsum(onehot, axis=0)
    rank = jnp.sum(csum * onehot, axis=1) - 1
    count = csum[-1]
    padded = (count + tm - 1) // tm * tm
    ends = jnp.cumsum(padded)
    pad_start = ends - padded + count
    pad_aligned = (pad_start + SUBLANES - 1) // SUBLANES * SUBLANES
    pad = jnp.concatenate([pad_start, pad_aligned, ends - pad_aligned, ends[-1:] // tm]).astype(jnp.int32)
    pos = jnp.sum(onehot * (ends - padded)[None, :], axis=1) + rank
    n_used = (ends[-1] // tm).astype(jnp.int32)
    tile = jnp.arange(n_tiles, dtype=jnp.int32)
    texp = jnp.sum((tile[:, None] >= (ends // tm)[None, :]).astype(jnp.int32), axis=1)
    texp = jnp.minimum(texp, N_EXPERTS - 1)
    texp = jnp.where(tile < n_used, texp, texp[jnp.maximum(n_used - 1, 0)])
    pos2d = pos.reshape(n // tc, tc, 2).transpose(0, 2, 1).reshape(n // tc, 2 * tc)
    return pos2d, texp, n_used.reshape(1), pad


def _fetch_rows_index(pos_hbm, idx_smem, sem_idx):
    cp = pltpu.make_async_copy(pos_hbm.at[pl.program_id(0)], idx_smem, sem_idx)
    cp.start()
    cp.wait()


def _moe_dispatch_kernel(pad_ref, pos_hbm, h_ref, xs_hbm, idx_smem, zeros_ref, sem_idx, sem, sem_pad):
    @pl.when(pl.program_id(0) == 0)
    def _():
        zeros_ref[...] = jnp.zeros_like(zeros_ref)

        def padding_copies(act):
            zero_rows = lambda off, size: pltpu.make_async_copy(
                zeros_ref.at[pl.ds(0, size)], xs_hbm.at[pl.ds(off, size)], sem_pad)
            for e in range(N_EXPERTS):
                first = pad_ref[e]
                aligned = pad_ref[N_EXPERTS + e]
                length = pad_ref[2 * N_EXPERTS + e]
                for j in range(SUBLANES - 1):
                    @pl.when(first + j < aligned)
                    def _(row=first + j):
                        act(zero_rows(row, 1))

                off = aligned
                for bit in reversed(range(3, MOE_TILE.bit_length() - 1)):
                    size = 1 << bit

                    @pl.when((length & size) != 0)
                    def _(off=off, size=size):
                        act(zero_rows(pl.multiple_of(off, SUBLANES), size))

                    off = off + (length & size)
            n_tiles = xs_hbm.shape[0] // MOE_TILE
            half = zeros_ref.shape[0]
            for t in range(n_tiles - N_EXPERTS, n_tiles):
                @pl.when(t >= pad_ref[3 * N_EXPERTS])
                def _(t=t):
                    for part in range(MOE_TILE // half):
                        act(zero_rows(t * MOE_TILE + part * half, half))

        padding_copies(lambda cp: cp.start())
        padding_copies(lambda cp: cp.wait())

    _fetch_rows_index(pos_hbm, idx_smem, sem_idx)
    tc = h_ref.shape[0]

    def issue(r, carry):
        src = h_ref.at[pl.ds(r, 1)]
        pltpu.make_async_copy(src, xs_hbm.at[pl.ds(idx_smem[r], 1)], sem).start()
        pltpu.make_async_copy(src, xs_hbm.at[pl.ds(idx_smem[tc + r], 1)], sem).start()
        return carry

    lax.fori_loop(0, tc, issue, 0, unroll=8)
    for _ in range(2):
        pltpu.make_async_copy(h_ref, xs_hbm.at[pl.ds(0, tc)], sem).wait()


def _moe_dispatch(pad, pos2d, h, rows):
    n = h.shape[0]
    tc = pos2d.shape[1] // 2
    return pl.pallas_call(
        _moe_dispatch_kernel,
        out_shape=jax.ShapeDtypeStruct((rows, D_MODEL), F32),
        grid_spec=pltpu.PrefetchScalarGridSpec(
            num_scalar_prefetch=1,
            grid=(n // tc,),
            in_specs=[pl.BlockSpec(memory_space=pl.ANY), pl.BlockSpec((tc, D_MODEL), lambda i, pad: (i, 0))],
            out_specs=pl.BlockSpec(memory_space=pl.ANY),
            scratch_shapes=[pltpu.SMEM((2 * tc,), jnp.int32), pltpu.VMEM((MOE_TILE // 2, D_MODEL), F32),
                            pltpu.SemaphoreType.DMA, pltpu.SemaphoreType.DMA, pltpu.SemaphoreType.DMA],
        ),
        compiler_params=_cparams(("arbitrary",)),
        name="moe_dispatch",
    )(pad, pos2d, h)


def _moe_ffn_kernel(texp_ref, nused_ref, xs_ref, wg_ref, wu_ref, wd_ref, o_ref, xb_ref):
    t = pl.program_id(0)
    f = pl.program_id(1)
    last = f == pl.num_programs(1) - 1
    valid = t < nused_ref[0]

    @pl.when(valid)
    def _():
        @pl.when(f == 0)
        def _():
            xb_ref[...] = xs_ref[...].astype(BF16)
            o_ref[...] = jnp.zeros_like(o_ref)

        o_ref[...] += _swiglu_partial(xb_ref[...], wg_ref[0, 0].astype(BF16), wu_ref[0, 0].astype(BF16),
                                      wd_ref[0, 0].astype(BF16))

    @pl.when(jnp.logical_and(jnp.logical_not(valid), last))
    def _():
        o_ref[...] = jnp.zeros_like(o_ref)


def _moe_ffn(texp, n_used, xs, wg, wu, wd, layer, tm, tf):
    rows = xs.shape[0]
    ff = wg.shape[-1]
    nf = ff // tf
    fidx = lambda t, f, nu: jnp.where(t < nu[0], f, nf - 1)
    row = pl.BlockSpec((tm, D_MODEL), lambda t, f, te, nu: (t, 0))
    xs_row = pl.BlockSpec((tm, D_MODEL), lambda t, f, te, nu: (jnp.minimum(t, jnp.maximum(nu[0] - 1, 0)), 0))
    return pl.pallas_call(
        _moe_ffn_kernel,
        out_shape=jax.ShapeDtypeStruct((rows, D_MODEL), F32),
        grid_spec=pltpu.PrefetchScalarGridSpec(
            num_scalar_prefetch=2,
            grid=(rows // tm, nf),
            in_specs=[xs_row,
                      pl.BlockSpec((1, 1, D_MODEL, tf), lambda t, f, te, nu: (layer, te[t], 0, fidx(t, f, nu))),
                      pl.BlockSpec((1, 1, D_MODEL, tf), lambda t, f, te, nu: (layer, te[t], 0, fidx(t, f, nu))),
                      pl.BlockSpec((1, 1, tf, D_MODEL), lambda t, f, te, nu: (layer, te[t], fidx(t, f, nu), 0))],
            out_specs=row,
            scratch_shapes=[pltpu.VMEM((tm, D_MODEL), BF16)],
        ),
        compiler_params=_cparams(("arbitrary", "arbitrary")),
        name="moe_ffn",
    )(texp, n_used, xs, wg, wu, wd)


def _moe_combine_kernel(pos_hbm, ys_hbm, x_ref, rec_ref, gfin_ref, o_ref, idx_smem, buf, sem_idx, sem, *, final):
    _fetch_rows_index(pos_hbm, idx_smem, sem_idx)
    count = buf.shape[0]

    def issue(r, carry):
        pltpu.make_async_copy(ys_hbm.at[pl.ds(idx_smem[r], 1)], buf.at[pl.ds(r, 1)], sem).start()
        return carry

    lax.fori_loop(0, count, issue, 0, unroll=8)
    pltpu.make_async_copy(ys_hbm.at[pl.ds(0, count)], buf, sem).wait()
    tc = x_ref.shape[0]
    rec = rec_ref[...]
    out = x_ref[...] + rec[:, 0:1] * buf[:tc] + rec[:, 1:2] * buf[tc:]
    if final:
        out = _rms(out, gfin_ref[...])
    o_ref[...] = out


def _moe_combine(pos2d, ys, x2, rec, gfin, final):
    n = x2.shape[0]
    tc = pos2d.shape[1] // 2
    row = lambda c: pl.BlockSpec((tc, c), lambda i: (i, 0))
    return pl.pallas_call(
        functools.partial(_moe_combine_kernel, final=final),
        out_shape=jax.ShapeDtypeStruct((n, D_MODEL), F32),
        grid=(n // tc,),
        in_specs=[pl.BlockSpec(memory_space=pl.ANY), pl.BlockSpec(memory_space=pl.ANY), row(D_MODEL),
                  row(N_EXPERTS), pl.BlockSpec((1, D_MODEL), lambda i: (0, 0))],
        out_specs=row(D_MODEL),
        scratch_shapes=[pltpu.SMEM((2 * tc,), jnp.int32), pltpu.VMEM((2 * tc, D_MODEL), F32),
                        pltpu.SemaphoreType.DMA, pltpu.SemaphoreType.DMA],
        compiler_params=_cparams(("arbitrary",)),
        name="moe_combine",
    )(pos2d, ys, x2, rec, gfin)


def _pad_axis(a, axis, size):
    pad = [(0, 0)] * a.ndim
    pad[axis] = (0, size - a.shape[axis])
    return jnp.pad(a, pad)


def kernel(x, g_mix, w_in, ssm_lambda_re, ssm_lambda_im, ssm_log_step, ssm_b_re, ssm_b_im, ssm_c_re, ssm_c_im, ssm_d, w_glu, b_glu, g_ssm_out, lambda_q1, lambda_k1, lambda_q2, lambda_k2, g_subln, w_out, g_ffn, dense_w_gate, dense_w_up, dense_w_down, w_router, b_router, moe_w_gate, moe_w_up, moe_w_down, g_final):
    bsz, seq, _ = x.shape
    n = bsz * seq
    depth = w_in.shape[0]
    n_chunks = seq // SSM_CHUNK
    assert seq % KV_CHUNK == 0 and bsz == SUBLANES and depth % 2 == 0 and (2 * n) % MOE_TILE == 0
    tm = min(512, n)
    tm_ffn = min(1024, n)
    d_ff = dense_w_gate.shape[-1]
    d_ff_pad = -(-d_ff // 256) * 256
    slopes2 = [2.0 ** (-8.0 * (h + 1) / HEADS) * LOG2E for h in range(HEADS)]
    kcols, bias_tiles = _alibi_constants(slopes2)
    pmat, qmat = _regroup_matrices()

    x2 = x.reshape(n, D_MODEL)
    for i in range(depth):
        lambda_init = 0.8 - 0.6 * math.exp(-0.3 * i)
        u, q, k, v = _inproj(x2, g_mix[i][None, :], w_in[i].astype(BF16), kcols)

        m, w, vv, dec_re, dec_im = _ssm_matrices(
            ssm_lambda_re[i], ssm_lambda_im[i], ssm_log_step[i], ssm_b_re[i], ssm_b_im[i],
            ssm_c_re[i], ssm_c_im[i], ssm_d[i])
        uc = _regroup_in(u, pmat, bsz)
        yc = _ssm(uc, m, w, vv, dec_re, dec_im, n_chunks, bsz)
        ys = _regroup_out(yc, qmat, bsz)

        lam = (jnp.exp(jnp.sum(lambda_q1[i].astype(F32) * lambda_k1[i].astype(F32)))
               - jnp.exp(jnp.sum(lambda_q2[i].astype(F32) * lambda_k2[i].astype(F32))) + lambda_init)
        par = jnp.concatenate([jnp.stack([lam, jnp.asarray(1.0 - lambda_init, F32)]),
                               jnp.asarray(slopes2, F32), jnp.zeros((2,), F32)])
        ya = _attention(par, q.reshape(bsz, seq, ATTN_WIDTH), k.reshape(bsz, seq, 2 * ATTN_WIDTH),
                        v.reshape(bsz, seq, ATTN_WIDTH), bias_tiles, g_subln[i].astype(F32)[:, None])
        ya = ya.reshape(n, ATTN_WIDTH)

        j = i // 2
        if i % 2 == 0:
            router_w = None
        else:
            router_w = (w_router[j].astype(F32).T,
                        jnp.concatenate([b_router[j].astype(F32),
                                         jnp.full((LANES - N_EXPERTS,), -jnp.inf, F32)])[None, :])
        outs = _postmix(x2, ys, ya, w_glu[i].astype(BF16), b_glu[i][None, :], g_ssm_out[i][None, :],
                        w_out[i].astype(BF16), g_ffn[i][None, :], router_w, tm)
        if i % 2 == 0:
            x2, h = outs
            wg = _pad_axis(dense_w_gate[j].astype(BF16), 1, d_ff_pad)
            wu = _pad_axis(dense_w_up[j].astype(BF16), 1, d_ff_pad)
            wd = _pad_axis(dense_w_down[j].astype(BF16), 0, d_ff_pad)
            x2 = _ffn(h, x2, wg, wu, wd, tm, d_ff_pad // 2)
        else:
            x2, h, rec = outs
            pos2d, texp, n_used, pad = _route(rec, MOE_TILE, min(MOE_TOKEN_TILE, n))
            xs = _moe_dispatch(pad, pos2d, h, texp.shape[0] * MOE_TILE)
            ys = _moe_ffn(texp, n_used, xs, moe_w_gate, moe_w_up, moe_w_down, j, MOE_TILE, 512)
            x2 = _moe_combine(pos2d, ys, x2, rec, g_final[None, :], i == depth - 1)
    return x2.reshape(bsz, seq, D_MODEL)
```

```python
import functools
import math

import numpy as np
import jax
import jax.numpy as jnp
from jax import lax
from jax.experimental import pallas as pl
from jax.experimental.pallas import tpu as pltpu

F32 = jnp.float32
BF16 = jnp.bfloat16

D_MODEL = 1024
SSM_WIDTH = 512
SSM_GROUP = 16
SSM_GROUPS = 32
SSM_STATE = 64
ATTN_WIDTH = 512
HEAD_DIM = 64
HEADS = 4
HEAD_V = 2 * HEAD_DIM
N_EXPERTS = 8
NORM_EPS = 1e-6

LANES = 128
SUBLANES = 8
KV_CHUNK = 512
TQ = 512
VT_ROWS = HEAD_V + 16
ALIBI_PIECES = 3
LOG2E = math.log2(math.e)
MOE_TILE = 1024
MOE_TOKEN_TILE = 1024
SSM_CHUNK = 16
CHUNK_COLS = SSM_CHUNK * SSM_GROUP
GROUP_BLOCK = LANES // SSM_GROUP
STEP_PAIRS = SSM_CHUNK // 2
VMEM_LIMIT = 56 * 1024 * 1024


def _cparams(sem):
    return pltpu.CompilerParams(dimension_semantics=sem, vmem_limit_bytes=VMEM_LIMIT)


def _rms(x, g):
    return x * lax.rsqrt(jnp.mean(x * x, axis=-1, keepdims=True) + NORM_EPS) * g


def _inproj_kernel(x_ref, g_ref, w_ref, kcols_ref, u_ref, q_ref, k_ref, v_ref, us_ref):
    h = _rms(x_ref[...], g_ref[...]).astype(BF16)
    proj = lambda n: jnp.dot(h, w_ref[:, n * 512:(n + 1) * 512], preferred_element_type=F32)
    z = proj(0)
    n_chunks = us_ref.shape[1] // SSM_CHUNK
    for j in range(SSM_WIDTH // LANES):
        us_ref[j] = z[:, j * LANES:(j + 1) * LANES]
        for s in range(SSM_CHUNK):
            u_ref[s, :, j * LANES:(j + 1) * LANES] = us_ref[j, pl.ds(s, n_chunks, stride=SSM_CHUNK), :].astype(BF16)
    q_ref[...] = (proj(1) * (HEAD_DIM ** -0.5 * LOG2E)).astype(BF16)
    k = proj(2).astype(BF16)
    for hd in range(HEADS):
        k_ref[:, 2 * hd * HEAD_V:(2 * hd + 1) * HEAD_V] = k[:, hd * HEAD_V:(hd + 1) * HEAD_V]
        k_ref[:, (2 * hd + 1) * HEAD_V:(2 * hd + 2) * HEAD_V] = kcols_ref[:, hd * HEAD_V:(hd + 1) * HEAD_V]
    v_ref[...] = proj(3).astype(BF16)


def _inproj(x2, g, w_bf, kcols):
    n = x2.shape[0]
    tm = KV_CHUNK
    out = jax.ShapeDtypeStruct((n, 512), BF16)
    row = lambda c: pl.BlockSpec((tm, c), lambda i: (i, 0))
    return pl.pallas_call(
        _inproj_kernel,
        out_shape=(jax.ShapeDtypeStruct((SSM_CHUNK, n // SSM_CHUNK, SSM_WIDTH), BF16), out,
                   jax.ShapeDtypeStruct((n, 2 * ATTN_WIDTH), BF16), out),
        grid=(n // tm,),
        in_specs=[
            row(D_MODEL),
            pl.BlockSpec((1, D_MODEL), lambda i: (0, 0)),
            pl.BlockSpec((D_MODEL, 2048), lambda i: (0, 0)),
            pl.BlockSpec((tm, ATTN_WIDTH), lambda i: (0, 0)),
        ],
        out_specs=(pl.BlockSpec((SSM_CHUNK, tm // SSM_CHUNK, SSM_WIDTH), lambda i: (0, i, 0)),
                   row(512), row(2 * ATTN_WIDTH), row(512)),
        scratch_shapes=[pltpu.VMEM((SSM_WIDTH // LANES, tm, LANES), F32)],
        compiler_params=_cparams(("parallel",)),
        name="inproj",
    )(x2, g, w_bf, kcols)


def _ssm_matrices(lam_re, lam_im, log_step, b_re, b_im, c_re, c_im, d_skip):
    t = SSM_CHUNK
    tau = jnp.arange(t + 1, dtype=F32)
    ks, ws, es, decs = [], [], [], []
    for d in range(2):
        lam = lax.complex(lam_re[d].astype(F32), lam_im[d].astype(F32))
        step = jnp.exp(log_step[d].astype(F32))[:, None]
        ls = lam * step
        lam_bar = jnp.exp(ls)
        pw = jnp.exp(ls[:, None, :] * tau[None, :, None])
        b_bar = ((lam_bar - 1.0) / lam)[:, :, None] * lax.complex(b_re[d].astype(F32), b_im[d].astype(F32))
        c = lax.complex(c_re[d].astype(F32), c_im[d].astype(F32))
        ks.append(jnp.real(jnp.einsum('gop,gtp,gpi->gtoi', c, pw[:, :t], b_bar)))
        if d == 0:
            wpow = pw[:, :t][:, ::-1]
            epow = pw[:, 1:t + 1]
        else:
            wpow = pw[:, :t]
            epow = pw[:, 1:t + 1][:, ::-1]
        ws.append(jnp.einsum('gsp,gpi->gsip', wpow, b_bar).reshape(SSM_GROUPS, CHUNK_COLS, SSM_STATE))
        es.append(jnp.einsum('gop,gtp->gpto', c, epow).reshape(SSM_GROUPS, SSM_STATE, CHUNK_COLS))
        decs.append(pw[:, t])
    kf, kb = ks
    s_idx = jnp.arange(t)[:, None]
    t_idx = jnp.arange(t)[None, :]
    lag_f = t_idx - s_idx
    lag_b = s_idx - t_idx
    lags = jnp.arange(t)[None, None, :]
    mf = jnp.einsum('stk,gkoi->gstoi', (lag_f[:, :, None] == lags).astype(F32), kf)
    mb = jnp.einsum('stk,gkoi->gstoi', (lag_b[:, :, None] == lags).astype(F32), kb)
    dmat = (jnp.eye(t, dtype=F32)[None, :, :, None, None]
            * (jnp.eye(SSM_GROUP, dtype=F32)[None, None, None] * d_skip.astype(F32).reshape(SSM_GROUPS, 1, 1, 1, SSM_GROUP)))
    m = (mf + mb + dmat).transpose(0, 1, 4, 2, 3).reshape(SSM_GROUPS, CHUNK_COLS, CHUNK_COLS)
    w = jnp.concatenate([jnp.real(ws[0]), jnp.real(ws[1]), jnp.imag(ws[0]), jnp.imag(ws[1])], axis=-1)
    v = jnp.concatenate([jnp.real(es[0]), jnp.real(es[1]), -jnp.imag(es[0]), -jnp.imag(es[1])], axis=1)
    dec_re = jnp.concatenate([jnp.real(decs[0]), jnp.real(decs[1])], axis=-1)[:, None, :]
    dec_im = jnp.concatenate([jnp.imag(decs[0]), jnp.imag(decs[1])], axis=-1)[:, None, :]
    return m.astype(BF16), w.astype(BF16), v.astype(BF16), dec_re, dec_im


def _ssm_kernel(u_ref, m_ref, w_ref, v_ref, ar_ref, ai_ref, y_ref, s_ref, hf_ref, hb_ref, *, n_chunks, rows):
    u = u_ref[0]
    s = jnp.dot(u, w_ref[0], preferred_element_type=F32)
    pitch = n_chunks + 1
    for b in range(rows):
        s_ref[0, b * pitch:b * pitch + n_chunks, :] = s[b * n_chunks:(b + 1) * n_chunks, :LANES]
        s_ref[1, b * pitch:b * pitch + n_chunks, :] = s[b * n_chunks:(b + 1) * n_chunks, LANES:]
    ar = jnp.broadcast_to(ar_ref[0], (rows, LANES))
    ai = jnp.broadcast_to(ai_ref[0], (rows, LANES))

    def body(c, carry):
        fr, fi, br, bi = carry
        at_f = pl.ds(c, rows, stride=pitch)
        at_b = pl.ds(n_chunks - 1 - c, rows, stride=pitch)
        hf_ref[0, at_f, :] = fr
        hf_ref[1, at_f, :] = fi
        hb_ref[0, at_b, :] = br
        hb_ref[1, at_b, :] = bi
        nfr = ar * fr - ai * fi + s_ref[0, at_f, :]
        nfi = ai * fr + ar * fi + s_ref[1, at_f, :]
        nbr = ar * br - ai * bi + s_ref[0, at_b, :]
        nbi = ai * br + ar * bi + s_ref[1, at_b, :]
        return nfr, nfi, nbr, nbi

    z = jnp.zeros((rows, LANES), F32)
    lax.fori_loop(0, n_chunks, body, (z, z, z, z), unroll=4)
    is_fwd = lax.broadcasted_iota(jnp.int32, (1, LANES), 1) < SSM_STATE
    seq_rows = lambda ref, part: jnp.concatenate(
        [ref[part, b * pitch:b * pitch + n_chunks, :] for b in range(rows)], axis=0)
    hcat = jnp.concatenate([jnp.where(is_fwd, seq_rows(hf_ref, 0), seq_rows(hb_ref, 0)),
                            jnp.where(is_fwd, seq_rows(hf_ref, 1), seq_rows(hb_ref, 1))], axis=1).astype(BF16)
    y = jnp.dot(u, m_ref[0], preferred_element_type=F32)
    y = y + jnp.dot(hcat, v_ref[0], preferred_element_type=F32)
    y_ref[0] = jax.nn.gelu(y).astype(BF16)


def _ssm(uc, m, w, v, dec_re, dec_im, n_chunks, rows):
    g, r, _ = uc.shape
    mat = pl.BlockSpec((1, CHUNK_COLS, CHUNK_COLS), lambda i: (i, 0, 0))
    dec = pl.BlockSpec((1, 1, LANES), lambda i: (i, 0, 0))
    return pl.pallas_call(
        functools.partial(_ssm_kernel, n_chunks=n_chunks, rows=rows),
        out_shape=jax.ShapeDtypeStruct((g, r, CHUNK_COLS), BF16),
        grid=(g,),
        in_specs=[pl.BlockSpec((1, r, CHUNK_COLS), lambda i: (i, 0, 0)), mat, mat, mat, dec, dec],
        out_specs=pl.BlockSpec((1, r, CHUNK_COLS), lambda i: (i, 0, 0)),
        scratch_shapes=[pltpu.VMEM((2, rows * (n_chunks + 1), LANES), F32)] * 3,
        compiler_params=_cparams(("parallel",)),
        name="ssm",
    )(uc, m, w, v, dec_re, dec_im)


def _regroup_matrices():
    p = np.zeros((STEP_PAIRS, 2, GROUP_BLOCK, SSM_GROUP, GROUP_BLOCK, SSM_CHUNK, SSM_GROUP), np.float32)
    gl = np.arange(GROUP_BLOCK)[:, None]
    h = np.arange(SSM_GROUP)[None, :]
    for sp in range(STEP_PAIRS):
        for half in range(2):
            p[sp, half, gl, h, gl, sp + half * STEP_PAIRS, h] = 1.0
    p = p.reshape(STEP_PAIRS, 2 * LANES, GROUP_BLOCK, CHUNK_COLS)
    q = p.transpose(0, 2, 3, 1)
    return (jnp.asarray(p.reshape(STEP_PAIRS, 2 * LANES, GROUP_BLOCK * CHUNK_COLS), BF16), jnp.asarray(q, BF16))


def _regroup_in_kernel(us_ref, p_ref, o_ref):
    pairs = [jnp.concatenate([us_ref[sp], us_ref[sp + STEP_PAIRS]], axis=1) for sp in range(STEP_PAIRS)]
    for gl in range(GROUP_BLOCK):
        acc = None
        for sp in range(STEP_PAIRS):
            d = jnp.dot(pairs[sp], p_ref[sp, :, gl * CHUNK_COLS:(gl + 1) * CHUNK_COLS], preferred_element_type=F32)
            acc = d if acc is None else acc + d
        o_ref[gl] = acc.astype(BF16)


def _regroup_out_kernel(yc_ref, q_ref, o_ref):
    for tp in range(STEP_PAIRS):
        acc = None
        for gl in range(GROUP_BLOCK):
            d = jnp.dot(yc_ref[gl], q_ref[tp, gl], preferred_element_type=F32)
            acc = d if acc is None else acc + d
        o_ref[tp] = acc[:, :LANES].astype(BF16)
        o_ref[tp + STEP_PAIRS] = acc[:, LANES:].astype(BF16)


def _regroup_specs(c):
    steps = lambda f: pl.BlockSpec((SSM_CHUNK, c, LANES), f)
    groups = lambda f: pl.BlockSpec((GROUP_BLOCK, c, CHUNK_COLS), f)
    return steps(lambda gb, b: (0, b, gb)), groups(lambda gb, b: (gb, b, 0))


def _regroup_in(us, pmat, bsz):
    c = us.shape[1] // bsz
    steps, groups = _regroup_specs(c)
    return pl.pallas_call(
        _regroup_in_kernel,
        out_shape=jax.ShapeDtypeStruct((SSM_GROUPS, bsz * c, CHUNK_COLS), BF16),
        grid=(SSM_GROUPS // GROUP_BLOCK, bsz),
        in_specs=[steps, pl.BlockSpec(pmat.shape, lambda gb, b: (0, 0, 0))],
        out_specs=groups,
        compiler_params=_cparams(("parallel", "parallel")),
        name="regroup_in",
    )(us, pmat)


def _regroup_out(yc, qmat, bsz):
    c = yc.shape[1] // bsz
    steps, groups = _regroup_specs(c)
    return pl.pallas_call(
        _regroup_out_kernel,
        out_shape=jax.ShapeDtypeStruct((SSM_CHUNK, c * bsz, SSM_WIDTH), BF16),
        grid=(SSM_GROUPS // GROUP_BLOCK, bsz),
        in_specs=[groups, pl.BlockSpec(qmat.shape, lambda gb, b: (0, 0, 0, 0))],
        out_specs=steps,
        compiler_params=_cparams(("parallel", "parallel")),
        name="regroup_out",
    )(yc, qmat)


def _alibi_constants(slopes2):
    bf = jnp.bfloat16
    jj = np.arange(KV_CHUNK, dtype=np.float32)
    rel = np.arange(TQ, dtype=np.float32)[None, :] - jj[:, None]
    key_cols = np.zeros((KV_CHUNK, HEADS, HEAD_V), np.float32)
    for hd, s in enumerate(slopes2):
        rest = np.float32(s) * jj
        for piece in range(ALIBI_PIECES):
            part = rest.astype(bf).astype(np.float32)
            key_cols[:, hd, piece] = part
            rest = rest - part
    tiles = np.stack([np.float32(s) * rel for s in slopes2])
    return jnp.asarray(key_cols.reshape(KV_CHUNK, HEADS * HEAD_V), BF16), jnp.asarray(tiles, F32)


def _attn_kernel(par_ref, q_ref, k_ref, v_ref, bt_ref, g_ref, o_ref, vt_ref, qv_ref, acc_ref, sa_ref, sb_ref,
                 *, n_chunks):
    head = pl.program_id(1)
    qi = pl.program_id(2)
    lam = par_ref[0]
    out_scale = par_ref[1]
    slope = par_ref[2 + head]
    q_pos = slope * lax.broadcasted_iota(jnp.int32, (1, TQ), 1).astype(F32)

    @pl.when(qi == 0)
    def _():
        ones_rows = (lax.broadcasted_iota(jnp.int32, (VT_ROWS - HEAD_V, KV_CHUNK), 0) == 0).astype(BF16)
        for c in range(n_chunks):
            vt_ref[c, :HEAD_V] = v_ref[0, c * KV_CHUNK:(c + 1) * KV_CHUNK, :].astype(F32).T.astype(BF16)
            vt_ref[c, HEAD_V:] = ones_rows

    qt = q_ref[0].astype(F32).T.astype(BF16)
    zero = jnp.zeros((HEAD_DIM, TQ), BF16)
    pick = (lax.broadcasted_iota(jnp.int32, (HEAD_V, TQ), 0) < ALIBI_PIECES).astype(F32)
    for var, sign in enumerate((1.0, -1.0, 0.0)):
        qv_ref[var, :HEAD_DIM] = qt[:HEAD_DIM]
        qv_ref[var, HEAD_DIM:HEAD_V] = zero
        qv_ref[3 + var, :HEAD_DIM] = zero
        qv_ref[3 + var, HEAD_DIM:HEAD_V] = qt[HEAD_DIM:]
        qv_ref[var, HEAD_V:] = (sign * pick).astype(BF16)
        qv_ref[3 + var, HEAD_V:] = (sign * pick).astype(BF16)

    qd = qi // (KV_CHUNK // TQ)
    i0 = qi * TQ

    s_bufs = (sa_ref, sb_ref)

    def scores(t):
        buf = s_bufs[t % 2]
        if t == 0:
            c = qd
            var = 2
            off = jnp.zeros((1, TQ), F32)
            bias = -jnp.abs(bt_ref[0] + slope * (i0 - qd * KV_CHUNK).astype(F32))
        else:
            c = (t - 1) + ((t - 1) >= qd).astype(jnp.int32)
            after = c < qd
            var = jnp.where(after, 0, 1)
            off = -slope * jnp.abs(i0 - c * KV_CHUNK).astype(F32) + jnp.where(after, -q_pos, q_pos)
            bias = None
        k = k_ref[0, pl.ds(pl.multiple_of(c * KV_CHUNK, KV_CHUNK), KV_CHUNK), :]
        smax = []
        for mi in range(2):
            s = jnp.dot(k, qv_ref[3 * mi + var], preferred_element_type=F32)
            if bias is not None:
                s = s + bias
            buf[mi] = s
            smax.append(jnp.max(s, axis=0, keepdims=True) + off)
        return c, off, smax

    def accumulate(t, c, c_off, smax, m):
        buf = s_bufs[t % 2]
        vt = vt_ref[c]
        for mi in range(2):
            m_new = jnp.maximum(m[mi], smax[mi])
            p = jnp.exp2(buf[mi] - (m_new - c_off))
            alpha = jnp.exp2(m[mi] - m_new)
            acc_ref[mi] = alpha * acc_ref[mi] + jnp.dot(vt, p.astype(BF16), preferred_element_type=F32)
            m[mi] = m_new

    acc_ref[...] = jnp.zeros_like(acc_ref)
    m = [jnp.full((1, TQ), -1e30, F32)] * 2
    pending = scores(0)
    for t in range(n_chunks):
        nxt = scores(t + 1) if t + 1 < n_chunks else None
        accumulate(t, *pending, m)
        pending = nxt
    o = (acc_ref[0, :HEAD_V] / acc_ref[0, HEAD_V:HEAD_V + 1]
         - lam * (acc_ref[1, :HEAD_V] / acc_ref[1, HEAD_V:HEAD_V + 1]))
    o = o * lax.rsqrt(jnp.mean(o * o, axis=0, keepdims=True) + NORM_EPS)
    o = o * (g_ref[...] * out_scale)
    o_ref[0] = o.T.astype(BF16)


def _attention(par, q, k, v, bias_tiles, g_col):
    b, l, _ = v.shape
    n_chunks = l // KV_CHUNK
    return pl.pallas_call(
        functools.partial(_attn_kernel, n_chunks=n_chunks),
        out_shape=jax.ShapeDtypeStruct((b, l, ATTN_WIDTH), BF16),
        grid=(b, HEADS, l // TQ),
        in_specs=[
            pl.BlockSpec(memory_space=pltpu.SMEM),
            pl.BlockSpec((1, TQ, HEAD_V), lambda bi, h, i: (bi, i, h)),
            pl.BlockSpec((1, l, 2 * HEAD_V), lambda bi, h, i: (bi, 0, h)),
            pl.BlockSpec((1, l, HEAD_V), lambda bi, h, i: (bi, 0, h)),
            pl.BlockSpec((1, KV_CHUNK, TQ), lambda bi, h, i: (h, 0, 0)),
            pl.BlockSpec((HEAD_V, 1), lambda bi, h, i: (0, 0)),
        ],
        out_specs=pl.BlockSpec((1, TQ, HEAD_V), lambda bi, h, i: (bi, i, h)),
        scratch_shapes=[
            pltpu.VMEM((n_chunks, VT_ROWS, KV_CHUNK), BF16),
            pltpu.VMEM((6, 2 * HEAD_V, TQ), BF16),
            pltpu.VMEM((2, VT_ROWS, TQ), F32),
            pltpu.VMEM((2, KV_CHUNK, TQ), F32),
            pltpu.VMEM((2, KV_CHUNK, TQ), F32),
        ],
        compiler_params=_cparams(("parallel", "parallel", "arbitrary")),
        name="diff_attn",
    )(par, q, k, v, bias_tiles, g_col)


def _postmix_kernel(x_ref, ys_ref, ya_ref, wglu_ref, bglu_ref, gs_ref, wout_ref, gf_ref, *rest, router):
    if router:
        wr_ref, br_ref, xo_ref, h_ref, gate_ref, ysc_ref = rest
    else:
        xo_ref, h_ref, ysc_ref = rest
    n_chunks = ys_ref.shape[1]
    for j in range(SSM_WIDTH // LANES):
        for s in range(SSM_CHUNK):
            ysc_ref[j, pl.ds(s, n_chunks, stride=SSM_CHUNK), :] = ys_ref[s, :, j * LANES:(j + 1) * LANES].astype(F32)
    y = jnp.concatenate([ysc_ref[j] for j in range(SSM_WIDTH // LANES)], axis=1)
    t = jnp.dot(y.astype(BF16), wglu_ref[...], preferred_element_type=F32) + bglu_ref[...]
    y = y * jax.nn.sigmoid(t)
    y = _rms(y, gs_ref[...]).astype(BF16)
    mix = jnp.dot(y, wout_ref[:SSM_WIDTH, :], preferred_element_type=F32)
    mix = mix + jnp.dot(ya_ref[...], wout_ref[SSM_WIDTH:, :], preferred_element_type=F32)
    x = x_ref[...] + mix
    xo_ref[...] = x
    h = _rms(x, gf_ref[...])
    h_ref[...] = h.astype(h_ref.dtype)
    if router:
        lane = lax.broadcasted_iota(jnp.int32, (h.shape[0], LANES), 1)
        logits = jnp.broadcast_to(br_ref[...], lane.shape)
        for e in range(N_EXPERTS):
            le = jnp.sum(h * wr_ref[e:e + 1, :], axis=-1, keepdims=True)
            logits = jnp.where(lane == e, logits + le, logits)
        big = jnp.int32(LANES)
        m1 = jnp.max(logits, axis=-1, keepdims=True)
        i1 = jnp.min(jnp.where(logits == m1, lane, big), axis=-1, keepdims=True)
        rest_l = jnp.where(lane == i1, -jnp.inf, logits)
        m2 = jnp.max(rest_l, axis=-1, keepdims=True)
        i2 = jnp.min(jnp.where(rest_l == m2, lane, big), axis=-1, keepdims=True)
        e2 = jnp.exp(m2 - m1)
        w1 = 1.0 / (1.0 + e2)
        w2 = e2 * w1
        rec = jnp.where(lane == 0, w1, jnp.where(lane == 1, w2, jnp.where(
            lane == 2, i1.astype(F32), jnp.where(lane == 3, i2.astype(F32), 0.0))))
        gate_ref[...] = rec[:, :N_EXPERTS]


def _postmix(x2, ys, ya, wglu, bglu, gs, wout, gf, router_w, tm):
    n = x2.shape[0]
    router = router_w is not None
    row = lambda c: pl.BlockSpec((tm, c), lambda i: (i, 0))
    full = lambda r, c: pl.BlockSpec((r, c), lambda i: (0, 0))
    steps = pl.BlockSpec((SSM_CHUNK, tm // SSM_CHUNK, SSM_WIDTH), lambda i: (0, i, 0))
    in_specs = [row(D_MODEL), steps, row(512), full(512, 512), full(1, 512), full(1, 512),
                full(D_MODEL, D_MODEL), full(1, D_MODEL)]
    args = [x2, ys, ya, wglu, bglu, gs, wout, gf]
    out_shape = [jax.ShapeDtypeStruct((n, D_MODEL), F32), jax.ShapeDtypeStruct((n, D_MODEL), F32 if router else BF16)]
    out_specs = [row(D_MODEL), row(D_MODEL)]
    if router:
        in_specs += [full(N_EXPERTS, D_MODEL), full(1, LANES)]
        args += list(router_w)
        out_shape.append(jax.ShapeDtypeStruct((n, N_EXPERTS), F32))
        out_specs.append(row(N_EXPERTS))
    return pl.pallas_call(
        functools.partial(_postmix_kernel, router=router),
        out_shape=tuple(out_shape),
        grid=(n // tm,),
        in_specs=in_specs,
        out_specs=tuple(out_specs),
        scratch_shapes=[pltpu.VMEM((SSM_WIDTH // LANES, tm, LANES), F32)],
        compiler_params=_cparams(("parallel",)),
        name="postmix_router" if router else "postmix",
    )(*args)


def _swiglu_partial(h, wg, wu, wd):
    g = jnp.dot(h, wg, preferred_element_type=F32)
    u = jnp.dot(h, wu, preferred_element_type=F32)
    return jnp.dot((jax.nn.silu(g) * u).astype(BF16), wd, preferred_element_type=F32)


def _ffn_kernel(h_ref, x_ref, wg_ref, wu_ref, wd_ref, o_ref):
    @pl.when(pl.program_id(1) == 0)
    def _():
        o_ref[...] = x_ref[...]

    o_ref[...] += _swiglu_partial(h_ref[...], wg_ref[...], wu_ref[...], wd_ref[...])


def _ffn(h, x2, wg, wu, wd, tm, tf):
    n = x2.shape[0]
    ff = wg.shape[1]
    row = lambda c: pl.BlockSpec((tm, c), lambda i, f: (i, 0))
    return pl.pallas_call(
        _ffn_kernel,
        out_shape=jax.ShapeDtypeStruct((n, D_MODEL), F32),
        grid=(n // tm, ff // tf),
        in_specs=[row(D_MODEL), row(D_MODEL),
                  pl.BlockSpec((D_MODEL, tf), lambda i, f: (0, f)),
                  pl.BlockSpec((D_MODEL, tf), lambda i, f: (0, f)),
                  pl.BlockSpec((tf, D_MODEL), lambda i, f: (f, 0))],
        out_specs=row(D_MODEL),
        compiler_params=_cparams(("parallel", "arbitrary")),
        name="ffn_dense",
    )(h, x2, wg, wu, wd)


def _route(rec, tm, tc):
    n = rec.shape[0]
    a = 2 * n
    n_tiles = a // tm + N_EXPERTS
    e = rec[:, 2:4].astype(jnp.int32).reshape(a)
    onehot = (e[:, None] == jnp.arange(N_EXPERTS, dtype=jnp.int32)[None, :]).astype(jnp.int32)
    csum = jnp.cumsum(onehot, axis=0)
    rank = jnp.sum(csum * onehot, axis=1) - 1
    count = csum[-1]
    padded = (count + tm - 1) // tm * tm
    ends = jnp.cumsum(padded)
    pad_start = ends - padded + count
    pad_aligned = (pad_start + SUBLANES - 1) // SUBLANES * SUBLANES
    pad = jnp.concatenate([pad_start, pad_aligned, ends - pad_aligned, ends[-1:] // tm]).astype(jnp.int32)
    pos = jnp.sum(onehot * (ends - padded)[None, :], axis=1) + rank
    n_used = (ends[-1] // tm).astype(jnp.int32)
    tile = jnp.arange(n_tiles, dtype=jnp.int32)
    texp = jnp.sum((tile[:, None] >= (ends // tm)[None, :]).astype(jnp.int32), axis=1)
    texp = jnp.minimum(texp, N_EXPERTS - 1)
    texp = jnp.where(tile < n_used, texp, texp[jnp.maximum(n_used - 1, 0)])
    pos2d = pos.reshape(n // tc, tc, 2).transpose(0, 2, 1).reshape(n // tc, 2 * tc)
    return pos2d, texp, n_used.reshape(1), pad


def _fetch_rows_index(pos_hbm, idx_smem, sem_idx):
    cp = pltpu.make_async_copy(pos_hbm.at[pl.program_id(0)], idx_smem, sem_idx)
    cp.start()
    cp.wait()


def _moe_dispatch_kernel(pad_ref, pos_hbm, h_ref, xs_hbm, idx_smem, zeros_ref, sem_idx, sem, sem_pad):
    @pl.when(pl.program_id(0) == 0)
    def _():
        zeros_ref[...] = jnp.zeros_like(zeros_ref)

        def padding_copies(act):
            zero_rows = lambda off, size: pltpu.make_async_copy(
                zeros_ref.at[pl.ds(0, size)], xs_hbm.at[pl.ds(off, size)], sem_pad)
            for e in range(N_EXPERTS):
                first = pad_ref[e]
                aligned = pad_ref[N_EXPERTS + e]
                length = pad_ref[2 * N_EXPERTS + e]
                for j in range(SUBLANES - 1):
                    @pl.when(first + j < aligned)
                    def _(row=first + j):
                        act(zero_rows(row, 1))

                off = aligned
                for bit in reversed(range(3, MOE_TILE.bit_length() - 1)):
                    size = 1 << bit

                    @pl.when((length & size) != 0)
                    def _(off=off, size=size):
                        act(zero_rows(pl.multiple_of(off, SUBLANES), size))

                    off = off + (length & size)
            n_tiles = xs_hbm.shape[0] // MOE_TILE
            half = zeros_ref.shape[0]
            for t in range(n_tiles - N_EXPERTS, n_tiles):
                @pl.when(t >= pad_ref[3 * N_EXPERTS])
                def _(t=t):
                    for part in range(MOE_TILE // half):
                        act(zero_rows(t * MOE_TILE + part * half, half))

        padding_copies(lambda cp: cp.start())
        padding_copies(lambda cp: cp.wait())

    _fetch_rows_index(pos_hbm, idx_smem, sem_idx)
    tc = h_ref.shape[0]

    def issue(r, carry):
        src = h_ref.at[pl.ds(r, 1)]
        pltpu.make_async_copy(src, xs_hbm.at[pl.ds(idx_smem[r], 1)], sem).start()
        pltpu.make_async_copy(src, xs_hbm.at[pl.ds(idx_smem[tc + r], 1)], sem).start()
        return carry

    lax.fori_loop(0, tc, issue, 0, unroll=8)
    for _ in range(2):
        pltpu.make_async_copy(h_ref, xs_hbm.at[pl.ds(0, tc)], sem).wait()


def _moe_dispatch(pad, pos2d, h, rows):
    n = h.shape[0]
    tc = pos2d.shape[1] // 2
    return pl.pallas_call(
        _moe_dispatch_kernel,
        out_shape=jax.ShapeDtypeStruct((rows, D_MODEL), F32),
        grid_spec=pltpu.PrefetchScalarGridSpec(
            num_scalar_prefetch=1,
            grid=(n // tc,),
            in_specs=[pl.BlockSpec(memory_space=pl.ANY), pl.BlockSpec((tc, D_MODEL), lambda i, pad: (i, 0))],
            out_specs=pl.BlockSpec(memory_space=pl.ANY),
            scratch_shapes=[pltpu.SMEM((2 * tc,), jnp.int32), pltpu.VMEM((MOE_TILE // 2, D_MODEL), F32),
                            pltpu.SemaphoreType.DMA, pltpu.SemaphoreType.DMA, pltpu.SemaphoreType.DMA],
        ),
        compiler_params=_cparams(("arbitrary",)),
        name="moe_dispatch",
    )(pad, pos2d, h)


def _moe_ffn_kernel(texp_ref, nused_ref, xs_ref, wg_ref, wu_ref, wd_ref, o_ref, xb_ref):
    t = pl.program_id(0)
    f = pl.program_id(1)
    last = f == pl.num_programs(1) - 1
    valid = t < nused_ref[0]

    @pl.when(valid)
    def _():
        @pl.when(f == 0)
        def _():
            xb_ref[...] = xs_ref[...].astype(BF16)
            o_ref[...] = jnp.zeros_like(o_ref)

        o_ref[...] += _swiglu_partial(xb_ref[...], wg_ref[0, 0].astype(BF16), wu_ref[0, 0].astype(BF16),
                                      wd_ref[0, 0].astype(BF16))

    @pl.when(jnp.logical_and(jnp.logical_not(valid), last))
    def _():
        o_ref[...] = jnp.zeros_like(o_ref)


def _moe_ffn(texp, n_used, xs, wg, wu, wd, layer, tm, tf):
    rows = xs.shape[0]
    ff = wg.shape[-1]
    nf = ff // tf
    fidx = lambda t, f, nu: jnp.where(t < nu[0], f, nf - 1)
    row = pl.BlockSpec((tm, D_MODEL), lambda t, f, te, nu: (t, 0))
    xs_row = pl.BlockSpec((tm, D_MODEL), lambda t, f, te, nu: (jnp.minimum(t, jnp.maximum(nu[0] - 1, 0)), 0))
    return pl.pallas_call(
        _moe_ffn_kernel,
        out_shape=jax.ShapeDtypeStruct((rows, D_MODEL), F32),
        grid_spec=pltpu.PrefetchScalarGridSpec(
            num_scalar_prefetch=2,
            grid=(rows // tm, nf),
            in_specs=[xs_row,
                      pl.BlockSpec((1, 1, D_MODEL, tf), lambda t, f, te, nu: (layer, te[t], 0, fidx(t, f, nu))),
                      pl.BlockSpec((1, 1, D_MODEL, tf), lambda t, f, te, nu: (layer, te[t], 0, fidx(t, f, nu))),
                      pl.BlockSpec((1, 1, tf, D_MODEL), lambda t, f, te, nu: (layer, te[t], fidx(t, f, nu), 0))],
            out_specs=row,
            scratch_shapes=[pltpu.VMEM((tm, D_MODEL), BF16)],
        ),
        compiler_params=_cparams(("arbitrary", "arbitrary")),
        name="moe_ffn",
    )(texp, n_used, xs, wg, wu, wd)


def _moe_combine_kernel(pos_hbm, ys_hbm, x_ref, rec_ref, gfin_ref, o_ref, idx_smem, buf, sem_idx, sem, *, final):
    _fetch_rows_index(pos_hbm, idx_smem, sem_idx)
    count = buf.shape[0]

    def issue(r, carry):
        pltpu.make_async_copy(ys_hbm.at[pl.ds(idx_smem[r], 1)], buf.at[pl.ds(r, 1)], sem).start()
        return carry

    lax.fori_loop(0, count, issue, 0, unroll=8)
    pltpu.make_async_copy(ys_hbm.at[pl.ds(0, count)], buf, sem).wait()
    tc = x_ref.shape[0]
    rec = rec_ref[...]
    out = x_ref[...] + rec[:, 0:1] * buf[:tc] + rec[:, 1:2] * buf[tc:]
    if final:
        out = _rms(out, gfin_ref[...])
    o_ref[...] = out


def _moe_combine(pos2d, ys, x2, rec, gfin, final):
    n = x2.shape[0]
    tc = pos2d.shape[1] // 2
    row = lambda c: pl.BlockSpec((tc, c), lambda i: (i, 0))
    return pl.pallas_call(
        functools.partial(_moe_combine_kernel, final=final),
        out_shape=jax.ShapeDtypeStruct((n, D_MODEL), F32),
        grid=(n // tc,),
        in_specs=[pl.BlockSpec(memory_space=pl.ANY), pl.BlockSpec(memory_space=pl.ANY), row(D_MODEL),
                  row(N_EXPERTS), pl.BlockSpec((1, D_MODEL), lambda i: (0, 0))],
        out_specs=row(D_MODEL),
        scratch_shapes=[pltpu.SMEM((2 * tc,), jnp.int32), pltpu.VMEM((2 * tc, D_MODEL), F32),
                        pltpu.SemaphoreType.DMA, pltpu.SemaphoreType.DMA],
        compiler_params=_cparams(("arbitrary",)),
        name="moe_combine",
    )(pos2d, ys, x2, rec, gfin)


def _pad_axis(a, axis, size):
    pad = [(0, 0)] * a.ndim
    pad[axis] = (0, size - a.shape[axis])
    return jnp.pad(a, pad)


def kernel(x, g_mix, w_in, ssm_lambda_re, ssm_lambda_im, ssm_log_step, ssm_b_re, ssm_b_im, ssm_c_re, ssm_c_im, ssm_d, w_glu, b_glu, g_ssm_out, lambda_q1, lambda_k1, lambda_q2, lambda_k2, g_subln, w_out, g_ffn, dense_w_gate, dense_w_up, dense_w_down, w_router, b_router, moe_w_gate, moe_w_up, moe_w_down, g_final):
    bsz, seq, _ = x.shape
    n = bsz * seq
    depth = w_in.shape[0]
    n_chunks = seq // SSM_CHUNK
    assert seq % KV_CHUNK == 0 and bsz == SUBLANES and depth % 2 == 0 and (2 * n) % MOE_TILE == 0
    tm = min(512, n)
    d_ff = dense_w_gate.shape[-1]
    d_ff_pad = -(-d_ff // 256) * 256
    slopes2 = [2.0 ** (-8.0 * (h + 1) / HEADS) * LOG2E for h in range(HEADS)]
    kcols, bias_tiles = _alibi_constants(slopes2)
    pmat, qmat = _regroup_matrices()

    x2 = x.reshape(n, D_MODEL)
    for i in range(depth):
        lambda_init = 0.8 - 0.6 * math.exp(-0.3 * i)
        u, q, k, v = _inproj(x2, g_mix[i][None, :], w_in[i].astype(BF16), kcols)

        m, w, vv, dec_re, dec_im = _ssm_matrices(
            ssm_lambda_re[i], ssm_lambda_im[i], ssm_log_step[i], ssm_b_re[i], ssm_b_im[i],
            ssm_c_re[i], ssm_c_im[i], ssm_d[i])
        uc = _regroup_in(u, pmat, bsz)
        yc = _ssm(uc, m, w, vv, dec_re, dec_im, n_chunks, bsz)
        ys = _regroup_out(yc, qmat, bsz)

        lam = (jnp.exp(jnp.sum(lambda_q1[i].astype(F32) * lambda_k1[i].astype(F32)))
               - jnp.exp(jnp.sum(lambda_q2[i].astype(F32) * lambda_k2[i].astype(F32))) + lambda_init)
        par = jnp.concatenate([jnp.stack([lam, jnp.asarray(1.0 - lambda_init, F32)]),
                               jnp.asarray(slopes2, F32), jnp.zeros((2,), F32)])
        ya = _attention(par, q.reshape(bsz, seq, ATTN_WIDTH), k.reshape(bsz, seq, 2 * ATTN_WIDTH),
                        v.reshape(bsz, seq, ATTN_WIDTH), bias_tiles, g_subln[i].astype(F32)[:, None])
        ya = ya.reshape(n, ATTN_WIDTH)

        j = i // 2
        if i % 2 == 0:
            router_w = None
        else:
            router_w = (w_router[j].astype(F32).T,
                        jnp.concatenate([b_router[j].astype(F32),
                                         jnp.full((LANES - N_EXPERTS,), -jnp.inf, F32)])[None, :])
        outs = _postmix(x2, ys, ya, w_glu[i].astype(BF16), b_glu[i][None, :], g_ssm_out[i][None, :],
                        w_out[i].astype(BF16), g_ffn[i][None, :], router_w, tm)
        if i % 2 == 0:
            x2, h = outs
            wg = _pad_axis(dense_w_gate[j].astype(BF16), 1, d_ff_pad)
            wu = _pad_axis(dense_w_up[j].astype(BF16), 1, d_ff_pad)
            wd = _pad_axis(dense_w_down[j].astype(BF16), 0, d_ff_pad)
            x2 = _ffn(h, x2, wg, wu, wd, tm, d_ff_pad // 2)
        else:
            x2, h, rec = outs
            pos2d, texp, n_used, pad = _route(rec, MOE_TILE, min(MOE_TOKEN_TILE, n))
            xs = _moe_dispatch(pad, pos2d, h, texp.shape[0] * MOE_TILE)
            ys = _moe_ffn(texp, n_used, xs, moe_w_gate, moe_w_up, moe_w_down, j, MOE_TILE, 512)
            x2 = _moe_combine(pos2d, ys, x2, rec, g_final[None, :], i == depth - 1)
    return x2.reshape(bsz, seq, D_MODEL)
```

```python
import functools
import math

import numpy as np
import jax
import jax.numpy as jnp
from jax import lax
from jax.experimental import pallas as pl
from jax.experimental.pallas import tpu as pltpu

F32 = jnp.float32
BF16 = jnp.bfloat16

D_MODEL = 1024
SSM_WIDTH = 512
SSM_GROUP = 16
SSM_GROUPS = 32
SSM_STATE = 64
ATTN_WIDTH = 512
HEAD_DIM = 64
HEADS = 4
HEAD_V = 2 * HEAD_DIM
N_EXPERTS = 8
NORM_EPS = 1e-6

LANES = 128
SUBLANES = 8
KV_CHUNK = 512
TQ = 512
VT_ROWS = HEAD_V + 16
ALIBI_PIECES = 3
LOG2E = math.log2(math.e)
MOE_TILE = 1024
MOE_TOKEN_TILE = 1024
SSM_CHUNK = 16
CHUNK_COLS = SSM_CHUNK * SSM_GROUP
GROUP_BLOCK = LANES // SSM_GROUP
STEP_PAIRS = SSM_CHUNK // 2
VMEM_LIMIT = 56 * 1024 * 1024


def _cparams(sem):
    return pltpu.CompilerParams(dimension_semantics=sem, vmem_limit_bytes=VMEM_LIMIT)


def _rms(x, g):
    return x * lax.rsqrt(jnp.mean(x * x, axis=-1, keepdims=True) + NORM_EPS) * g


def _inproj_kernel(x_ref, g_ref, w_ref, kcols_ref, u_ref, q_ref, k_ref, v_ref, us_ref):
    h = _rms(x_ref[...], g_ref[...]).astype(BF16)
    proj = lambda n: jnp.dot(h, w_ref[:, n * 512:(n + 1) * 512], preferred_element_type=F32)
    z = proj(0)
    n_chunks = us_ref.shape[1] // SSM_CHUNK
    for j in range(SSM_WIDTH // LANES):
        us_ref[j] = z[:, j * LANES:(j + 1) * LANES]
        for s in range(SSM_CHUNK):
            u_ref[s, :, j * LANES:(j + 1) * LANES] = us_ref[j, pl.ds(s, n_chunks, stride=SSM_CHUNK), :].astype(BF16)
    q_ref[...] = (proj(1) * (HEAD_DIM ** -0.5 * LOG2E)).astype(BF16)
    k = proj(2).astype(BF16)
    for hd in range(HEADS):
        k_ref[:, 2 * hd * HEAD_V:(2 * hd + 1) * HEAD_V] = k[:, hd * HEAD_V:(hd + 1) * HEAD_V]
        k_ref[:, (2 * hd + 1) * HEAD_V:(2 * hd + 2) * HEAD_V] = kcols_ref[:, hd * HEAD_V:(hd + 1) * HEAD_V]
    v_ref[...] = proj(3).astype(BF16)


def _inproj(x2, g, w_bf, kcols):
    n = x2.shape[0]
    tm = KV_CHUNK
    out = jax.ShapeDtypeStruct((n, 512), BF16)
    row = lambda c: pl.BlockSpec((tm, c), lambda i: (i, 0))
    return pl.pallas_call(
        _inproj_kernel,
        out_shape=(jax.ShapeDtypeStruct((SSM_CHUNK, n // SSM_CHUNK, SSM_WIDTH), BF16), out,
                   jax.ShapeDtypeStruct((n, 2 * ATTN_WIDTH), BF16), out),
        grid=(n // tm,),
        in_specs=[
            row(D_MODEL),
            pl.BlockSpec((1, D_MODEL), lambda i: (0, 0)),
            pl.BlockSpec((D_MODEL, 2048), lambda i: (0, 0)),
            pl.BlockSpec((tm, ATTN_WIDTH), lambda i: (0, 0)),
        ],
        out_specs=(pl.BlockSpec((SSM_CHUNK, tm // SSM_CHUNK, SSM_WIDTH), lambda i: (0, i, 0)),
                   row(512), row(2 * ATTN_WIDTH), row(512)),
        scratch_shapes=[pltpu.VMEM((SSM_WIDTH // LANES, tm, LANES), F32)],
        compiler_params=_cparams(("parallel",)),
        name="inproj",
    )(x2, g, w_bf, kcols)


def _ssm_matrices(lam_re, lam_im, log_step, b_re, b_im, c_re, c_im, d_skip):
    t = SSM_CHUNK
    tau = jnp.arange(t + 1, dtype=F32)
    ks, ws, es, decs = [], [], [], []
    for d in range(2):
        lam = lax.complex(lam_re[d].astype(F32), lam_im[d].astype(F32))
        step = jnp.exp(log_step[d].astype(F32))[:, None]
        ls = lam * step
        lam_bar = jnp.exp(ls)
        pw = jnp.exp(ls[:, None, :] * tau[None, :, None])
        b_bar = ((lam_bar - 1.0) / lam)[:, :, None] * lax.complex(b_re[d].astype(F32), b_im[d].astype(F32))
        c = lax.complex(c_re[d].astype(F32), c_im[d].astype(F32))
        ks.append(jnp.real(jnp.einsum('gop,gtp,gpi->gtoi', c, pw[:, :t], b_bar)))
        if d == 0:
            wpow = pw[:, :t][:, ::-1]
            epow = pw[:, 1:t + 1]
        else:
            wpow = pw[:, :t]
            epow = pw[:, 1:t + 1][:, ::-1]
        ws.append(jnp.einsum('gsp,gpi->gsip', wpow, b_bar).reshape(SSM_GROUPS, CHUNK_COLS, SSM_STATE))
        es.append(jnp.einsum('gop,gtp->gpto', c, epow).reshape(SSM_GROUPS, SSM_STATE, CHUNK_COLS))
        decs.append(pw[:, t])
    kf, kb = ks
    s_idx = jnp.arange(t)[:, None]
    t_idx = jnp.arange(t)[None, :]
    lag_f = t_idx - s_idx
    lag_b = s_idx - t_idx
    lags = jnp.arange(t)[None, None, :]
    mf = jnp.einsum('stk,gkoi->gstoi', (lag_f[:, :, None] == lags).astype(F32), kf)
    mb = jnp.einsum('stk,gkoi->gstoi', (lag_b[:, :, None] == lags).astype(F32), kb)
    dmat = (jnp.eye(t, dtype=F32)[None, :, :, None, None]
            * (jnp.eye(SSM_GROUP, dtype=F32)[None, None, None] * d_skip.astype(F32).reshape(SSM_GROUPS, 1, 1, 1, SSM_GROUP)))
    m = (mf + mb + dmat).transpose(0, 1, 4, 2, 3).reshape(SSM_GROUPS, CHUNK_COLS, CHUNK_COLS)
    w = jnp.concatenate([jnp.real(ws[0]), jnp.real(ws[1]), jnp.imag(ws[0]), jnp.imag(ws[1])], axis=-1)
    v = jnp.concatenate([jnp.real(es[0]), jnp.real(es[1]), -jnp.imag(es[0]), -jnp.imag(es[1])], axis=1)
    dec_re = jnp.concatenate([jnp.real(decs[0]), jnp.real(decs[1])], axis=-1)[:, None, :]
    dec_im = jnp.concatenate([jnp.imag(decs[0]), jnp.imag(decs[1])], axis=-1)[:, None, :]
    return m.astype(BF16), w.astype(BF16), v.astype(BF16), dec_re, dec_im


def _ssm_kernel(u_ref, m_ref, w_ref, v_ref, ar_ref, ai_ref, y_ref, s_ref, hf_ref, hb_ref, *, n_chunks, rows):
    u = u_ref[0]
    s = jnp.dot(u, w_ref[0], preferred_element_type=F32)
    pitch = n_chunks + 1
    for b in range(rows):
        s_ref[0, b * pitch:b * pitch + n_chunks, :] = s[b * n_chunks:(b + 1) * n_chunks, :LANES]
        s_ref[1, b * pitch:b * pitch + n_chunks, :] = s[b * n_chunks:(b + 1) * n_chunks, LANES:]
    ar = jnp.broadcast_to(ar_ref[0], (rows, LANES))
    ai = jnp.broadcast_to(ai_ref[0], (rows, LANES))

    def body(c, carry):
        fr, fi, br, bi = carry
        at_f = pl.ds(c, rows, stride=pitch)
        at_b = pl.ds(n_chunks - 1 - c, rows, stride=pitch)
        hf_ref[0, at_f, :] = fr
        hf_ref[1, at_f, :] = fi
        hb_ref[0, at_b, :] = br
        hb_ref[1, at_b, :] = bi
        nfr = ar * fr - ai * fi + s_ref[0, at_f, :]
        nfi = ai * fr + ar * fi + s_ref[1, at_f, :]
        nbr = ar * br - ai * bi + s_ref[0, at_b, :]
        nbi = ai * br + ar * bi + s_ref[1, at_b, :]
        return nfr, nfi, nbr, nbi

    z = jnp.zeros((rows, LANES), F32)
    lax.fori_loop(0, n_chunks, body, (z, z, z, z), unroll=4)
    is_fwd = lax.broadcasted_iota(jnp.int32, (1, LANES), 1) < SSM_STATE
    seq_rows = lambda ref, part: jnp.concatenate(
        [ref[part, b * pitch:b * pitch + n_chunks, :] for b in range(rows)], axis=0)
    hcat = jnp.concatenate([jnp.where(is_fwd, seq_rows(hf_ref, 0), seq_rows(hb_ref, 0)),
                            jnp.where(is_fwd, seq_rows(hf_ref, 1), seq_rows(hb_ref, 1))], axis=1).astype(BF16)
    y = jnp.dot(u, m_ref[0], preferred_element_type=F32)
    y = y + jnp.dot(hcat, v_ref[0], preferred_element_type=F32)
    y_ref[0] = jax.nn.gelu(y).astype(BF16)


def _ssm(uc, m, w, v, dec_re, dec_im, n_chunks, rows):
    g, r, _ = uc.shape
    mat = pl.BlockSpec((1, CHUNK_COLS, CHUNK_COLS), lambda i: (i, 0, 0))
    dec = pl.BlockSpec((1, 1, LANES), lambda i: (i, 0, 0))
    return pl.pallas_call(
        functools.partial(_ssm_kernel, n_chunks=n_chunks, rows=rows),
        out_shape=jax.ShapeDtypeStruct((g, r, CHUNK_COLS), BF16),
        grid=(g,),
        in_specs=[pl.BlockSpec((1, r, CHUNK_COLS), lambda i: (i, 0, 0)), mat, mat, mat, dec, dec],
        out_specs=pl.BlockSpec((1, r, CHUNK_COLS), lambda i: (i, 0, 0)),
        scratch_shapes=[pltpu.VMEM((2, rows * (n_chunks + 1), LANES), F32)] * 3,
        compiler_params=_cparams(("parallel",)),
        name="ssm",
    )(uc, m, w, v, dec_re, dec_im)


def _regroup_matrices():
    p = np.zeros((STEP_PAIRS, 2, GROUP_BLOCK, SSM_GROUP, GROUP_BLOCK, SSM_CHUNK, SSM_GROUP), np.float32)
    gl = np.arange(GROUP_BLOCK)[:, None]
    h = np.arange(SSM_GROUP)[None, :]
    for sp in range(STEP_PAIRS):
        for half in range(2):
            p[sp, half, gl, h, gl, sp + half * STEP_PAIRS, h] = 1.0
    p = p.reshape(STEP_PAIRS, 2 * LANES, GROUP_BLOCK, CHUNK_COLS)
    q = p.transpose(0, 2, 3, 1)
    return (jnp.asarray(p.reshape(STEP_PAIRS, 2 * LANES, GROUP_BLOCK * CHUNK_COLS), BF16), jnp.asarray(q, BF16))


def _regroup_in_kernel(us_ref, p_ref, o_ref):
    pairs = [jnp.concatenate([us_ref[sp], us_ref[sp + STEP_PAIRS]], axis=1) for sp in range(STEP_PAIRS)]
    for gl in range(GROUP_BLOCK):
        acc = None
        for sp in range(STEP_PAIRS):
            d = jnp.dot(pairs[sp], p_ref[sp, :, gl * CHUNK_COLS:(gl + 1) * CHUNK_COLS], preferred_element_type=F32)
            acc = d if acc is None else acc + d
        o_ref[gl] = acc.astype(BF16)


def _regroup_out_kernel(yc_ref, q_ref, o_ref):
    for tp in range(STEP_PAIRS):
        acc = None
        for gl in range(GROUP_BLOCK):
            d = jnp.dot(yc_ref[gl], q_ref[tp, gl], preferred_element_type=F32)
            acc = d if acc is None else acc + d
        o_ref[tp] = acc[:, :LANES].astype(BF16)
        o_ref[tp + STEP_PAIRS] = acc[:, LANES:].astype(BF16)


def _regroup_specs(c):
    steps = lambda f: pl.BlockSpec((SSM_CHUNK, c, LANES), f)
    groups = lambda f: pl.BlockSpec((GROUP_BLOCK, c, CHUNK_COLS), f)
    return steps(lambda gb, b: (0, b, gb)), groups(lambda gb, b: (gb, b, 0))


def _regroup_in(us, pmat, bsz):
    c = us.shape[1] // bsz
    steps, groups = _regroup_specs(c)
    return pl.pallas_call(
        _regroup_in_kernel,
        out_shape=jax.ShapeDtypeStruct((SSM_GROUPS, bsz * c, CHUNK_COLS), BF16),
        grid=(SSM_GROUPS // GROUP_BLOCK, bsz),
        in_specs=[steps, pl.BlockSpec(pmat.shape, lambda gb, b: (0, 0, 0))],
        out_specs=groups,
        compiler_params=_cparams(("parallel", "parallel")),
        name="regroup_in",
    )(us, pmat)


def _regroup_out(yc, qmat, bsz):
    c = yc.shape[1] // bsz
    steps, groups = _regroup_specs(c)
    return pl.pallas_call(
        _regroup_out_kernel,
        out_shape=jax.ShapeDtypeStruct((SSM_CHUNK, c * bsz, SSM_WIDTH), BF16),
        grid=(SSM_GROUPS // GROUP_BLOCK, bsz),
        in_specs=[groups, pl.BlockSpec(qmat.shape, lambda gb, b: (0, 0, 0, 0))],
        out_specs=steps,
        compiler_params=_cparams(("parallel", "parallel")),
        name="regroup_out",
    )(yc, qmat)


def _alibi_constants(slopes2):
    bf = jnp.bfloat16
    jj = np.arange(KV_CHUNK, dtype=np.float32)
    rel = np.arange(TQ, dtype=np.float32)[None, :] - jj[:, None]
    key_cols = np.zeros((KV_CHUNK, HEADS, HEAD_V), np.float32)
    for hd, s in enumerate(slopes2):
        rest = np.float32(s) * jj
        for piece in range(ALIBI_PIECES):
            part = rest.astype(bf).astype(np.float32)
            key_cols[:, hd, piece] = part
            rest = rest - part
    tiles = np.stack([np.float32(s) * rel for s in slopes2])
    return jnp.asarray(key_cols.reshape(KV_CHUNK, HEADS * HEAD_V), BF16), jnp.asarray(tiles, F32)


def _attn_kernel(par_ref, q_ref, k_ref, v_ref, bt_ref, g_ref, o_ref, vt_ref, qv_ref, acc_ref, sa_ref, sb_ref,
                 *, n_chunks):
    head = pl.program_id(1)
    qi = pl.program_id(2)
    lam = par_ref[0]
    out_scale = par_ref[1]
    slope = par_ref[2 + head]
    q_pos = slope * lax.broadcasted_iota(jnp.int32, (1, TQ), 1).astype(F32)

    @pl.when(qi == 0)
    def _():
        ones_rows = (lax.broadcasted_iota(jnp.int32, (VT_ROWS - HEAD_V, KV_CHUNK), 0) == 0).astype(BF16)
        for c in range(n_chunks):
            vt_ref[c, :HEAD_V] = v_ref[0, c * KV_CHUNK:(c + 1) * KV_CHUNK, :].astype(F32).T.astype(BF16)
            vt_ref[c, HEAD_V:] = ones_rows

    qt = q_ref[0].astype(F32).T.astype(BF16)
    zero = jnp.zeros((HEAD_DIM, TQ), BF16)
    pick = (lax.broadcasted_iota(jnp.int32, (HEAD_V, TQ), 0) < ALIBI_PIECES).astype(F32)
    for var, sign in enumerate((1.0, -1.0, 0.0)):
        qv_ref[var, :HEAD_DIM] = qt[:HEAD_DIM]
        qv_ref[var, HEAD_DIM:HEAD_V] = zero
        qv_ref[3 + var, :HEAD_DIM] = zero
        qv_ref[3 + var, HEAD_DIM:HEAD_V] = qt[HEAD_DIM:]
        qv_ref[var, HEAD_V:] = (sign * pick).astype(BF16)
        qv_ref[3 + var, HEAD_V:] = (sign * pick).astype(BF16)

    qd = qi // (KV_CHUNK // TQ)
    i0 = qi * TQ

    s_bufs = (sa_ref, sb_ref)

    def scores(t):
        buf = s_bufs[t % 2]
        if t == 0:
            c = qd
            var = 2
            off = jnp.zeros((1, TQ), F32)
            bias = -jnp.abs(bt_ref[0] + slope * (i0 - qd * KV_CHUNK).astype(F32))
        else:
            c = (t - 1) + ((t - 1) >= qd).astype(jnp.int32)
            after = c < qd
            var = jnp.where(after, 0, 1)
            off = -slope * jnp.abs(i0 - c * KV_CHUNK).astype(F32) + jnp.where(after, -q_pos, q_pos)
            bias = None
        k = k_ref[0, pl.ds(pl.multiple_of(c * KV_CHUNK, KV_CHUNK), KV_CHUNK), :]

        def one_map(mi):
            s = jnp.dot(k, qv_ref[3 * mi + var], preferred_element_type=F32)
            if bias is not None:
                s = s + bias
            buf[mi] = s
            return jnp.max(s, axis=0, keepdims=True) + off

        return c, off, one_map

    def accumulate(t, mi, c, c_off, smax, m):
        buf = s_bufs[t % 2]
        m_new = jnp.maximum(m[mi], smax)
        p = jnp.exp2(buf[mi] - (m_new - c_off))
        alpha = jnp.exp2(m[mi] - m_new)
        acc_ref[mi] = alpha * acc_ref[mi] + jnp.dot(vt_ref[c], p.astype(BF16), preferred_element_type=F32)
        m[mi] = m_new

    acc_ref[...] = jnp.zeros_like(acc_ref)
    m = [jnp.full((1, TQ), -1e30, F32)] * 2
    c, off, one_map = scores(0)
    smax = [one_map(0), one_map(1)]
    for t in range(n_chunks):
        if t + 1 < n_chunks:
            c_n, off_n, one_map_n = scores(t + 1)
        smax_n = []
        for mi in range(2):
            if t + 1 < n_chunks:
                smax_n.append(one_map_n(mi))
            accumulate(t, mi, c, off, smax[mi], m)
        if t + 1 < n_chunks:
            c, off, smax = c_n, off_n, smax_n
    o = (acc_ref[0, :HEAD_V] / acc_ref[0, HEAD_V:HEAD_V + 1]
         - lam * (acc_ref[1, :HEAD_V] / acc_ref[1, HEAD_V:HEAD_V + 1]))
    o = o * lax.rsqrt(jnp.mean(o * o, axis=0, keepdims=True) + NORM_EPS)
    o = o * (g_ref[...] * out_scale)
    o_ref[0] = o.T.astype(BF16)


def _attention(par, q, k, v, bias_tiles, g_col):
    b, l, _ = v.shape
    n_chunks = l // KV_CHUNK
    return pl.pallas_call(
        functools.partial(_attn_kernel, n_chunks=n_chunks),
        out_shape=jax.ShapeDtypeStruct((b, l, ATTN_WIDTH), BF16),
        grid=(b, HEADS, l // TQ),
        in_specs=[
            pl.BlockSpec(memory_space=pltpu.SMEM),
            pl.BlockSpec((1, TQ, HEAD_V), lambda bi, h, i: (bi, i, h)),
            pl.BlockSpec((1, l, 2 * HEAD_V), lambda bi, h, i: (bi, 0, h)),
            pl.BlockSpec((1, l, HEAD_V), lambda bi, h, i: (bi, 0, h)),
            pl.BlockSpec((1, KV_CHUNK, TQ), lambda bi, h, i: (h, 0, 0)),
            pl.BlockSpec((HEAD_V, 1), lambda bi, h, i: (0, 0)),
        ],
        out_specs=pl.BlockSpec((1, TQ, HEAD_V), lambda bi, h, i: (bi, i, h)),
        scratch_shapes=[
            pltpu.VMEM((n_chunks, VT_ROWS, KV_CHUNK), BF16),
            pltpu.VMEM((6, 2 * HEAD_V, TQ), BF16),
            pltpu.VMEM((2, VT_ROWS, TQ), F32),
            pltpu.VMEM((2, KV_CHUNK, TQ), F32),
            pltpu.VMEM((2, KV_CHUNK, TQ), F32),
        ],
        compiler_params=_cparams(("parallel", "parallel", "arbitrary")),
        name="diff_attn",
    )(par, q, k, v, bias_tiles, g_col)


def _postmix_kernel(x_ref, ys_ref, ya_ref, wglu_ref, bglu_ref, gs_ref, wout_ref, gf_ref, *rest, router):
    if router:
        wr_ref, br_ref, xo_ref, h_ref, gate_ref, ysc_ref = rest
    else:
        xo_ref, h_ref, ysc_ref = rest
    n_chunks = ys_ref.shape[1]
    for j in range(SSM_WIDTH // LANES):
        for s in range(SSM_CHUNK):
            ysc_ref[j, pl.ds(s, n_chunks, stride=SSM_CHUNK), :] = ys_ref[s, :, j * LANES:(j + 1) * LANES].astype(F32)
    y = jnp.concatenate([ysc_ref[j] for j in range(SSM_WIDTH // LANES)], axis=1)
    t = jnp.dot(y.astype(BF16), wglu_ref[...], preferred_element_type=F32) + bglu_ref[...]
    y = y * jax.nn.sigmoid(t)
    y = _rms(y, gs_ref[...]).astype(BF16)
    mix = jnp.dot(y, wout_ref[:SSM_WIDTH, :], preferred_element_type=F32)
    mix = mix + jnp.dot(ya_ref[...], wout_ref[SSM_WIDTH:, :], preferred_element_type=F32)
    x = x_ref[...] + mix
    xo_ref[...] = x
    h = _rms(x, gf_ref[...])
    h_ref[...] = h.astype(h_ref.dtype)
    if router:
        lane = lax.broadcasted_iota(jnp.int32, (h.shape[0], LANES), 1)
        logits = jnp.broadcast_to(br_ref[...], lane.shape)
        for e in range(N_EXPERTS):
            le = jnp.sum(h * wr_ref[e:e + 1, :], axis=-1, keepdims=True)
            logits = jnp.where(lane == e, logits + le, logits)
        big = jnp.int32(LANES)
        m1 = jnp.max(logits, axis=-1, keepdims=True)
        i1 = jnp.min(jnp.where(logits == m1, lane, big), axis=-1, keepdims=True)
        rest_l = jnp.where(lane == i1, -jnp.inf, logits)
        m2 = jnp.max(rest_l, axis=-1, keepdims=True)
        i2 = jnp.min(jnp.where(rest_l == m2, lane, big), axis=-1, keepdims=True)
        e2 = jnp.exp(m2 - m1)
        w1 = 1.0 / (1.0 + e2)
        w2 = e2 * w1
        rec = jnp.where(lane == 0, w1, jnp.where(lane == 1, w2, jnp.where(
            lane == 2, i1.astype(F32), jnp.where(lane == 3, i2.astype(F32), 0.0))))
        gate_ref[...] = rec[:, :N_EXPERTS]


def _postmix(x2, ys, ya, wglu, bglu, gs, wout, gf, router_w, tm):
    n = x2.shape[0]
    router = router_w is not None
    row = lambda c: pl.BlockSpec((tm, c), lambda i: (i, 0))
    full = lambda r, c: pl.BlockSpec((r, c), lambda i: (0, 0))
    steps = pl.BlockSpec((SSM_CHUNK, tm // SSM_CHUNK, SSM_WIDTH), lambda i: (0, i, 0))
    in_specs = [row(D_MODEL), steps, row(512), full(512, 512), full(1, 512), full(1, 512),
                full(D_MODEL, D_MODEL), full(1, D_MODEL)]
    args = [x2, ys, ya, wglu, bglu, gs, wout, gf]
    out_shape = [jax.ShapeDtypeStruct((n, D_MODEL), F32), jax.ShapeDtypeStruct((n, D_MODEL), F32 if router else BF16)]
    out_specs = [row(D_MODEL), row(D_MODEL)]
    if router:
        in_specs += [full(N_EXPERTS, D_MODEL), full(1, LANES)]
        args += list(router_w)
        out_shape.append(jax.ShapeDtypeStruct((n, N_EXPERTS), F32))
        out_specs.append(row(N_EXPERTS))
    return pl.pallas_call(
        functools.partial(_postmix_kernel, router=router),
        out_shape=tuple(out_shape),
        grid=(n // tm,),
        in_specs=in_specs,
        out_specs=tuple(out_specs),
        scratch_shapes=[pltpu.VMEM((SSM_WIDTH // LANES, tm, LANES), F32)],
        compiler_params=_cparams(("parallel",)),
        name="postmix_router" if router else "postmix",
    )(*args)


def _swiglu_partial(h, wg, wu, wd):
    g = jnp.dot(h, wg, preferred_element_type=F32)
    u = jnp.dot(h, wu, preferred_element_type=F32)
    return jnp.dot((jax.nn.silu(g) * u).astype(BF16), wd, preferred_element_type=F32)


def _ffn_kernel(h_ref, x_ref, wg_ref, wu_ref, wd_ref, o_ref):
    @pl.when(pl.program_id(1) == 0)
    def _():
        o_ref[...] = x_ref[...]

    o_ref[...] += _swiglu_partial(h_ref[...], wg_ref[...], wu_ref[...], wd_ref[...])


def _ffn(h, x2, wg, wu, wd, tm, tf):
    n = x2.shape[0]
    ff = wg.shape[1]
    row = lambda c: pl.BlockSpec((tm, c), lambda i, f: (i, 0))
    return pl.pallas_call(
        _ffn_kernel,
        out_shape=jax.ShapeDtypeStruct((n, D_MODEL), F32),
        grid=(n // tm, ff // tf),
        in_specs=[row(D_MODEL), row(D_MODEL),
                  pl.BlockSpec((D_MODEL, tf), lambda i, f: (0, f)),
                  pl.BlockSpec((D_MODEL, tf), lambda i, f: (0, f)),
                  pl.BlockSpec((tf, D_MODEL), lambda i, f: (f, 0))],
        out_specs=row(D_MODEL),
        compiler_params=_cparams(("parallel", "arbitrary")),
        name="ffn_dense",
    )(h, x2, wg, wu, wd)


def _route(rec, tm, tc):
    n = rec.shape[0]
    a = 2 * n
    n_tiles = a // tm + N_EXPERTS
    e = rec[:, 2:4].astype(jnp.int32).reshape(a)
    onehot = (e[:, None] == jnp.arange(N_EXPERTS, dtype=jnp.int32)[None, :]).astype(jnp.int32)
    csum = jnp.cumsum(onehot, axis=0)
    rank = jnp.sum(csum * onehot, axis=1) - 1
    count = csum[-1]
    padded = (count + tm - 1) // tm * tm
    ends = jnp.cumsum(padded)
    pad_start = ends - padded + count
    pad_aligned = (pad_start + SUBLANES - 1) // SUBLANES * SUBLANES
    pad = jnp.concatenate([pad_start, pad_aligned, ends - pad_aligned, ends[-1:] // tm]).astype(jnp.int32)
    pos = jnp.sum(onehot * (ends - padded)[None, :], axis=1) + rank
    n_used = (ends[-1] // tm).astype(jnp.int32)
    tile = jnp.arange(n_tiles, dtype=jnp.int32)
    texp = jnp.sum((tile[:, None] >= (ends // tm)[None, :]).astype(jnp.int32), axis=1)
    texp = jnp.minimum(texp, N_EXPERTS - 1)
    texp = jnp.where(tile < n_used, texp, texp[jnp.maximum(n_used - 1, 0)])
    pos2d = pos.reshape(n // tc, tc, 2).transpose(0, 2, 1).reshape(n // tc, 2 * tc)
    return pos2d, texp, n_used.reshape(1), pad


def _fetch_rows_index(pos_hbm, idx_smem, sem_idx):
    cp = pltpu.make_async_copy(pos_hbm.at[pl.program_id(0)], idx_smem, sem_idx)
    cp.start()
    cp.wait()


def _moe_dispatch_kernel(pad_ref, pos_hbm, h_ref, xs_hbm, idx_smem, zeros_ref, sem_idx, sem, sem_pad):
    @pl.when(pl.program_id(0) == 0)
    def _():
        zeros_ref[...] = jnp.zeros_like(zeros_ref)

        def padding_copies(act):
            zero_rows = lambda off, size: pltpu.make_async_copy(
                zeros_ref.at[pl.ds(0, size)], xs_hbm.at[pl.ds(off, size)], sem_pad)
            for e in range(N_EXPERTS):
                first = pad_ref[e]
                aligned = pad_ref[N_EXPERTS + e]
                length = pad_ref[2 * N_EXPERTS + e]
                for j in range(SUBLANES - 1):
                    @pl.when(first + j < aligned)
                    def _(row=first + j):
                        act(zero_rows(row, 1))

                off = aligned
                for bit in reversed(range(3, MOE_TILE.bit_length() - 1)):
                    size = 1 << bit

                    @pl.when((length & size) != 0)
                    def _(off=off, size=size):
                        act(zero_rows(pl.multiple_of(off, SUBLANES), size))

                    off = off + (length & size)
            n_tiles = xs_hbm.shape[0] // MOE_TILE
            half = zeros_ref.shape[0]
            for t in range(n_tiles - N_EXPERTS, n_tiles):
                @pl.when(t >= pad_ref[3 * N_EXPERTS])
                def _(t=t):
                    for part in range(MOE_TILE // half):
                        act(zero_rows(t * MOE_TILE + part * half, half))

        padding_copies(lambda cp: cp.start())
        padding_copies(lambda cp: cp.wait())

    _fetch_rows_index(pos_hbm, idx_smem, sem_idx)
    tc = h_ref.shape[0]

    def issue(r, carry):
        src = h_ref.at[pl.ds(r, 1)]
        pltpu.make_async_copy(src, xs_hbm.at[pl.ds(idx_smem[r], 1)], sem).start()
        pltpu.make_async_copy(src, xs_hbm.at[pl.ds(idx_smem[tc + r], 1)], sem).start()
        return carry

    lax.fori_loop(0, tc, issue, 0, unroll=8)
    for _ in range(2):
        pltpu.make_async_copy(h_ref, xs_hbm.at[pl.ds(0, tc)], sem).wait()


def _moe_dispatch(pad, pos2d, h, rows):
    n = h.shape[0]
    tc = pos2d.shape[1] // 2
    return pl.pallas_call(
        _moe_dispatch_kernel,
        out_shape=jax.ShapeDtypeStruct((rows, D_MODEL), F32),
        grid_spec=pltpu.PrefetchScalarGridSpec(
            num_scalar_prefetch=1,
            grid=(n // tc,),
            in_specs=[pl.BlockSpec(memory_space=pl.ANY), pl.BlockSpec((tc, D_MODEL), lambda i, pad: (i, 0))],
            out_specs=pl.BlockSpec(memory_space=pl.ANY),
            scratch_shapes=[pltpu.SMEM((2 * tc,), jnp.int32), pltpu.VMEM((MOE_TILE // 2, D_MODEL), F32),
                            pltpu.SemaphoreType.DMA, pltpu.SemaphoreType.DMA, pltpu.SemaphoreType.DMA],
        ),
        compiler_params=_cparams(("arbitrary",)),
        name="moe_dispatch",
    )(pad, pos2d, h)


def _moe_ffn_kernel(texp_ref, nused_ref, xs_ref, wg_ref, wu_ref, wd_ref, o_ref, xb_ref):
    t = pl.program_id(0)
    f = pl.program_id(1)
    last = f == pl.num_programs(1) - 1
    valid = t < nused_ref[0]

    @pl.when(valid)
    def _():
        @pl.when(f == 0)
        def _():
            xb_ref[...] = xs_ref[...].astype(BF16)
            o_ref[...] = jnp.zeros_like(o_ref)

        o_ref[...] += _swiglu_partial(xb_ref[...], wg_ref[0, 0].astype(BF16), wu_ref[0, 0].astype(BF16),
                                      wd_ref[0, 0].astype(BF16))

    @pl.when(jnp.logical_and(jnp.logical_not(valid), last))
    def _():
        o_ref[...] = jnp.zeros_like(o_ref)


def _moe_ffn(texp, n_used, xs, wg, wu, wd, layer, tm, tf):
    rows = xs.shape[0]
    ff = wg.shape[-1]
    nf = ff // tf
    fidx = lambda t, f, nu: jnp.where(t < nu[0], f, nf - 1)
    row = pl.BlockSpec((tm, D_MODEL), lambda t, f, te, nu: (t, 0))
    xs_row = pl.BlockSpec((tm, D_MODEL), lambda t, f, te, nu: (jnp.minimum(t, jnp.maximum(nu[0] - 1, 0)), 0))
    return pl.pallas_call(
        _moe_ffn_kernel,
        out_shape=jax.ShapeDtypeStruct((rows, D_MODEL), F32),
        grid_spec=pltpu.PrefetchScalarGridSpec(
            num_scalar_prefetch=2,
            grid=(rows // tm, nf),
            in_specs=[xs_row,
                      pl.BlockSpec((1, 1, D_MODEL, tf), lambda t, f, te, nu: (layer, te[t], 0, fidx(t, f, nu))),
                      pl.BlockSpec((1, 1, D_MODEL, tf), lambda t, f, te, nu: (layer, te[t], 0, fidx(t, f, nu))),
                      pl.BlockSpec((1, 1, tf, D_MODEL), lambda t, f, te, nu: (layer, te[t], fidx(t, f, nu), 0))],
            out_specs=row,
            scratch_shapes=[pltpu.VMEM((tm, D_MODEL), BF16)],
        ),
        compiler_params=_cparams(("arbitrary", "arbitrary")),
        name="moe_ffn",
    )(texp, n_used, xs, wg, wu, wd)


def _moe_combine_kernel(pos_hbm, ys_hbm, x_ref, rec_ref, gfin_ref, o_ref, idx_smem, buf, sem_idx, sem, *, final):
    _fetch_rows_index(pos_hbm, idx_smem, sem_idx)
    count = buf.shape[0]

    def issue(r, carry):
        pltpu.make_async_copy(ys_hbm.at[pl.ds(idx_smem[r], 1)], buf.at[pl.ds(r, 1)], sem).start()
        return carry

    lax.fori_loop(0, count, issue, 0, unroll=8)
    pltpu.make_async_copy(ys_hbm.at[pl.ds(0, count)], buf, sem).wait()
    tc = x_ref.shape[0]
    rec = rec_ref[...]
    out = x_ref[...] + rec[:, 0:1] * buf[:tc] + rec[:, 1:2] * buf[tc:]
    if final:
        out = _rms(out, gfin_ref[...])
    o_ref[...] = out


def _moe_combine(pos2d, ys, x2, rec, gfin, final):
    n = x2.shape[0]
    tc = pos2d.shape[1] // 2
    row = lambda c: pl.BlockSpec((tc, c), lambda i: (i, 0))
    return pl.pallas_call(
        functools.partial(_moe_combine_kernel, final=final),
        out_shape=jax.ShapeDtypeStruct((n, D_MODEL), F32),
        grid=(n // tc,),
        in_specs=[pl.BlockSpec(memory_space=pl.ANY), pl.BlockSpec(memory_space=pl.ANY), row(D_MODEL),
                  row(N_EXPERTS), pl.BlockSpec((1, D_MODEL), lambda i: (0, 0))],
        out_specs=row(D_MODEL),
        scratch_shapes=[pltpu.SMEM((2 * tc,), jnp.int32), pltpu.VMEM((2 * tc, D_MODEL), F32),
                        pltpu.SemaphoreType.DMA, pltpu.SemaphoreType.DMA],
        compiler_params=_cparams(("arbitrary",)),
        name="moe_combine",
    )(pos2d, ys, x2, rec, gfin)


def _pad_axis(a, axis, size):
    pad = [(0, 0)] * a.ndim
    pad[axis] = (0, size - a.shape[axis])
    return jnp.pad(a, pad)


def kernel(x, g_mix, w_in, ssm_lambda_re, ssm_lambda_im, ssm_log_step, ssm_b_re, ssm_b_im, ssm_c_re, ssm_c_im, ssm_d, w_glu, b_glu, g_ssm_out, lambda_q1, lambda_k1, lambda_q2, lambda_k2, g_subln, w_out, g_ffn, dense_w_gate, dense_w_up, dense_w_down, w_router, b_router, moe_w_gate, moe_w_up, moe_w_down, g_final):
    bsz, seq, _ = x.shape
    n = bsz * seq
    depth = w_in.shape[0]
    n_chunks = seq // SSM_CHUNK
    assert seq % KV_CHUNK == 0 and bsz == SUBLANES and depth % 2 == 0 and (2 * n) % MOE_TILE == 0
    tm = min(512, n)
    d_ff = dense_w_gate.shape[-1]
    d_ff_pad = -(-d_ff // 256) * 256
    slopes2 = [2.0 ** (-8.0 * (h + 1) / HEADS) * LOG2E for h in range(HEADS)]
    kcols, bias_tiles = _alibi_constants(slopes2)
    pmat, qmat = _regroup_matrices()

    x2 = x.reshape(n, D_MODEL)
    for i in range(depth):
        lambda_init = 0.8 - 0.6 * math.exp(-0.3 * i)
        u, q, k, v = _inproj(x2, g_mix[i][None, :], w_in[i].astype(BF16), kcols)

        m, w, vv, dec_re, dec_im = _ssm_matrices(
            ssm_lambda_re[i], ssm_lambda_im[i], ssm_log_step[i], ssm_b_re[i], ssm_b_im[i],
            ssm_c_re[i], ssm_c_im[i], ssm_d[i])
        uc = _regroup_in(u, pmat, bsz)
        yc = _ssm(uc, m, w, vv, dec_re, dec_im, n_chunks, bsz)
        ys = _regroup_out(yc, qmat, bsz)

        lam = (jnp.exp(jnp.sum(lambda_q1[i].astype(F32) * lambda_k1[i].astype(F32)))
               - jnp.exp(jnp.sum(lambda_q2[i].astype(F32) * lambda_k2[i].astype(F32))) + lambda_init)
        par = jnp.concatenate([jnp.stack([lam, jnp.asarray(1.0 - lambda_init, F32)]),
                               jnp.asarray(slopes2, F32), jnp.zeros((2,), F32)])
        ya = _attention(par, q.reshape(bsz, seq, ATTN_WIDTH), k.reshape(bsz, seq, 2 * ATTN_WIDTH),
                        v.reshape(bsz, seq, ATTN_WIDTH), bias_tiles, g_subln[i].astype(F32)[:, None])
        ya = ya.reshape(n, ATTN_WIDTH)

        j = i // 2
        if i % 2 == 0:
            router_w = None
        else:
            router_w = (w_router[j].astype(F32).T,
                        jnp.concatenate([b_router[j].astype(F32),
                                         jnp.full((LANES - N_EXPERTS,), -jnp.inf, F32)])[None, :])
        outs = _postmix(x2, ys, ya, w_glu[i].astype(BF16), b_glu[i][None, :], g_ssm_out[i][None, :],
                        w_out[i].astype(BF16), g_ffn[i][None, :], router_w, tm)
        if i % 2 == 0:
            x2, h = outs
            wg = _pad_axis(dense_w_gate[j].astype(BF16), 1, d_ff_pad)
            wu = _pad_axis(dense_w_up[j].astype(BF16), 1, d_ff_pad)
            wd = _pad_axis(dense_w_down[j].astype(BF16), 0, d_ff_pad)
            x2 = _ffn(h, x2, wg, wu, wd, tm, d_ff_pad // 2)
        else:
            x2, h, rec = outs
            pos2d, texp, n_used, pad = _route(rec, MOE_TILE, min(MOE_TOKEN_TILE, n))
            xs = _moe_dispatch(pad, pos2d, h, texp.shape[0] * MOE_TILE)
            ys = _moe_ffn(texp, n_used, xs, moe_w_gate, moe_w_up, moe_w_down, j, MOE_TILE, 512)
            x2 = _moe_combine(pos2d, ys, x2, rec, g_final[None, :], i == depth - 1)
    return x2.reshape(bsz, seq, D_MODEL)
```

```python
import functools
import math

import numpy as np
import jax
import jax.numpy as jnp
from jax import lax
from jax.experimental import pallas as pl
from jax.experimental.pallas import tpu as pltpu

F32 = jnp.float32
BF16 = jnp.bfloat16

D_MODEL = 1024
SSM_WIDTH = 512
SSM_GROUP = 16
SSM_GROUPS = 32
SSM_STATE = 64
ATTN_WIDTH = 512
HEAD_DIM = 64
HEADS = 4
HEAD_V = 2 * HEAD_DIM
N_EXPERTS = 8
NORM_EPS = 1e-6

LANES = 128
SUBLANES = 8
KV_CHUNK = 512
TQ = 512
VT_ROWS = HEAD_V + 16
ALIBI_PIECES = 3
LOG2E = math.log2(math.e)
MOE_TILE = 1024
ROWS_PER_ISSUE = 8
MOE_TOKEN_TILE = 1024
SSM_CHUNK = 16
CHUNK_COLS = SSM_CHUNK * SSM_GROUP
GROUP_BLOCK = LANES // SSM_GROUP
STEP_PAIRS = SSM_CHUNK // 2
VMEM_LIMIT = 56 * 1024 * 1024


def _cparams(sem):
    return pltpu.CompilerParams(dimension_semantics=sem, vmem_limit_bytes=VMEM_LIMIT)


def _rms(x, g):
    return x * lax.rsqrt(jnp.mean(x * x, axis=-1, keepdims=True) + NORM_EPS) * g


def _inproj_kernel(x_ref, g_ref, w_ref, kcols_ref, u_ref, q_ref, k_ref, v_ref, us_ref):
    h = _rms(x_ref[...], g_ref[...]).astype(BF16)
    proj = lambda n: jnp.dot(h, w_ref[:, n * 512:(n + 1) * 512], preferred_element_type=F32)
    z = proj(0)
    n_chunks = us_ref.shape[1] // SSM_CHUNK
    for j in range(SSM_WIDTH // LANES):
        us_ref[j] = z[:, j * LANES:(j + 1) * LANES]
        for s in range(SSM_CHUNK):
            u_ref[s, :, j * LANES:(j + 1) * LANES] = us_ref[j, pl.ds(s, n_chunks, stride=SSM_CHUNK), :].astype(BF16)
    q_ref[...] = (proj(1) * (HEAD_DIM ** -0.5 * LOG2E)).astype(BF16)
    k = proj(2).astype(BF16)
    for hd in range(HEADS):
        k_ref[:, 2 * hd * HEAD_V:(2 * hd + 1) * HEAD_V] = k[:, hd * HEAD_V:(hd + 1) * HEAD_V]
        k_ref[:, (2 * hd + 1) * HEAD_V:(2 * hd + 2) * HEAD_V] = kcols_ref[:, hd * HEAD_V:(hd + 1) * HEAD_V]
    v_ref[...] = proj(3).astype(BF16)


def _inproj(x2, g, w_bf, kcols):
    n = x2.shape[0]
    tm = KV_CHUNK
    out = jax.ShapeDtypeStruct((n, 512), BF16)
    row = lambda c: pl.BlockSpec((tm, c), lambda i: (i, 0))
    return pl.pallas_call(
        _inproj_kernel,
        out_shape=(jax.ShapeDtypeStruct((SSM_CHUNK, n // SSM_CHUNK, SSM_WIDTH), BF16), out,
                   jax.ShapeDtypeStruct((n, 2 * ATTN_WIDTH), BF16), out),
        grid=(n // tm,),
        in_specs=[
            row(D_MODEL),
            pl.BlockSpec((1, D_MODEL), lambda i: (0, 0)),
            pl.BlockSpec((D_MODEL, 2048), lambda i: (0, 0)),
            pl.BlockSpec((tm, ATTN_WIDTH), lambda i: (0, 0)),
        ],
        out_specs=(pl.BlockSpec((SSM_CHUNK, tm // SSM_CHUNK, SSM_WIDTH), lambda i: (0, i, 0)),
                   row(512), row(2 * ATTN_WIDTH), row(512)),
        scratch_shapes=[pltpu.VMEM((SSM_WIDTH // LANES, tm, LANES), F32)],
        compiler_params=_cparams(("parallel",)),
        name="inproj",
    )(x2, g, w_bf, kcols)


def _ssm_matrices(lam_re, lam_im, log_step, b_re, b_im, c_re, c_im, d_skip):
    t = SSM_CHUNK
    tau = jnp.arange(t + 1, dtype=F32)
    ks, ws, es, decs = [], [], [], []
    for d in range(2):
        lam = lax.complex(lam_re[d].astype(F32), lam_im[d].astype(F32))
        step = jnp.exp(log_step[d].astype(F32))[:, None]
        ls = lam * step
        lam_bar = jnp.exp(ls)
        pw = jnp.exp(ls[:, None, :] * tau[None, :, None])
        b_bar = ((lam_bar - 1.0) / lam)[:, :, None] * lax.complex(b_re[d].astype(F32), b_im[d].astype(F32))
        c = lax.complex(c_re[d].astype(F32), c_im[d].astype(F32))
        ks.append(jnp.real(jnp.einsum('gop,gtp,gpi->gtoi', c, pw[:, :t], b_bar)))
        if d == 0:
            wpow = pw[:, :t][:, ::-1]
            epow = pw[:, 1:t + 1]
        else:
            wpow = pw[:, :t]
            epow = pw[:, 1:t + 1][:, ::-1]
        ws.append(jnp.einsum('gsp,gpi->gsip', wpow, b_bar).reshape(SSM_GROUPS, CHUNK_COLS, SSM_STATE))
        es.append(jnp.einsum('gop,gtp->gpto', c, epow).reshape(SSM_GROUPS, SSM_STATE, CHUNK_COLS))
        decs.append(pw[:, t])
    kf, kb = ks
    s_idx = jnp.arange(t)[:, None]
    t_idx = jnp.arange(t)[None, :]
    lag_f = t_idx - s_idx
    lag_b = s_idx - t_idx
    lags = jnp.arange(t)[None, None, :]
    mf = jnp.einsum('stk,gkoi->gstoi', (lag_f[:, :, None] == lags).astype(F32), kf)
    mb = jnp.einsum('stk,gkoi->gstoi', (lag_b[:, :, None] == lags).astype(F32), kb)
    dmat = (jnp.eye(t, dtype=F32)[None, :, :, None, None]
            * (jnp.eye(SSM_GROUP, dtype=F32)[None, None, None] * d_skip.astype(F32).reshape(SSM_GROUPS, 1, 1, 1, SSM_GROUP)))
    m = (mf + mb + dmat).transpose(0, 1, 4, 2, 3).reshape(SSM_GROUPS, CHUNK_COLS, CHUNK_COLS)
    w = jnp.concatenate([jnp.real(ws[0]), jnp.real(ws[1]), jnp.imag(ws[0]), jnp.imag(ws[1])], axis=-1)
    v = jnp.concatenate([jnp.real(es[0]), jnp.real(es[1]), -jnp.imag(es[0]), -jnp.imag(es[1])], axis=1)
    dec_re = jnp.concatenate([jnp.real(decs[0]), jnp.real(decs[1])], axis=-1)[:, None, :]
    dec_im = jnp.concatenate([jnp.imag(decs[0]), jnp.imag(decs[1])], axis=-1)[:, None, :]
    return m.astype(BF16), w.astype(BF16), v.astype(BF16), dec_re, dec_im


def _ssm_kernel(u_ref, m_ref, w_ref, v_ref, ar_ref, ai_ref, y_ref, s_ref, hf_ref, hb_ref, *, n_chunks, rows):
    u = u_ref[0]
    s = jnp.dot(u, w_ref[0], preferred_element_type=F32)
    pitch = n_chunks + 1
    for b in range(rows):
        s_ref[0, b * pitch:b * pitch + n_chunks, :] = s[b * n_chunks:(b + 1) * n_chunks, :LANES]
        s_ref[1, b * pitch:b * pitch + n_chunks, :] = s[b * n_chunks:(b + 1) * n_chunks, LANES:]
    ar = jnp.broadcast_to(ar_ref[0], (rows, LANES))
    ai = jnp.broadcast_to(ai_ref[0], (rows, LANES))

    def body(c, carry):
        fr, fi, br, bi = carry
        at_f = pl.ds(c, rows, stride=pitch)
        at_b = pl.ds(n_chunks - 1 - c, rows, stride=pitch)
        hf_ref[0, at_f, :] = fr
        hf_ref[1, at_f, :] = fi
        hb_ref[0, at_b, :] = br
        hb_ref[1, at_b, :] = bi
        nfr = ar * fr - ai * fi + s_ref[0, at_f, :]
        nfi = ai * fr + ar * fi + s_ref[1, at_f, :]
        nbr = ar * br - ai * bi + s_ref[0, at_b, :]
        nbi = ai * br + ar * bi + s_ref[1, at_b, :]
        return nfr, nfi, nbr, nbi

    z = jnp.zeros((rows, LANES), F32)
    lax.fori_loop(0, n_chunks, body, (z, z, z, z), unroll=4)
    is_fwd = lax.broadcasted_iota(jnp.int32, (1, LANES), 1) < SSM_STATE
    seq_rows = lambda ref, part: jnp.concatenate(
        [ref[part, b * pitch:b * pitch + n_chunks, :] for b in range(rows)], axis=0)
    hcat = jnp.concatenate([jnp.where(is_fwd, seq_rows(hf_ref, 0), seq_rows(hb_ref, 0)),
                            jnp.where(is_fwd, seq_rows(hf_ref, 1), seq_rows(hb_ref, 1))], axis=1).astype(BF16)
    y = jnp.dot(u, m_ref[0], preferred_element_type=F32)
    y = y + jnp.dot(hcat, v_ref[0], preferred_element_type=F32)
    y_ref[0] = jax.nn.gelu(y).astype(BF16)


def _ssm(uc, m, w, v, dec_re, dec_im, n_chunks, rows):
    g, r, _ = uc.shape
    mat = pl.BlockSpec((1, CHUNK_COLS, CHUNK_COLS), lambda i: (i, 0, 0))
    dec = pl.BlockSpec((1, 1, LANES), lambda i: (i, 0, 0))
    return pl.pallas_call(
        functools.partial(_ssm_kernel, n_chunks=n_chunks, rows=rows),
        out_shape=jax.ShapeDtypeStruct((g, r, CHUNK_COLS), BF16),
        grid=(g,),
        in_specs=[pl.BlockSpec((1, r, CHUNK_COLS), lambda i: (i, 0, 0)), mat, mat, mat, dec, dec],
        out_specs=pl.BlockSpec((1, r, CHUNK_COLS), lambda i: (i, 0, 0)),
        scratch_shapes=[pltpu.VMEM((2, rows * (n_chunks + 1), LANES), F32)] * 3,
        compiler_params=_cparams(("parallel",)),
        name="ssm",
    )(uc, m, w, v, dec_re, dec_im)


def _regroup_matrices():
    p = np.zeros((STEP_PAIRS, 2, GROUP_BLOCK, SSM_GROUP, GROUP_BLOCK, SSM_CHUNK, SSM_GROUP), np.float32)
    gl = np.arange(GROUP_BLOCK)[:, None]
    h = np.arange(SSM_GROUP)[None, :]
    for sp in range(STEP_PAIRS):
        for half in range(2):
            p[sp, half, gl, h, gl, sp + half * STEP_PAIRS, h] = 1.0
    p = p.reshape(STEP_PAIRS, 2 * LANES, GROUP_BLOCK, CHUNK_COLS)
    q = p.transpose(0, 2, 3, 1)
    return (jnp.asarray(p.reshape(STEP_PAIRS, 2 * LANES, GROUP_BLOCK * CHUNK_COLS), BF16), jnp.asarray(q, BF16))


def _regroup_in_kernel(us_ref, p_ref, o_ref):
    pairs = [jnp.concatenate([us_ref[sp], us_ref[sp + STEP_PAIRS]], axis=1) for sp in range(STEP_PAIRS)]
    for gl in range(GROUP_BLOCK):
        acc = None
        for sp in range(STEP_PAIRS):
            d = jnp.dot(pairs[sp], p_ref[sp, :, gl * CHUNK_COLS:(gl + 1) * CHUNK_COLS], preferred_element_type=F32)
            acc = d if acc is None else acc + d
        o_ref[gl] = acc.astype(BF16)


def _regroup_out_kernel(yc_ref, q_ref, o_ref):
    for tp in range(STEP_PAIRS):
        acc = None
        for gl in range(GROUP_BLOCK):
            d = jnp.dot(yc_ref[gl], q_ref[tp, gl], preferred_element_type=F32)
            acc = d if acc is None else acc + d
        o_ref[tp] = acc[:, :LANES].astype(BF16)
        o_ref[tp + STEP_PAIRS] = acc[:, LANES:].astype(BF16)


def _regroup_specs(c):
    steps = lambda f: pl.BlockSpec((SSM_CHUNK, c, LANES), f)
    groups = lambda f: pl.BlockSpec((GROUP_BLOCK, c, CHUNK_COLS), f)
    return steps(lambda gb, b: (0, b, gb)), groups(lambda gb, b: (gb, b, 0))


def _regroup_in(us, pmat, bsz):
    c = us.shape[1] // bsz
    steps, groups = _regroup_specs(c)
    return pl.pallas_call(
        _regroup_in_kernel,
        out_shape=jax.ShapeDtypeStruct((SSM_GROUPS, bsz * c, CHUNK_COLS), BF16),
        grid=(SSM_GROUPS // GROUP_BLOCK, bsz),
        in_specs=[steps, pl.BlockSpec(pmat.shape, lambda gb, b: (0, 0, 0))],
        out_specs=groups,
        compiler_params=_cparams(("parallel", "parallel")),
        name="regroup_in",
    )(us, pmat)


def _regroup_out(yc, qmat, bsz):
    c = yc.shape[1] // bsz
    steps, groups = _regroup_specs(c)
    return pl.pallas_call(
        _regroup_out_kernel,
        out_shape=jax.ShapeDtypeStruct((SSM_CHUNK, c * bsz, SSM_WIDTH), BF16),
        grid=(SSM_GROUPS // GROUP_BLOCK, bsz),
        in_specs=[groups, pl.BlockSpec(qmat.shape, lambda gb, b: (0, 0, 0, 0))],
        out_specs=steps,
        compiler_params=_cparams(("parallel", "parallel")),
        name="regroup_out",
    )(yc, qmat)


def _alibi_constants(slopes2):
    bf = jnp.bfloat16
    jj = np.arange(KV_CHUNK, dtype=np.float32)
    rel = np.arange(TQ, dtype=np.float32)[None, :] - jj[:, None]
    key_cols = np.zeros((KV_CHUNK, HEADS, HEAD_V), np.float32)
    for hd, s in enumerate(slopes2):
        rest = np.float32(s) * jj
        for piece in range(ALIBI_PIECES):
            part = rest.astype(bf).astype(np.float32)
            key_cols[:, hd, piece] = part
            rest = rest - part
    tiles = np.stack([np.float32(s) * rel for s in slopes2])
    return jnp.asarray(key_cols.reshape(KV_CHUNK, HEADS * HEAD_V), BF16), jnp.asarray(tiles, F32)


def _attn_kernel(par_ref, q_ref, k_ref, v_ref, bt_ref, g_ref, o_ref, vt_ref, qv_ref, acc_ref, sa_ref, sb_ref,
                 *, n_chunks):
    head = pl.program_id(1)
    qi = pl.program_id(2)
    lam = par_ref[0]
    out_scale = par_ref[1]
    slope = par_ref[2 + head]
    q_pos = slope * lax.broadcasted_iota(jnp.int32, (1, TQ), 1).astype(F32)

    @pl.when(qi == 0)
    def _():
        ones_rows = (lax.broadcasted_iota(jnp.int32, (VT_ROWS - HEAD_V, KV_CHUNK), 0) == 0).astype(BF16)
        for c in range(n_chunks):
            vt_ref[c, :HEAD_V] = v_ref[0, c * KV_CHUNK:(c + 1) * KV_CHUNK, :].astype(F32).T.astype(BF16)
            vt_ref[c, HEAD_V:] = ones_rows

    qt = q_ref[0].astype(F32).T.astype(BF16)
    zero = jnp.zeros((HEAD_DIM, TQ), BF16)
    pick = (lax.broadcasted_iota(jnp.int32, (HEAD_V, TQ), 0) < ALIBI_PIECES).astype(F32)
    for var, sign in enumerate((1.0, -1.0, 0.0)):
        qv_ref[var, :HEAD_DIM] = qt[:HEAD_DIM]
        qv_ref[var, HEAD_DIM:HEAD_V] = zero
        qv_ref[3 + var, :HEAD_DIM] = zero
        qv_ref[3 + var, HEAD_DIM:HEAD_V] = qt[HEAD_DIM:]
        qv_ref[var, HEAD_V:] = (sign * pick).astype(BF16)
        qv_ref[3 + var, HEAD_V:] = (sign * pick).astype(BF16)

    qd = qi // (KV_CHUNK // TQ)
    i0 = qi * TQ

    s_bufs = (sa_ref, sb_ref)

    def scores(t):
        buf = s_bufs[t % 2]
        if t == 0:
            c = qd
            var = 2
            off = jnp.zeros((1, TQ), F32)
            bias = -jnp.abs(bt_ref[0] + slope * (i0 - qd * KV_CHUNK).astype(F32))
        else:
            c = (t - 1) + ((t - 1) >= qd).astype(jnp.int32)
            after = c < qd
            var = jnp.where(after, 0, 1)
            off = -slope * jnp.abs(i0 - c * KV_CHUNK).astype(F32) + jnp.where(after, -q_pos, q_pos)
            bias = None
        k = k_ref[0, pl.ds(pl.multiple_of(c * KV_CHUNK, KV_CHUNK), KV_CHUNK), :]

        def one_map(mi):
            s = jnp.dot(k, qv_ref[3 * mi + var], preferred_element_type=F32)
            if bias is not None:
                s = s + bias
            buf[mi] = s
            return jnp.max(s, axis=0, keepdims=True) + off

        return c, off, one_map

    def accumulate(t, mi, c, c_off, smax, m):
        buf = s_bufs[t % 2]
        m_new = jnp.maximum(m[mi], smax)
        p = jnp.exp2(buf[mi] - (m_new - c_off))
        alpha = jnp.exp2(m[mi] - m_new)
        acc_ref[mi] = alpha * acc_ref[mi] + jnp.dot(vt_ref[c], p.astype(BF16), preferred_element_type=F32)
        m[mi] = m_new

    acc_ref[...] = jnp.zeros_like(acc_ref)
    m = [jnp.full((1, TQ), -1e30, F32)] * 2
    c, off, one_map = scores(0)
    smax = [one_map(0), one_map(1)]
    for t in range(n_chunks):
        if t + 1 < n_chunks:
            c_n, off_n, one_map_n = scores(t + 1)
        smax_n = []
        for mi in range(2):
            if t + 1 < n_chunks:
                smax_n.append(one_map_n(mi))
            accumulate(t, mi, c, off, smax[mi], m)
        if t + 1 < n_chunks:
            c, off, smax = c_n, off_n, smax_n
    o = (acc_ref[0, :HEAD_V] / acc_ref[0, HEAD_V:HEAD_V + 1]
         - lam * (acc_ref[1, :HEAD_V] / acc_ref[1, HEAD_V:HEAD_V + 1]))
    o = o * lax.rsqrt(jnp.mean(o * o, axis=0, keepdims=True) + NORM_EPS)
    o = o * (g_ref[...] * out_scale)
    o_ref[0] = o.T.astype(BF16)


def _attention(par, q, k, v, bias_tiles, g_col):
    b, l, _ = v.shape
    n_chunks = l // KV_CHUNK
    return pl.pallas_call(
        functools.partial(_attn_kernel, n_chunks=n_chunks),
        out_shape=jax.ShapeDtypeStruct((b, l, ATTN_WIDTH), BF16),
        grid=(b, HEADS, l // TQ),
        in_specs=[
            pl.BlockSpec(memory_space=pltpu.SMEM),
            pl.BlockSpec((1, TQ, HEAD_V), lambda bi, h, i: (bi, i, h)),
            pl.BlockSpec((1, l, 2 * HEAD_V), lambda bi, h, i: (bi, 0, h)),
            pl.BlockSpec((1, l, HEAD_V), lambda bi, h, i: (bi, 0, h)),
            pl.BlockSpec((1, KV_CHUNK, TQ), lambda bi, h, i: (h, 0, 0)),
            pl.BlockSpec((HEAD_V, 1), lambda bi, h, i: (0, 0)),
        ],
        out_specs=pl.BlockSpec((1, TQ, HEAD_V), lambda bi, h, i: (bi, i, h)),
        scratch_shapes=[
            pltpu.VMEM((n_chunks, VT_ROWS, KV_CHUNK), BF16),
            pltpu.VMEM((6, 2 * HEAD_V, TQ), BF16),
            pltpu.VMEM((2, VT_ROWS, TQ), F32),
            pltpu.VMEM((2, KV_CHUNK, TQ), F32),
            pltpu.VMEM((2, KV_CHUNK, TQ), F32),
        ],
        compiler_params=_cparams(("parallel", "parallel", "arbitrary")),
        name="diff_attn",
    )(par, q, k, v, bias_tiles, g_col)


def _postmix_kernel(x_ref, ys_ref, ya_ref, wglu_ref, bglu_ref, gs_ref, wout_ref, gf_ref, *rest, router):
    if router:
        wr_ref, br_ref, xo_ref, h_ref, gate_ref, ysc_ref = rest
    else:
        xo_ref, h_ref, ysc_ref = rest
    n_chunks = ys_ref.shape[1]
    for j in range(SSM_WIDTH // LANES):
        for s in range(SSM_CHUNK):
            ysc_ref[j, pl.ds(s, n_chunks, stride=SSM_CHUNK), :] = ys_ref[s, :, j * LANES:(j + 1) * LANES].astype(F32)
    y = jnp.concatenate([ysc_ref[j] for j in range(SSM_WIDTH // LANES)], axis=1)
    t = jnp.dot(y.astype(BF16), wglu_ref[...], preferred_element_type=F32) + bglu_ref[...]
    y = y * jax.nn.sigmoid(t)
    y = _rms(y, gs_ref[...]).astype(BF16)
    mix = jnp.dot(y, wout_ref[:SSM_WIDTH, :], preferred_element_type=F32)
    mix = mix + jnp.dot(ya_ref[...], wout_ref[SSM_WIDTH:, :], preferred_element_type=F32)
    x = x_ref[...] + mix
    xo_ref[...] = x
    h = _rms(x, gf_ref[...])
    h_ref[...] = h.astype(h_ref.dtype)
    if router:
        lane = lax.broadcasted_iota(jnp.int32, (h.shape[0], LANES), 1)
        logits = jnp.broadcast_to(br_ref[...], lane.shape)
        for e in range(N_EXPERTS):
            le = jnp.sum(h * wr_ref[e:e + 1, :], axis=-1, keepdims=True)
            logits = jnp.where(lane == e, logits + le, logits)
        big = jnp.int32(LANES)
        m1 = jnp.max(logits, axis=-1, keepdims=True)
        i1 = jnp.min(jnp.where(logits == m1, lane, big), axis=-1, keepdims=True)
        rest_l = jnp.where(lane == i1, -jnp.inf, logits)
        m2 = jnp.max(rest_l, axis=-1, keepdims=True)
        i2 = jnp.min(jnp.where(rest_l == m2, lane, big), axis=-1, keepdims=True)
        e2 = jnp.exp(m2 - m1)
        w1 = 1.0 / (1.0 + e2)
        w2 = e2 * w1
        rec = jnp.where(lane == 0, w1, jnp.where(lane == 1, w2, jnp.where(
            lane == 2, i1.astype(F32), jnp.where(lane == 3, i2.astype(F32), 0.0))))
        gate_ref[...] = rec[:, :N_EXPERTS]


def _postmix(x2, ys, ya, wglu, bglu, gs, wout, gf, router_w, tm):
    n = x2.shape[0]
    router = router_w is not None
    row = lambda c: pl.BlockSpec((tm, c), lambda i: (i, 0))
    full = lambda r, c: pl.BlockSpec((r, c), lambda i: (0, 0))
    steps = pl.BlockSpec((SSM_CHUNK, tm // SSM_CHUNK, SSM_WIDTH), lambda i: (0, i, 0))
    in_specs = [row(D_MODEL), steps, row(512), full(512, 512), full(1, 512), full(1, 512),
                full(D_MODEL, D_MODEL), full(1, D_MODEL)]
    args = [x2, ys, ya, wglu, bglu, gs, wout, gf]
    out_shape = [jax.ShapeDtypeStruct((n, D_MODEL), F32), jax.ShapeDtypeStruct((n, D_MODEL), F32 if router else BF16)]
    out_specs = [row(D_MODEL), row(D_MODEL)]
    if router:
        in_specs += [full(N_EXPERTS, D_MODEL), full(1, LANES)]
        args += list(router_w)
        out_shape.append(jax.ShapeDtypeStruct((n, N_EXPERTS), F32))
        out_specs.append(row(N_EXPERTS))
    return pl.pallas_call(
        functools.partial(_postmix_kernel, router=router),
        out_shape=tuple(out_shape),
        grid=(n // tm,),
        in_specs=in_specs,
        out_specs=tuple(out_specs),
        scratch_shapes=[pltpu.VMEM((SSM_WIDTH // LANES, tm, LANES), F32)],
        compiler_params=_cparams(("parallel",)),
        name="postmix_router" if router else "postmix",
    )(*args)


def _swiglu_partial(h, wg, wu, wd):
    g = jnp.dot(h, wg, preferred_element_type=F32)
    u = jnp.dot(h, wu, preferred_element_type=F32)
    return jnp.dot((jax.nn.silu(g) * u).astype(BF16), wd, preferred_element_type=F32)


def _ffn_kernel(h_ref, x_ref, wg_ref, wu_ref, wd_ref, o_ref):
    @pl.when(pl.program_id(1) == 0)
    def _():
        o_ref[...] = x_ref[...]

    o_ref[...] += _swiglu_partial(h_ref[...], wg_ref[...], wu_ref[...], wd_ref[...])


def _ffn(h, x2, wg, wu, wd, tm, tf):
    n = x2.shape[0]
    ff = wg.shape[1]
    row = lambda c: pl.BlockSpec((tm, c), lambda i, f: (i, 0))
    return pl.pallas_call(
        _ffn_kernel,
        out_shape=jax.ShapeDtypeStruct((n, D_MODEL), F32),
        grid=(n // tm, ff // tf),
        in_specs=[row(D_MODEL), row(D_MODEL),
                  pl.BlockSpec((D_MODEL, tf), lambda i, f: (0, f)),
                  pl.BlockSpec((D_MODEL, tf), lambda i, f: (0, f)),
                  pl.BlockSpec((tf, D_MODEL), lambda i, f: (f, 0))],
        out_specs=row(D_MODEL),
        compiler_params=_cparams(("parallel", "arbitrary")),
        name="ffn_dense",
    )(h, x2, wg, wu, wd)


def _route(rec, tm, tc):
    n = rec.shape[0]
    a = 2 * n
    n_tiles = a // tm + N_EXPERTS
    e = rec[:, 2:4].astype(jnp.int32).reshape(a)
    onehot = (e[:, None] == jnp.arange(N_EXPERTS, dtype=jnp.int32)[None, :]).astype(jnp.int32)
    csum = jnp.cumsum(onehot, axis=0)
    rank = jnp.sum(csum * onehot, axis=1) - 1
    count = csum[-1]
    padded = (count + tm - 1) // tm * tm
    ends = jnp.cumsum(padded)
    pad_start = ends - padded + count
    pad_aligned = (pad_start + SUBLANES - 1) // SUBLANES * SUBLANES
    pad = jnp.concatenate([pad_start, pad_aligned, ends - pad_aligned, ends[-1:] // tm]).astype(jnp.int32)
    pos = jnp.sum(onehot * (ends - padded)[None, :], axis=1) + rank
    n_used = (ends[-1] // tm).astype(jnp.int32)
    tile = jnp.arange(n_tiles, dtype=jnp.int32)
    texp = jnp.sum((tile[:, None] >= (ends // tm)[None, :]).astype(jnp.int32), axis=1)
    texp = jnp.minimum(texp, N_EXPERTS - 1)
    texp = jnp.where(tile < n_used, texp, texp[jnp.maximum(n_used - 1, 0)])
    pos2d = pos.reshape(n // tc, tc, 2).transpose(0, 2, 1).reshape(n // tc, 2 * tc)
    return pos2d, texp, n_used.reshape(1), pad


def _fetch_rows_index(pos_hbm, idx_smem, sem_idx):
    cp = pltpu.make_async_copy(pos_hbm.at[pl.program_id(0)], idx_smem, sem_idx)
    cp.start()
    cp.wait()


def _moe_dispatch_kernel(pad_ref, pos_hbm, h_ref, xs_hbm, idx_smem, zeros_ref, sem_idx, sem, sem_pad):
    @pl.when(pl.program_id(0) == 0)
    def _():
        zeros_ref[...] = jnp.zeros_like(zeros_ref)

        def padding_copies(act):
            zero_rows = lambda off, size: pltpu.make_async_copy(
                zeros_ref.at[pl.ds(0, size)], xs_hbm.at[pl.ds(off, size)], sem_pad)
            for e in range(N_EXPERTS):
                first = pad_ref[e]
                aligned = pad_ref[N_EXPERTS + e]
                length = pad_ref[2 * N_EXPERTS + e]
                for j in range(SUBLANES - 1):
                    @pl.when(first + j < aligned)
                    def _(row=first + j):
                        act(zero_rows(row, 1))

                off = aligned
                for bit in reversed(range(3, MOE_TILE.bit_length() - 1)):
                    size = 1 << bit

                    @pl.when((length & size) != 0)
                    def _(off=off, size=size):
                        act(zero_rows(pl.multiple_of(off, SUBLANES), size))

                    off = off + (length & size)
            n_tiles = xs_hbm.shape[0] // MOE_TILE
            half = zeros_ref.shape[0]
            for t in range(n_tiles - N_EXPERTS, n_tiles):
                @pl.when(t >= pad_ref[3 * N_EXPERTS])
                def _(t=t):
                    for part in range(MOE_TILE // half):
                        act(zero_rows(t * MOE_TILE + part * half, half))

        padding_copies(lambda cp: cp.start())
        padding_copies(lambda cp: cp.wait())

    _fetch_rows_index(pos_hbm, idx_smem, sem_idx)
    tc = h_ref.shape[0]

    def issue(g, carry):
        for j in range(ROWS_PER_ISSUE):
            r = g * ROWS_PER_ISSUE + j
            src = h_ref.at[pl.ds(r, 1)]
            pltpu.make_async_copy(src, xs_hbm.at[pl.ds(idx_smem[r], 1)], sem).start(priority=0)
            pltpu.make_async_copy(src, xs_hbm.at[pl.ds(idx_smem[tc + r], 1)], sem).start(priority=1)
        return carry

    lax.fori_loop(0, tc // ROWS_PER_ISSUE, issue, 0)
    for _ in range(2):
        pltpu.make_async_copy(h_ref, xs_hbm.at[pl.ds(0, tc)], sem).wait()


def _moe_dispatch(pad, pos2d, h, rows):
    n = h.shape[0]
    tc = pos2d.shape[1] // 2
    return pl.pallas_call(
        _moe_dispatch_kernel,
        out_shape=jax.ShapeDtypeStruct((rows, D_MODEL), F32),
        grid_spec=pltpu.PrefetchScalarGridSpec(
            num_scalar_prefetch=1,
            grid=(n // tc,),
            in_specs=[pl.BlockSpec(memory_space=pl.ANY), pl.BlockSpec((tc, D_MODEL), lambda i, pad: (i, 0))],
            out_specs=pl.BlockSpec(memory_space=pl.ANY),
            scratch_shapes=[pltpu.SMEM((2 * tc,), jnp.int32), pltpu.VMEM((MOE_TILE // 2, D_MODEL), F32),
                            pltpu.SemaphoreType.DMA, pltpu.SemaphoreType.DMA, pltpu.SemaphoreType.DMA],
        ),
        compiler_params=_cparams(("arbitrary",)),
        name="moe_dispatch",
    )(pad, pos2d, h)


def _moe_ffn_kernel(texp_ref, nused_ref, xs_ref, wg_ref, wu_ref, wd_ref, o_ref, xb_ref):
    t = pl.program_id(0)
    f = pl.program_id(1)
    last = f == pl.num_programs(1) - 1
    valid = t < nused_ref[0]

    @pl.when(valid)
    def _():
        @pl.when(f == 0)
        def _():
            xb_ref[...] = xs_ref[...].astype(BF16)
            o_ref[...] = jnp.zeros_like(o_ref)

        o_ref[...] += _swiglu_partial(xb_ref[...], wg_ref[0, 0].astype(BF16), wu_ref[0, 0].astype(BF16),
                                      wd_ref[0, 0].astype(BF16))

    @pl.when(jnp.logical_and(jnp.logical_not(valid), last))
    def _():
        o_ref[...] = jnp.zeros_like(o_ref)


def _moe_ffn(texp, n_used, xs, wg, wu, wd, layer, tm, tf):
    rows = xs.shape[0]
    ff = wg.shape[-1]
    nf = ff // tf
    fidx = lambda t, f, nu: jnp.where(t < nu[0], f, nf - 1)
    row = pl.BlockSpec((tm, D_MODEL), lambda t, f, te, nu: (t, 0))
    xs_row = pl.BlockSpec((tm, D_MODEL), lambda t, f, te, nu: (jnp.minimum(t, jnp.maximum(nu[0] - 1, 0)), 0))
    return pl.pallas_call(
        _moe_ffn_kernel,
        out_shape=jax.ShapeDtypeStruct((rows, D_MODEL), F32),
        grid_spec=pltpu.PrefetchScalarGridSpec(
            num_scalar_prefetch=2,
            grid=(rows // tm, nf),
            in_specs=[xs_row,
                      pl.BlockSpec((1, 1, D_MODEL, tf), lambda t, f, te, nu: (layer, te[t], 0, fidx(t, f, nu))),
                      pl.BlockSpec((1, 1, D_MODEL, tf), lambda t, f, te, nu: (layer, te[t], 0, fidx(t, f, nu))),
                      pl.BlockSpec((1, 1, tf, D_MODEL), lambda t, f, te, nu: (layer, te[t], fidx(t, f, nu), 0))],
            out_specs=row,
            scratch_shapes=[pltpu.VMEM((tm, D_MODEL), BF16)],
        ),
        compiler_params=_cparams(("arbitrary", "arbitrary")),
        name="moe_ffn",
    )(texp, n_used, xs, wg, wu, wd)


def _moe_combine_kernel(pos_hbm, ys_hbm, x_ref, rec_ref, gfin_ref, o_ref, idx_smem, buf, sem_idx, sem, *, final):
    _fetch_rows_index(pos_hbm, idx_smem, sem_idx)
    count = buf.shape[0]

    def issue(g, carry):
        for j in range(ROWS_PER_ISSUE):
            r = g * ROWS_PER_ISSUE + j
            pltpu.make_async_copy(ys_hbm.at[pl.ds(idx_smem[r], 1)], buf.at[pl.ds(r, 1)], sem).start(priority=j % 2)
        return carry

    lax.fori_loop(0, count // ROWS_PER_ISSUE, issue, 0)
    pltpu.make_async_copy(ys_hbm.at[pl.ds(0, count)], buf, sem).wait()
    tc = x_ref.shape[0]
    rec = rec_ref[...]
    out = x_ref[...] + rec[:, 0:1] * buf[:tc] + rec[:, 1:2] * buf[tc:]
    if final:
        out = _rms(out, gfin_ref[...])
    o_ref[...] = out


def _moe_combine(pos2d, ys, x2, rec, gfin, final):
    n = x2.shape[0]
    tc = pos2d.shape[1] // 2
    row = lambda c: pl.BlockSpec((tc, c), lambda i: (i, 0))
    return pl.pallas_call(
        functools.partial(_moe_combine_kernel, final=final),
        out_shape=jax.ShapeDtypeStruct((n, D_MODEL), F32),
        grid=(n // tc,),
        in_specs=[pl.BlockSpec(memory_space=pl.ANY), pl.BlockSpec(memory_space=pl.ANY), row(D_MODEL),
                  row(N_EXPERTS), pl.BlockSpec((1, D_MODEL), lambda i: (0, 0))],
        out_specs=row(D_MODEL),
        scratch_shapes=[pltpu.SMEM((2 * tc,), jnp.int32), pltpu.VMEM((2 * tc, D_MODEL), F32),
                        pltpu.SemaphoreType.DMA, pltpu.SemaphoreType.DMA],
        compiler_params=_cparams(("arbitrary",)),
        name="moe_combine",
    )(pos2d, ys, x2, rec, gfin)


def _pad_axis(a, axis, size):
    pad = [(0, 0)] * a.ndim
    pad[axis] = (0, size - a.shape[axis])
    return jnp.pad(a, pad)


def kernel(x, g_mix, w_in, ssm_lambda_re, ssm_lambda_im, ssm_log_step, ssm_b_re, ssm_b_im, ssm_c_re, ssm_c_im, ssm_d, w_glu, b_glu, g_ssm_out, lambda_q1, lambda_k1, lambda_q2, lambda_k2, g_subln, w_out, g_ffn, dense_w_gate, dense_w_up, dense_w_down, w_router, b_router, moe_w_gate, moe_w_up, moe_w_down, g_final):
    bsz, seq, _ = x.shape
    n = bsz * seq
    depth = w_in.shape[0]
    n_chunks = seq // SSM_CHUNK
    assert seq % KV_CHUNK == 0 and bsz == SUBLANES and depth % 2 == 0 and (2 * n) % MOE_TILE == 0
    tm = min(512, n)
    d_ff = dense_w_gate.shape[-1]
    d_ff_pad = -(-d_ff // 256) * 256
    slopes2 = [2.0 ** (-8.0 * (h + 1) / HEADS) * LOG2E for h in range(HEADS)]
    kcols, bias_tiles = _alibi_constants(slopes2)
    pmat, qmat = _regroup_matrices()

    x2 = x.reshape(n, D_MODEL)
    for i in range(depth):
        lambda_init = 0.8 - 0.6 * math.exp(-0.3 * i)
        u, q, k, v = _inproj(x2, g_mix[i][None, :], w_in[i].astype(BF16), kcols)

        m, w, vv, dec_re, dec_im = _ssm_matrices(
            ssm_lambda_re[i], ssm_lambda_im[i], ssm_log_step[i], ssm_b_re[i], ssm_b_im[i],
            ssm_c_re[i], ssm_c_im[i], ssm_d[i])
        uc = _regroup_in(u, pmat, bsz)
        yc = _ssm(uc, m, w, vv, dec_re, dec_im, n_chunks, bsz)
        ys = _regroup_out(yc, qmat, bsz)

        lam = (jnp.exp(jnp.sum(lambda_q1[i].astype(F32) * lambda_k1[i].astype(F32)))
               - jnp.exp(jnp.sum(lambda_q2[i].astype(F32) * lambda_k2[i].astype(F32))) + lambda_init)
        par = jnp.concatenate([jnp.stack([lam, jnp.asarray(1.0 - lambda_init, F32)]),
                               jnp.asarray(slopes2, F32), jnp.zeros((2,), F32)])
        ya = _attention(par, q.reshape(bsz, seq, ATTN_WIDTH), k.reshape(bsz, seq, 2 * ATTN_WIDTH),
                        v.reshape(bsz, seq, ATTN_WIDTH), bias_tiles, g_subln[i].astype(F32)[:, None])
        ya = ya.reshape(n, ATTN_WIDTH)

        j = i // 2
        if i % 2 == 0:
            router_w = None
        else:
            router_w = (w_router[j].astype(F32).T,
                        jnp.concatenate([b_router[j].astype(F32),
                                         jnp.full((LANES - N_EXPERTS,), -jnp.inf, F32)])[None, :])
        outs = _postmix(x2, ys, ya, w_glu[i].astype(BF16), b_glu[i][None, :], g_ssm_out[i][None, :],
                        w_out[i].astype(BF16), g_ffn[i][None, :], router_w, tm)
        if i % 2 == 0:
            x2, h = outs
            wg = _pad_axis(dense_w_gate[j].astype(BF16), 1, d_ff_pad)
            wu = _pad_axis(dense_w_up[j].astype(BF16), 1, d_ff_pad)
            wd = _pad_axis(dense_w_down[j].astype(BF16), 0, d_ff_pad)
            x2 = _ffn(h, x2, wg, wu, wd, tm, d_ff_pad // 2)
        else:
            x2, h, rec = outs
            pos2d, texp, n_used, pad = _route(rec, MOE_TILE, min(MOE_TOKEN_TILE, n))
            xs = _moe_dispatch(pad, pos2d, h, texp.shape[0] * MOE_TILE)
            ys = _moe_ffn(texp, n_used, xs, moe_w_gate, moe_w_up, moe_w_down, j, MOE_TILE, 512)
            x2 = _moe_combine(pos2d, ys, x2, rec, g_final[None, :], i == depth - 1)
    return x2.reshape(bsz, seq, D_MODEL)
```
